```python
import math
import jax, jax.numpy as jnp
from jax import lax
import numpy as np

D_MODEL = 1024
BATCH = 8
SEQ = 2048
DEPTH = 1
DEC_BATCH = 32
DEC_SEQ = 8
PAST_LEN = 8192
PAGE_SIZE = 128

HEAD_DIM = 64
GROUPS = ((128, 1), (512, 4), (2048, 16))
N_GROUPS = 3
HEADS_PER_GROUP = 4
N_ATTN_HEADS = N_GROUPS * HEADS_PER_GROUP
ATTN_WIDTH = N_ATTN_HEADS * HEAD_DIM
ATTN_OUT_WIDTH = HEADS_PER_GROUP * HEAD_DIM
CONV_DIM = D_MODEL // 2
CONV_WIDTH = 31
N_BUCKETS = 32
MAX_DISTANCE = 2048
D_FF = ((8 * D_MODEL // 3 + 127) // 128) * 128
N_MEM = 256
X_HEADS = 4
X_HEAD_DIM = D_MODEL // X_HEADS
EPS = 1e-6
NEG_INF = -1e30
IN_WIDTH = 3 * ATTN_WIDTH + 2 * CONV_DIM + 2 * D_MODEL
SPLIT_POINTS = (ATTN_WIDTH, 2 * ATTN_WIDTH, 3 * ATTN_WIDTH,
                3 * ATTN_WIDTH + CONV_DIM, 3 * ATTN_WIDTH + 2 * CONV_DIM,
                3 * ATTN_WIDTH + 2 * CONV_DIM + D_MODEL)

kernel_name = 'hybrid_dilated_conformer_decode_step'


def rmsnorm(x, g):
    xf = x.astype(jnp.float32)
    y = xf * lax.rsqrt(jnp.mean(xf * xf, axis=-1, keepdims=True) + EPS)
    return (y * g.astype(jnp.float32)).astype(x.dtype)


def layernorm(x, g, b):
    xf = x.astype(jnp.float32)
    mu = jnp.mean(xf, axis=-1, keepdims=True)
    var = jnp.mean(jnp.square(xf - mu), axis=-1, keepdims=True)
    y = (xf - mu) * lax.rsqrt(var + EPS)
    return (y * g.astype(jnp.float32) + b.astype(jnp.float32)).astype(x.dtype)


def swiglu(x, w_gate, w_up, w_down):
    return (jax.nn.silu(x @ w_gate) * (x @ w_up)) @ w_down


def rel_bucket(dist):
    n = jnp.maximum(dist, 0)
    max_exact = N_BUCKETS // 2
    nf = jnp.maximum(n, 1).astype(jnp.float32)
    large = max_exact + (jnp.log(nf / max_exact) / math.log(MAX_DISTANCE / max_exact)
                         * (N_BUCKETS - max_exact)).astype(jnp.int32)
    return jnp.where(n < max_exact, n, jnp.minimum(large, N_BUCKETS - 1))


def dilated_group_band(q, k, v, bias, window, dil):
    B, S, H, Dh = q.shape
    span = window // dil
    L = S // dil
    n_sub = B * dil
    blk = span
    nb = -(-L // blk)
    Lp = nb * blk

    def to_blocks(a):
        a = a.reshape(B, L, dil, H, Dh).transpose(0, 2, 1, 3, 4).reshape(n_sub, L, H, Dh)
        a = jnp.pad(a, ((0, 0), (0, Lp - L), (0, 0), (0, 0)))
        return a.reshape(n_sub, nb, blk, H, Dh)

    def with_prev(a):
        prev = jnp.pad(a, ((0, 0), (1, 0), (0, 0), (0, 0), (0, 0)))[:, :-1]
        return jnp.concatenate([prev, a], axis=2)

    qb = to_blocks(q)
    kk = with_prev(to_blocks(k))
    vv = with_prev(to_blocks(v))
    kj = jnp.arange(2 * blk)[None, :]
    delta = (jnp.arange(blk)[:, None] + blk) - kj
    in_band = (delta >= 0) & (delta <= span)
    key_idx = jnp.arange(nb)[:, None] * blk - blk + kj
    mask = in_band[None] & (key_idx[:, None, :] >= 0)
    bias_qk = jnp.transpose(bias[rel_bucket(delta * dil)], (2, 0, 1)).astype(jnp.float32)
    s = jnp.einsum('nbqhd,nbkhd->nbhqk', qb, kk).astype(jnp.float32) * (HEAD_DIM ** -0.5)
    s = jnp.where(mask[None, :, None], s + bias_qk[None, None], NEG_INF)
    lse = jax.nn.logsumexp(s, axis=-1)
    p = jnp.exp(s - lse[..., None]).astype(v.dtype)
    o = jnp.einsum('nbhqk,nbkhd->nbqhd', p, vv)
    o = o.reshape(n_sub, Lp, H, Dh)[:, :L]
    lse = lse.transpose(0, 1, 3, 2).reshape(n_sub, Lp, H)[:, :L]

    def from_sub(a):
        tail = a.shape[2:]
        return a.reshape((B, dil, L) + tail).swapaxes(1, 2).reshape((B, S) + tail)

    return from_sub(o), from_sub(lse)


def dilated_group_gather(q, k_new, v_new, k_buf, v_buf, bias, window, dil):
    T = q.shape[1]
    Lb = k_buf.shape[1]
    kc = jnp.concatenate([k_buf.astype(k_new.dtype), k_new], axis=1)
    vc = jnp.concatenate([v_buf.astype(v_new.dtype), v_new], axis=1)
    taps = jnp.arange(window // dil + 1)
    idx = (Lb + jnp.arange(T))[:, None] - taps[None, :] * dil
    valid = idx >= 0
    idx_c = jnp.maximum(idx, 0)
    kg = kc[:, idx_c]
    vg = vc[:, idx_c]
    bias_m = jnp.transpose(bias[rel_bucket(taps * dil)], (1, 0)).astype(jnp.float32)
    s = jnp.einsum('bthd,btmhd->bhtm', q, kg).astype(jnp.float32) * (HEAD_DIM ** -0.5)
    s = jnp.where(valid[None, None], s + bias_m[None, :, None, :], NEG_INF)
    lse = jax.nn.logsumexp(s, axis=-1)
    p = jnp.exp(s - lse[..., None]).astype(vg.dtype)
    o = jnp.einsum('bhtm,btmhd->bthd', p, vg)
    return o, lse.transpose(0, 2, 1), kc[:, -Lb:], vc[:, -Lb:]


def merge_groups(outs, lses, dtype):
    w = jax.nn.softmax(jnp.stack(lses, axis=0).astype(jnp.float32), axis=0)
    o = jnp.einsum('gbth,gbthd->bthd', w, jnp.stack(outs, axis=0).astype(jnp.float32))
    B, T = o.shape[:2]
    return o.reshape(B, T, ATTN_OUT_WIDTH).astype(dtype)


def dilated_attention_prompt(q, k, v, rel_bias):
    S = q.shape[1]
    outs, lses, bk, bv = [], [], [], []
    for g, (window, dil) in enumerate(GROUPS):
        bias_g = rel_bias[:, g * HEADS_PER_GROUP:(g + 1) * HEADS_PER_GROUP]
        o, lse = dilated_group_band(q[:, :, g], k[:, :, g], v[:, :, g], bias_g, window, dil)
        outs.append(o)
        lses.append(lse)
        keep = min(window, S)
        bk.append(k[:, S - keep:, g])
        bv.append(v[:, S - keep:, g])
    return merge_groups(outs, lses, q.dtype), bk, bv


def dilated_attention_sample(q, k, v, k_bufs, v_bufs, rel_bias):
    outs, lses, bk, bv = [], [], [], []
    for g, (window, dil) in enumerate(GROUPS):
        bias_g = rel_bias[:, g * HEADS_PER_GROUP:(g + 1) * HEADS_PER_GROUP]
        o, lse, nk, nv = dilated_group_gather(q[:, :, g], k[:, :, g], v[:, :, g],
                                              k_bufs[g], v_bufs[g], bias_g, window, dil)
        outs.append(o)
        lses.append(lse)
        bk.append(nk)
        bv.append(nv)
    return merge_groups(outs, lses, q.dtype), bk, bv


def conformer_conv(u, buf, p):
    C = u.shape[-1]
    ext = jnp.concatenate([buf.astype(u.dtype), u], axis=1)
    y = lax.conv_general_dilated(ext, p['conv_dw_w'][:, None, :].astype(u.dtype),
                                 window_strides=(1,), padding='VALID',
                                 dimension_numbers=('NWC', 'WIO', 'NWC'),
                                 feature_group_count=C)
    y = jax.nn.silu(layernorm(y + p['conv_dw_b'], p['conv_ln_g'], p['conv_ln_b']))
    return y @ p['w_conv_proj'], ext[:, -(CONV_WIDTH - 1):]


def memory_kv(mem, p):
    B = mem.shape[0]
    m = rmsnorm(mem, p['mem_norm']) @ p['w_xkv']
    mk, mv = jnp.split(m, 2, axis=-1)
    return (mk.reshape(B, N_MEM, X_HEADS, X_HEAD_DIM), mv.reshape(B, N_MEM, X_HEADS, X_HEAD_DIM))


def cross_attention(h, mk, mv, w_xq, w_xo):
    B, T, _ = h.shape
    q = (h @ w_xq).reshape(B, T, X_HEADS, X_HEAD_DIM)
    s = jnp.einsum('bthd,bmhd->bhtm', q, mk.astype(q.dtype)).astype(jnp.float32) * (X_HEAD_DIM ** -0.5)
    pr = jax.nn.softmax(s, axis=-1).astype(q.dtype)
    o = jnp.einsum('bhtm,bmhd->bthd', pr, mv.astype(q.dtype)).reshape(B, T, D_MODEL)
    return o @ w_xo


def decoder_layer(x, p, attn_fn, conv_buf, mk, mv):
    B, T, _ = x.shape
    x = x + 0.5 * swiglu(rmsnorm(x, p['ffn1_norm']), p['ffn1_w_gate'], p['ffn1_w_up'], p['ffn1_w_down'])
    h = rmsnorm(x, p['mix_norm'])
    z = h @ p['w_in']
    q, k, v, u_a, u_b, g_a, g_b = jnp.split(z, SPLIT_POINTS, axis=-1)
    to_heads = lambda a: a.reshape(B, T, N_GROUPS, HEADS_PER_GROUP, HEAD_DIM)
    attn, win_k, win_v = attn_fn(to_heads(q), to_heads(k), to_heads(v))
    a_branch = attn @ p['w_attn_proj']
    c_branch, conv_state = conformer_conv(u_a * jax.nn.sigmoid(u_b), conv_buf, p)
    merged = jax.nn.sigmoid(g_a) * a_branch + jax.nn.sigmoid(g_b) * c_branch
    x = x + merged @ p['w_o']
    x = x + cross_attention(rmsnorm(x, p['xattn_norm']), mk, mv, p['w_xq'], p['w_xo'])
    x = x + 0.5 * swiglu(rmsnorm(x, p['ffn2_norm']), p['ffn2_w_gate'], p['ffn2_w_up'], p['ffn2_w_down'])
    return x, win_k, win_v, conv_state


def setup_inputs(seed: int = 0) -> dict:
    key = jax.random.key(seed)
    keys = iter(jax.random.split(key, 64))
    f32 = jnp.float32

    def nrm(shape, scale):
        return jax.random.normal(next(keys), shape, f32) * scale

    def gain(shape):
        return 1.0 + nrm(shape, 0.01)

    Ld = DEPTH
    d = {}
    d['x_prompt'] = nrm((BATCH, SEQ, D_MODEL), 1.0)
    d['x_sample'] = nrm((DEC_BATCH, DEC_SEQ, D_MODEL), 1.0)
    d['mem_prompt'] = nrm((BATCH, N_MEM, D_MODEL), 1.0)
    for g, (window, _) in enumerate(GROUPS):
        lb = min(window, PAST_LEN)
        d['cache_win%d_k' % g] = nrm((Ld, DEC_BATCH, lb, HEADS_PER_GROUP, HEAD_DIM), 1.0)
        d['cache_win%d_v' % g] = nrm((Ld, DEC_BATCH, lb, HEADS_PER_GROUP, HEAD_DIM), 1.0)
    d['state_conv'] = nrm((Ld, DEC_BATCH, CONV_WIDTH - 1, CONV_DIM), 0.5)
    d['cache_mem_k'] = nrm((Ld, DEC_BATCH, N_MEM, X_HEADS, X_HEAD_DIM), 1.0)
    d['cache_mem_v'] = nrm((Ld, DEC_BATCH, N_MEM, X_HEADS, X_HEAD_DIM), 1.0)
    d['rel_bias'] = nrm((N_BUCKETS, N_ATTN_HEADS), 0.5)
    d['ffn1_norm'] = gain((Ld, D_MODEL))
    d['ffn1_w_gate'] = nrm((Ld, D_MODEL, D_FF), D_MODEL ** -0.5)
    d['ffn1_w_up'] = nrm((Ld, D_MODEL, D_FF), D_MODEL ** -0.5)
    d['ffn1_w_down'] = nrm((Ld, D_FF, D_MODEL), D_FF ** -0.5)
    d['mix_norm'] = gain((Ld, D_MODEL))
    d['w_in'] = nrm((Ld, D_MODEL, IN_WIDTH), D_MODEL ** -0.5)
    d['w_attn_proj'] = nrm((Ld, ATTN_OUT_WIDTH, D_MODEL), ATTN_OUT_WIDTH ** -0.5)
    d['conv_dw_w'] = nrm((Ld, CONV_WIDTH, CONV_DIM), CONV_WIDTH ** -0.5)
    d['conv_dw_b'] = nrm((Ld, CONV_DIM), 0.01)
    d['conv_ln_g'] = gain((Ld, CONV_DIM))
    d['conv_ln_b'] = nrm((Ld, CONV_DIM), 0.01)
    d['w_conv_proj'] = nrm((Ld, CONV_DIM, D_MODEL), CONV_DIM ** -0.5)
    d['w_o'] = nrm((Ld, D_MODEL, D_MODEL), D_MODEL ** -0.5)
    d['xattn_norm'] = gain((Ld, D_MODEL))
    d['mem_norm'] = gain((Ld, D_MODEL))
    d['w_xq'] = nrm((Ld, D_MODEL, D_MODEL), D_MODEL ** -0.5)
    d['w_xkv'] = nrm((Ld, D_MODEL, 2 * D_MODEL), D_MODEL ** -0.5)
    d['w_xo'] = nrm((Ld, D_MODEL, D_MODEL), D_MODEL ** -0.5)
    d['ffn2_norm'] = gain((Ld, D_MODEL))
    d['ffn2_w_gate'] = nrm((Ld, D_MODEL, D_FF), D_MODEL ** -0.5)
    d['ffn2_w_up'] = nrm((Ld, D_MODEL, D_FF), D_MODEL ** -0.5)
    d['ffn2_w_down'] = nrm((Ld, D_FF, D_MODEL), D_FF ** -0.5)
    d['final_norm'] = gain((D_MODEL,))
    return d


def reference(x_prompt, x_sample, mem_prompt,
              cache_win0_k, cache_win0_v, cache_win1_k, cache_win1_v, cache_win2_k, cache_win2_v,
              state_conv, cache_mem_k, cache_mem_v, rel_bias,
              ffn1_norm, ffn1_w_gate, ffn1_w_up, ffn1_w_down, mix_norm, w_in, w_attn_proj,
              conv_dw_w, conv_dw_b, conv_ln_g, conv_ln_b, w_conv_proj, w_o,
              xattn_norm, mem_norm, w_xq, w_xkv, w_xo,
              ffn2_norm, ffn2_w_gate, ffn2_w_up, ffn2_w_down, final_norm):
    stacked = dict(ffn1_norm=ffn1_norm, ffn1_w_gate=ffn1_w_gate, ffn1_w_up=ffn1_w_up,
                   ffn1_w_down=ffn1_w_down, mix_norm=mix_norm, w_in=w_in, w_attn_proj=w_attn_proj,
                   conv_dw_w=conv_dw_w, conv_dw_b=conv_dw_b, conv_ln_g=conv_ln_g, conv_ln_b=conv_ln_b,
                   w_conv_proj=w_conv_proj, w_o=w_o, xattn_norm=xattn_norm, mem_norm=mem_norm,
                   w_xq=w_xq, w_xkv=w_xkv, w_xo=w_xo, ffn2_norm=ffn2_norm,
                   ffn2_w_gate=ffn2_w_gate, ffn2_w_up=ffn2_w_up, ffn2_w_down=ffn2_w_down)
    cache_k = (cache_win0_k, cache_win1_k, cache_win2_k)
    cache_v = (cache_win0_v, cache_win1_v, cache_win2_v)
    xp, xs = x_prompt, x_sample
    pk, pv, pc, pmk, pmv = [], [], [], [], []
    sk, sv, sc = [], [], []
    for l in range(DEPTH):
        p = {name: arr[l] for name, arr in stacked.items()}
        mk_p, mv_p = memory_kv(mem_prompt, p)
        zero_buf = jnp.zeros((xp.shape[0], CONV_WIDTH - 1, CONV_DIM), xp.dtype)
        xp, wk, wv, cst = decoder_layer(
            xp, p, lambda q, k, v: dilated_attention_prompt(q, k, v, rel_bias), zero_buf, mk_p, mv_p)
        pk.append(wk); pv.append(wv); pc.append(cst); pmk.append(mk_p); pmv.append(mv_p)
        bk = [c[l] for c in cache_k]
        bv = [c[l] for c in cache_v]
        xs, wk, wv, cst = decoder_layer(
            xs, p, lambda q, k, v, bk=bk, bv=bv: dilated_attention_sample(q, k, v, bk, bv, rel_bias),
            state_conv[l], cache_mem_k[l], cache_mem_v[l])
        sk.append(wk); sv.append(wv); sc.append(cst)
    y_prompt = rmsnorm(xp, final_norm)
    y_sample = rmsnorm(xs, final_norm)
    stk = lambda lst, g: jnp.stack([e[g] for e in lst], axis=0)
    return (y_prompt, y_sample,
            stk(pk, 0), stk(pv, 0), stk(pk, 1), stk(pv, 1), stk(pk, 2), stk(pv, 2),
            jnp.stack(pc, axis=0), jnp.stack(pmk, axis=0), jnp.stack(pmv, axis=0),
            stk(sk, 0), stk(sv, 0), stk(sk, 1), stk(sv, 1), stk(sk, 2), stk(sv, 2),
            jnp.stack(sc, axis=0))
```

```python
import functools
import math

import jax
import jax.numpy as jnp
import numpy as np
from jax import lax
from jax.experimental import pallas as pl
from jax.experimental.pallas import tpu as pltpu

D_MODEL = 1024
HEAD_DIM = 64
GROUPS = ((128, 1), (512, 4), (2048, 16))
HEADS_PER_GROUP = 4
GROUP_WIDTH = HEADS_PER_GROUP * HEAD_DIM
ATTN_WIDTH = len(GROUPS) * GROUP_WIDTH
CONV_DIM = D_MODEL // 2
CONV_WIDTH = 31
CONV_HALO = 32
N_BUCKETS = 32
MAX_DISTANCE = 2048
D_FF = ((8 * D_MODEL // 3 + 127) // 128) * 128
N_MEM = 256
X_HEADS = 4
X_HEAD_DIM = D_MODEL // X_HEADS
EPS = 1e-6
NEG_INF = -1e30
SPAN = 128

V7X_VMEM_LIMIT_BYTES = 56 * 1024 * 1024
BF16 = jnp.bfloat16
F32 = jnp.float32


def _params(n_axes):
    return pltpu.CompilerParams(dimension_semantics=("parallel",) * n_axes,
                                vmem_limit_bytes=V7X_VMEM_LIMIT_BYTES)


def _const_spec(shape):
    return pl.BlockSpec(shape, lambda *_: (0,) * len(shape), pipeline_mode=pl.Buffered(1))


def _rows_spec(tm, width):
    return pl.BlockSpec((tm, width), lambda i: (i, 0))


def _rms(x, g):
    return x * lax.rsqrt(jnp.mean(x * x, axis=-1, keepdims=True) + EPS) * g


def _dot(a, b):
    return jnp.dot(a, b, preferred_element_type=F32)


def _dot_nt(a, b):
    return lax.dot_general(a, b, (((1,), (1,)), ((), ())), preferred_element_type=F32)


def _head_masks(width=GROUP_WIDTH):
    lane = lax.broadcasted_iota(jnp.int32, (1, width), 1)
    return [(lane >= h * HEAD_DIM) & (lane < (h + 1) * HEAD_DIM) for h in range(HEADS_PER_GROUP)]


def _ffn_kernel(*refs, pre_proj, final_norm):
    refs = list(refs)
    x_ref = refs.pop(0)
    if pre_proj:
        a_ref, wp_ref = refs.pop(0), refs.pop(0)
    g_ref, wg_ref, wu_ref, wd_ref = refs.pop(0), refs.pop(0), refs.pop(0), refs.pop(0)
    if final_norm:
        fg_ref = refs.pop(0)
    o_ref = refs.pop(0)

    x = x_ref[...]
    if pre_proj:
        x = x + _dot(a_ref[...], wp_ref[...])
    h = _rms(x, g_ref[...]).astype(BF16)
    gate = _dot(h, wg_ref[...])
    up = _dot(h, wu_ref[...])
    act = (gate * jax.nn.sigmoid(gate) * up).astype(BF16)
    x = x + 0.5 * _dot(act, wd_ref[...])
    if final_norm:
        x = _rms(x, fg_ref[...])
    o_ref[...] = x


def _ffn(x, norm_g, wg, wu, wd, *, tm, pre=None, final_g=None):
    rows = x.shape[0]
    args = [x]
    specs = [_rows_spec(tm, D_MODEL)]
    if pre is not None:
        a, wp = pre
        args += [a, wp]
        specs += [_rows_spec(tm, D_MODEL), _const_spec((D_MODEL, D_MODEL))]
    args += [norm_g, wg, wu, wd]
    specs += [_const_spec((1, D_MODEL)), _const_spec((D_MODEL, D_FF)), _const_spec((D_MODEL, D_FF)),
              _const_spec((D_FF, D_MODEL))]
    if final_g is not None:
        args.append(final_g)
        specs.append(_const_spec((1, D_MODEL)))
    return pl.pallas_call(
        functools.partial(_ffn_kernel, pre_proj=pre is not None, final_norm=final_g is not None),
        grid=(rows // tm,),
        in_specs=specs,
        out_specs=_rows_spec(tm, D_MODEL),
        out_shape=jax.ShapeDtypeStruct((rows, D_MODEL), F32),
        compiler_params=_params(1),
        name="ffn",
    )(*args)


_Q_END = ATTN_WIDTH
_K_END = 2 * ATTN_WIDTH
_V_END = 3 * ATTN_WIDTH
_UA_END = _V_END + CONV_DIM
_UB_END = _UA_END + CONV_DIM
_GA_END = _UB_END + D_MODEL
IN_WIDTH = _GA_END + D_MODEL


def _win_kernel(x_ref, g_ref, w_ref, q0, q1, q2, k0, k1, k2, v0, v1, v2, u_ref, ga_ref, gb_ref):
    h = _rms(x_ref[...], g_ref[...]).astype(BF16)

    def seg(lo, hi):
        return _dot(h, w_ref[:, lo:hi])

    for g, (q_ref, k_ref, v_ref) in enumerate(((q0, k0, v0), (q1, k1, v1), (q2, k2, v2))):
        lo = g * GROUP_WIDTH
        q_ref[...] = (seg(lo, lo + GROUP_WIDTH) * (HEAD_DIM ** -0.5)).astype(BF16)
        k_ref[...] = seg(_Q_END + lo, _Q_END + lo + GROUP_WIDTH)
        v_ref[...] = seg(_K_END + lo, _K_END + lo + GROUP_WIDTH)
    u_ref[...] = seg(_V_END, _UA_END) * jax.nn.sigmoid(seg(_UA_END, _UB_END))
    ga_ref[...] = jax.nn.sigmoid(seg(_UB_END, _GA_END))
    gb_ref[...] = jax.nn.sigmoid(seg(_GA_END, IN_WIDTH))


def _win(x, norm_g, w_in, *, tm):
    rows = x.shape[0]
    grp = lambda dt: jax.ShapeDtypeStruct((rows, GROUP_WIDTH), dt)
    out_shape = ([grp(BF16)] * 3 + [grp(F32)] * 6
                 + [jax.ShapeDtypeStruct((rows, CONV_DIM), F32)]
                 + [jax.ShapeDtypeStruct((rows, D_MODEL), F32)] * 2)
    out_specs = ([_rows_spec(tm, GROUP_WIDTH)] * 9 + [_rows_spec(tm, CONV_DIM)]
                 + [_rows_spec(tm, D_MODEL)] * 2)
    return pl.pallas_call(
        _win_kernel,
        grid=(rows // tm,),
        in_specs=[_rows_spec(tm, D_MODEL), _const_spec((1, D_MODEL)), _const_spec((D_MODEL, IN_WIDTH))],
        out_specs=out_specs,
        out_shape=out_shape,
        compiler_params=_params(1),
        name="w_in",
    )(x, norm_g, w_in)


def _rel_bucket(dist):
    n = jnp.maximum(dist, 0)
    max_exact = N_BUCKETS // 2
    nf = jnp.maximum(n, 1).astype(F32)
    large = max_exact + (jnp.log(nf / max_exact) / math.log(MAX_DISTANCE / max_exact)
                         * (N_BUCKETS - max_exact)).astype(jnp.int32)
    return jnp.where(n < max_exact, n, jnp.minimum(large, N_BUCKETS - 1))


def _tap_bias(rel_bias, g, dil):
    bias_g = rel_bias[:, g * HEADS_PER_GROUP:(g + 1) * HEADS_PER_GROUP]
    return bias_g[_rel_bucket(jnp.arange(SPAN + 1) * dil)].astype(F32)


def _expand_bias(tap_bias, tap_idx, valid):
    b = jnp.transpose(tap_bias[np.where(valid, tap_idx, 0)], (2, 0, 1))
    return jnp.where(valid[None], b, NEG_INF)


def _band_bias(tap_bias):
    delta = (np.arange(SPAN)[:, None] + SPAN) - np.arange(2 * SPAN)[None, :]
    return _expand_bias(tap_bias, delta, (delta >= 0) & (delta <= SPAN))


def _band_attn_kernel(q_ref, k_ref, v_ref, b_ref, o_ref, l_ref, kb_ref, vb_ref, *, dil, seq):
    n_blk = seq // SPAN
    kb_ref[...] = k_ref[0].astype(BF16)
    vb_ref[...] = v_ref[0].astype(BF16)
    masks = _head_masks()

    def block(q, kb, vb, key_lo):
        o = jnp.zeros((SPAN, GROUP_WIDTH), F32)
        lse = jnp.zeros((SPAN, GROUP_WIDTH), F32)
        for h in range(HEADS_PER_GROUP):
            s = _dot_nt(jnp.where(masks[h], q, jnp.zeros_like(q)), kb) + b_ref[h, :, key_lo:]
            m = jnp.max(s, axis=-1, keepdims=True)
            p = jnp.exp(s - m)
            den = jnp.sum(p, axis=-1, keepdims=True)
            acc = _dot(p.astype(BF16), vb)
            o = jnp.where(masks[h], acc * (1.0 / den), o)
            lse = jnp.where(masks[h], m + jnp.log(den), lse)
        return o, lse

    for r in range(dil):
        lanes = slice(r * GROUP_WIDTH, (r + 1) * GROUP_WIDTH)
        o, lse = block(q_ref[0, 0:SPAN, lanes], kb_ref[0:SPAN, lanes], vb_ref[0:SPAN, lanes], SPAN)
        o_ref[0, 0:SPAN, lanes] = o
        l_ref[0, 0:SPAN, lanes] = lse

        if n_blk > 1:
            def body(j, carry, lanes=lanes):
                q_lo = pl.multiple_of(j * SPAN, SPAN)
                k_lo = pl.multiple_of((j - 1) * SPAN, SPAN)
                o, lse = block(q_ref[0, pl.ds(q_lo, SPAN), lanes], kb_ref[pl.ds(k_lo, 2 * SPAN), lanes],
                               vb_ref[pl.ds(k_lo, 2 * SPAN), lanes], 0)
                o_ref[0, pl.ds(q_lo, SPAN), lanes] = o
                l_ref[0, pl.ds(q_lo, SPAN), lanes] = lse
                return carry
            lax.fori_loop(1, n_blk, body, 0)


def _band_attn(q, k, v, bias, *, batch, seq_len, dil):
    seq = seq_len // dil
    width = dil * GROUP_WIDTH
    view = lambda a: a.reshape(batch, seq, width)
    spec = pl.BlockSpec((1, seq, width), lambda b: (b, 0, 0))
    o, lse = pl.pallas_call(
        functools.partial(_band_attn_kernel, dil=dil, seq=seq),
        grid=(batch,),
        in_specs=[spec, spec, spec, _const_spec((HEADS_PER_GROUP, SPAN, 2 * SPAN))],
        out_specs=[spec, spec],
        out_shape=[jax.ShapeDtypeStruct((batch, seq, width), F32)] * 2,
        scratch_shapes=[pltpu.VMEM((seq, width), BF16), pltpu.VMEM((seq, width), BF16)],
        compiler_params=_params(1),
        name="band_attn_d%d" % dil,
    )(view(q), view(k), view(v), bias)
    rows = batch * seq_len
    return o.reshape(rows, GROUP_WIDTH), lse.reshape(rows, GROUP_WIDTH)


def _cache_attn_kernel(*refs, n_new):
    n_g = len(GROUPS)
    q_refs, kn_refs, vn_refs = refs[0:n_g], refs[n_g:2 * n_g], refs[2 * n_g:3 * n_g]
    kc_refs, vc_refs = refs[3 * n_g:4 * n_g], refs[4 * n_g:5 * n_g]
    bc_refs, bn_refs = refs[5 * n_g:6 * n_g], refs[6 * n_g:7 * n_g]
    outs = refs[7 * n_g:]
    ko_refs, vo_refs = outs[0:n_g], outs[n_g:2 * n_g]
    o_refs, l_refs = outs[2 * n_g:3 * n_g], outs[3 * n_g:4 * n_g]
    masks = _head_masks()
    pad = jnp.zeros((SPAN - n_new, GROUP_WIDTH), F32)

    for g in range(n_g):
        kc, vc = kc_refs[g][0], vc_refs[g][0]
        kn, vn = kn_refs[g][0], vn_refs[g][0]
        cache_len = kc.shape[0]
        for src, new, dst in ((kc, kn, ko_refs[g]), (vc, vn, vo_refs[g])):
            dst[0, 0:cache_len - n_new, :] = src[n_new:cache_len]
            dst[0, cache_len - n_new:cache_len, :] = new
        q = q_refs[g][0].astype(F32)
        qm = jnp.concatenate([jnp.where(masks[h], q, 0.0) for h in range(HEADS_PER_GROUP)], axis=0).astype(BF16)
        kn_p = jnp.concatenate([kn, pad], axis=0).astype(BF16)
        vn_p = jnp.concatenate([vn, pad], axis=0).astype(BF16)
        s_c = _dot_nt(qm, kc.astype(BF16)) + bc_refs[g][...]
        s_n = _dot_nt(qm, kn_p) + bn_refs[g][...]
        m = jnp.maximum(jnp.max(s_c, axis=-1, keepdims=True), jnp.max(s_n, axis=-1, keepdims=True))
        p_c = jnp.exp(s_c - m)
        p_n = jnp.exp(s_n - m)
        den = jnp.sum(p_c, axis=-1, keepdims=True) + jnp.sum(p_n, axis=-1, keepdims=True)
        acc = _dot(p_c.astype(BF16), vc.astype(BF16)) + _dot(p_n.astype(BF16), vn_p)
        acc = acc * (1.0 / den)
        lse_rows = m + jnp.log(den)
        o = jnp.zeros((n_new, GROUP_WIDTH), F32)
        lse = jnp.zeros((n_new, GROUP_WIDTH), F32)
        for h in range(HEADS_PER_GROUP):
            rows = slice(h * n_new, (h + 1) * n_new)
            o = jnp.where(masks[h], acc[rows], o)
            lse = jnp.where(masks[h], lse_rows[rows], lse)
        o_refs[g][0] = o
        l_refs[g][0] = lse


def _cache_bias(tap_bias, dil, cache_len, n_new):
    t = np.arange(n_new)[:, None]
    dist_c = cache_len + t - np.arange(cache_len)[None, :]
    valid_c = (dist_c % dil == 0) & (dist_c // dil <= SPAN)
    dist_n = t - np.arange(SPAN)[None, :]
    valid_n = (dist_n >= 0) & (dist_n % dil == 0) & (np.arange(SPAN)[None, :] < n_new)
    flat = lambda b: b.reshape(HEADS_PER_GROUP * n_new, -1)
    return (flat(_expand_bias(tap_bias, dist_c // dil, valid_c)),
            flat(_expand_bias(tap_bias, dist_n // dil, valid_n)))


def _cache_attn(q, k_new, v_new, k_cache, v_cache, tap_biases, *, batch, n_new):
    n_g = len(GROUPS)
    new3 = lambda a: a.reshape(batch, n_new, GROUP_WIDTH)
    new_spec = pl.BlockSpec((1, n_new, GROUP_WIDTH), lambda b: (b, 0, 0))
    cache_specs = [pl.BlockSpec((1, c.shape[1], GROUP_WIDTH), lambda b: (b, 0, 0)) for c in k_cache]
    biases_c, biases_n = [], []
    for g, (_, dil) in enumerate(GROUPS):
        bc, bn = _cache_bias(tap_biases[g], dil, k_cache[g].shape[1], n_new)
        biases_c.append(bc)
        biases_n.append(bn)
    args = ([new3(a) for a in q] + [new3(a) for a in k_new] + [new3(a) for a in v_new]
            + list(k_cache) + list(v_cache) + biases_c + biases_n)
    in_specs = ([new_spec] * (3 * n_g) + cache_specs * 2
                + [_const_spec(b.shape) for b in biases_c] + [_const_spec(b.shape) for b in biases_n])
    out_shape = ([jax.ShapeDtypeStruct(c.shape, F32) for c in k_cache] * 2
                 + [jax.ShapeDtypeStruct((batch, n_new, GROUP_WIDTH), F32)] * (2 * n_g))
    out_specs = cache_specs * 2 + [new_spec] * (2 * n_g)
    outs = pl.pallas_call(
        functools.partial(_cache_attn_kernel, n_new=n_new),
        grid=(batch,),
        in_specs=in_specs,
        out_specs=out_specs,
        out_shape=out_shape,
        compiler_params=_params(1),
        name="cache_attn",
    )(*args)
    flat = lambda a: a.reshape(batch * n_new, GROUP_WIDTH)
    return (outs[0:n_g], outs[n_g:2 * n_g],
            [flat(a) for a in outs[2 * n_g:3 * n_g]], [flat(a) for a in outs[3 * n_g:4 * n_g]])


def _conv_taps(ext_ref, w_ref, first_row, n_rows):
    acc = jnp.zeros((n_rows, CONV_DIM), F32)
    for j in range(CONV_WIDTH):
        acc = acc + ext_ref[pl.ds(first_row + j, n_rows), :] * w_ref[j:j + 1, :]
    return acc


def _conv_prompt_kernel(u_ref, halo_ref, w_ref, y_ref, ext_ref, *, tc):
    i = pl.program_id(1)
    ext_ref[0:CONV_HALO, :] = jnp.where(i > 0, halo_ref[0], 0.0)
    ext_ref[CONV_HALO:CONV_HALO + tc, :] = u_ref[0]
    y_ref[0] = _conv_taps(ext_ref, w_ref, CONV_HALO - (CONV_WIDTH - 1), tc)


def _conv_prompt(u, w, *, batch, seq_len, tc):
    u3 = u.reshape(batch, seq_len, CONV_DIM)
    halo_per_tile = tc // CONV_HALO
    y = pl.pallas_call(
        functools.partial(_conv_prompt_kernel, tc=tc),
        grid=(batch, seq_len // tc),
        in_specs=[pl.BlockSpec((1, tc, CONV_DIM), lambda b, i: (b, i, 0)),
                  pl.BlockSpec((1, CONV_HALO, CONV_DIM),
                               lambda b, i: (b, jnp.maximum(i * halo_per_tile - 1, 0), 0)),
                  pl.BlockSpec((CONV_WIDTH, CONV_DIM), lambda b, i: (0, 0))],
        out_specs=pl.BlockSpec((1, tc, CONV_DIM), lambda b, i: (b, i, 0)),
        out_shape=jax.ShapeDtypeStruct((batch, seq_len, CONV_DIM), F32),
        scratch_shapes=[pltpu.VMEM((CONV_HALO + tc, CONV_DIM), F32)],
        compiler_params=_params(2),
        name="conv_prompt",
    )(u3, u3, w)
    return y.reshape(batch * seq_len, CONV_DIM)


def _conv_sample_kernel(u_ref, st_ref, w_ref, y_ref, so_ref, ext_ref, *, n_new):
    hist = CONV_WIDTH - 1
    ext_ref[0:hist, :] = st_ref[0]
    ext_ref[hist:hist + n_new, :] = u_ref[0]
    y_ref[0] = _conv_taps(ext_ref, w_ref, 0, n_new)
    so_ref[0] = ext_ref[n_new:n_new + hist, :]


def _conv_sample(u, state, w, *, batch, n_new):
    hist = CONV_WIDTH - 1
    u3 = u.reshape(batch, n_new, CONV_DIM)
    y, st = pl.pallas_call(
        functools.partial(_conv_sample_kernel, n_new=n_new),
        grid=(batch,),
        in_specs=[pl.BlockSpec((1, n_new, CONV_DIM), lambda b: (b, 0, 0)),
                  pl.BlockSpec((1, hist, CONV_DIM), lambda b: (b, 0, 0)),
                  pl.BlockSpec((CONV_WIDTH, CONV_DIM), lambda b: (0, 0))],
        out_specs=[pl.BlockSpec((1, n_new, CONV_DIM), lambda b: (b, 0, 0)),
                   pl.BlockSpec((1, hist, CONV_DIM), lambda b: (b, 0, 0))],
        out_shape=[jax.ShapeDtypeStruct((batch, n_new, CONV_DIM), F32),
                   jax.ShapeDtypeStruct((batch, hist, CONV_DIM), F32)],
        scratch_shapes=[pltpu.VMEM((hist + n_new + 2, CONV_DIM), F32)],
        compiler_params=_params(1),
        name="conv_sample",
    )(u3, state, w)
    return y.reshape(batch * n_new, CONV_DIM), st


def _mix_kernel(o0, o1, o2, l0, l1, l2, yc_ref, ga_ref, gb_ref, x_ref,
                wap_ref, cb_ref, lg_ref, lb_ref, wcp_ref, wo_ref, xg_ref, wxq_ref,
                x_out, q_out):
    l_0, l_1, l_2 = l0[...], l1[...], l2[...]
    m = jnp.maximum(jnp.maximum(l_0, l_1), l_2)
    e0, e1, e2 = jnp.exp(l_0 - m), jnp.exp(l_1 - m), jnp.exp(l_2 - m)
    attn = (e0 * o0[...] + e1 * o1[...] + e2 * o2[...]) * (1.0 / (e0 + e1 + e2))
    a_branch = _dot(attn.astype(BF16), wap_ref[...])
    y = yc_ref[...] + cb_ref[...]
    mu = jnp.mean(y, axis=-1, keepdims=True)
    yc = y - mu
    var = jnp.mean(yc * yc, axis=-1, keepdims=True)
    y = yc * lax.rsqrt(var + EPS) * lg_ref[...] + lb_ref[...]
    y = y * jax.nn.sigmoid(y)
    c_branch = _dot(y.astype(BF16), wcp_ref[...])
    merged = ga_ref[...] * a_branch + gb_ref[...] * c_branch
    x = x_ref[...] + _dot(merged.astype(BF16), wo_ref[...])
    x_out[...] = x
    hq = _rms(x, xg_ref[...]).astype(BF16)
    q_out[...] = (_dot(hq, wxq_ref[...]) * (X_HEAD_DIM ** -0.5)).astype(BF16)


def _mix(o, lse, yconv, ga, gb, x, w, *, tm):
    rows = x.shape[0]
    grp = _rows_spec(tm, GROUP_WIDTH)
    wide = _rows_spec(tm, D_MODEL)
    return pl.pallas_call(
        _mix_kernel,
        grid=(rows // tm,),
        in_specs=[grp] * 6 + [_rows_spec(tm, CONV_DIM), wide, wide, wide,
                              _const_spec((GROUP_WIDTH, D_MODEL)), _const_spec((1, CONV_DIM)),
                              _const_spec((1, CONV_DIM)), _const_spec((1, CONV_DIM)),
                              _const_spec((CONV_DIM, D_MODEL)), _const_spec((D_MODEL, D_MODEL)),
                              _const_spec((1, D_MODEL)), _const_spec((D_MODEL, D_MODEL))],
        out_specs=[wide, wide],
        out_shape=[jax.ShapeDtypeStruct((rows, D_MODEL), F32), jax.ShapeDtypeStruct((rows, D_MODEL), BF16)],
        compiler_params=_params(1),
        name="mix",
    )(*o, *lse, yconv, ga, gb, x, w['w_attn_proj'], w['conv_dw_b'], w['conv_ln_g'], w['conv_ln_b'],
      w['w_conv_proj'], w['w_o'], w['xattn_norm'], w['w_xq'])


def _memkv_kernel(m_ref, g_ref, w_ref, k_ref, v_ref):
    h = _rms(m_ref[...], g_ref[...]).astype(BF16)
    k_ref[...] = _dot(h, w_ref[:, 0:D_MODEL])
    v_ref[...] = _dot(h, w_ref[:, D_MODEL:2 * D_MODEL])


def _memkv(mem, norm_g, w_xkv, *, tm):
    rows = mem.shape[0]
    wide = _rows_spec(tm, D_MODEL)
    return pl.pallas_call(
        _memkv_kernel,
        grid=(rows // tm,),
        in_specs=[wide, _const_spec((1, D_MODEL)), _const_spec((D_MODEL, 2 * D_MODEL))],
        out_specs=[wide, wide],
        out_shape=[jax.ShapeDtypeStruct((rows, D_MODEL), F32)] * 2,
        compiler_params=_params(1),
        name="memory_kv",
    )(mem, norm_g, w_xkv)


def _xattn_kernel(q_ref, k_ref, v_ref, o_ref):
    for h in range(X_HEADS):
        lanes = slice(h * X_HEAD_DIM, (h + 1) * X_HEAD_DIM)
        s = _dot_nt(q_ref[0, :, lanes], k_ref[0, :, lanes].astype(BF16))
        m = jnp.max(s, axis=-1, keepdims=True)
        p = jnp.exp(s - m)
        den = jnp.sum(p, axis=-1, keepdims=True)
        acc = _dot(p.astype(BF16), v_ref[0, :, lanes].astype(BF16))
        o_ref[0, :, lanes] = (acc * (1.0 / den)).astype(BF16)


def _xattn(q, mk, mv, *, batch, tq):
    rpb = q.shape[1]
    q_spec = pl.BlockSpec((1, tq, D_MODEL), lambda b, i: (b, i, 0))
    m_spec = pl.BlockSpec((1, N_MEM, D_MODEL), lambda b, i: (b, 0, 0))
    return pl.pallas_call(
        _xattn_kernel,
        grid=(batch, rpb // tq),
        in_specs=[q_spec, m_spec, m_spec],
        out_specs=q_spec,
        out_shape=jax.ShapeDtypeStruct((batch, rpb, D_MODEL), BF16),
        compiler_params=_params(2),
        name="cross_attn",
    )(q, mk, mv)


def _layer_tail(x1, o, lse, yconv, ga, gb, mk, mv, w, final_g, *, batch, tm, tq):
    rows = x1.shape[0]
    x2, xq = _mix(o, lse, yconv, ga, gb, x1, w, tm=tm)
    xo = _xattn(xq.reshape(batch, rows // batch, D_MODEL), mk, mv, batch=batch, tq=tq)
    return _ffn(x2, w['ffn2_norm'], w['ffn2_w_gate'], w['ffn2_w_up'], w['ffn2_w_down'], tm=tm,
                pre=(xo.reshape(rows, D_MODEL), w['w_xo']), final_g=final_g)


def kernel(x_prompt, x_sample, mem_prompt, cache_win0_k, cache_win0_v, cache_win1_k, cache_win1_v, cache_win2_k, cache_win2_v, state_conv, cache_mem_k, cache_mem_v, rel_bias, ffn1_norm, ffn1_w_gate, ffn1_w_up, ffn1_w_down, mix_norm, w_in, w_attn_proj, conv_dw_w, conv_dw_b, conv_ln_g, conv_ln_b, w_conv_proj, w_o, xattn_norm, mem_norm, w_xq, w_xkv, w_xo, ffn2_norm, ffn2_w_gate, ffn2_w_up, ffn2_w_down, final_norm):
    batch, seq_len, _ = x_prompt.shape
    dec_batch, dec_seq, _ = x_sample.shape
    assert ffn1_norm.shape[0] == 1, "single layer"
    mat = lambda a: a[0].astype(BF16)
    vec = lambda a: a[0].reshape(1, -1)
    w = dict(ffn1_norm=vec(ffn1_norm), ffn1_w_gate=mat(ffn1_w_gate), ffn1_w_up=mat(ffn1_w_up),
             ffn1_w_down=mat(ffn1_w_down), mix_norm=vec(mix_norm), w_in=mat(w_in),
             w_attn_proj=mat(w_attn_proj), conv_dw_w=conv_dw_w[0], conv_dw_b=vec(conv_dw_b),
             conv_ln_g=vec(conv_ln_g), conv_ln_b=vec(conv_ln_b), w_conv_proj=mat(w_conv_proj),
             w_o=mat(w_o), xattn_norm=vec(xattn_norm), mem_norm=vec(mem_norm), w_xq=mat(w_xq),
             w_xkv=mat(w_xkv), w_xo=mat(w_xo), ffn2_norm=vec(ffn2_norm), ffn2_w_gate=mat(ffn2_w_gate),
             ffn2_w_up=mat(ffn2_w_up), ffn2_w_down=mat(ffn2_w_down))
    final_g = final_norm.reshape(1, -1)
    tap_biases = [_tap_bias(rel_bias, g, dil) for g, (_, dil) in enumerate(GROUPS)]

    def head(x, tm):
        x1 = _ffn(x, w['ffn1_norm'], w['ffn1_w_gate'], w['ffn1_w_up'], w['ffn1_w_down'], tm=tm)
        outs = _win(x1, w['mix_norm'], w['w_in'], tm=tm)
        return x1, outs[0:3], outs[3:6], outs[6:9], outs[9], outs[10], outs[11]

    rows_p = batch * seq_len
    tm_p = 512
    x1, q, k, v, u, ga, gb = head(x_prompt.reshape(rows_p, D_MODEL), tm_p)
    o, lse = [], []
    for g, (_, dil) in enumerate(GROUPS):
        o_g, l_g = _band_attn(q[g], k[g], v[g], _band_bias(tap_biases[g]), batch=batch, seq_len=seq_len, dil=dil)
        o.append(o_g)
        lse.append(l_g)
    yconv = _conv_prompt(u, w['conv_dw_w'], batch=batch, seq_len=seq_len, tc=256)
    mk_p, mv_p = _memkv(mem_prompt.reshape(batch * N_MEM, D_MODEL), w['mem_norm'], w['w_xkv'], tm=512)
    mk_p = mk_p.reshape(batch, N_MEM, D_MODEL)
    mv_p = mv_p.reshape(batch, N_MEM, D_MODEL)
    y_prompt = _layer_tail(x1, o, lse, yconv, ga, gb, mk_p, mv_p, w, final_g, batch=batch, tm=tm_p, tq=512)

    heads5 = lambda a, b, n: a.reshape(1, b, n, HEADS_PER_GROUP, HEAD_DIM)
    p_win = []
    for g, (window, _) in enumerate(GROUPS):
        keep = min(window, seq_len)
        for a in (k[g], v[g]):
            p_win.append(heads5(a.reshape(batch, seq_len, GROUP_WIDTH)[:, seq_len - keep:], batch, keep))
    p_conv = u.reshape(batch, seq_len, CONV_DIM)[:, seq_len - (CONV_WIDTH - 1):][None]
    p_mem_k = mk_p.reshape(1, batch, N_MEM, X_HEADS, X_HEAD_DIM)
    p_mem_v = mv_p.reshape(1, batch, N_MEM, X_HEADS, X_HEAD_DIM)

    rows_s = dec_batch * dec_seq
    x1, q, k, v, u, ga, gb = head(x_sample.reshape(rows_s, D_MODEL), rows_s)
    caches_k = [c[0].reshape(dec_batch, c.shape[2], GROUP_WIDTH) for c in (cache_win0_k, cache_win1_k, cache_win2_k)]
    caches_v = [c[0].reshape(dec_batch, c.shape[2], GROUP_WIDTH) for c in (cache_win0_v, cache_win1_v, cache_win2_v)]
    new_k, new_v, o, lse = _cache_attn(q, k, v, caches_k, caches_v, tap_biases, batch=dec_batch, n_new=dec_seq)
    yconv, s_conv = _conv_sample(u, state_conv[0], w['conv_dw_w'], batch=dec_batch, n_new=dec_seq)
    mk_s = cache_mem_k[0].reshape(dec_batch, N_MEM, D_MODEL)
    mv_s = cache_mem_v[0].reshape(dec_batch, N_MEM, D_MODEL)
    y_sample = _layer_tail(x1, o, lse, yconv, ga, gb, mk_s, mv_s, w, final_g, batch=dec_batch, tm=rows_s, tq=dec_seq)

    s_win = []
    for g in range(len(GROUPS)):
        for a in (new_k[g], new_v[g]):
            s_win.append(heads5(a, dec_batch, a.shape[1]))

    return (y_prompt.reshape(batch, seq_len, D_MODEL), y_sample.reshape(dec_batch, dec_seq, D_MODEL),
            *p_win, p_conv, p_mem_k, p_mem_v, *s_win, s_conv[None])
```

```python
import functools
import math

import jax
import jax.numpy as jnp
import numpy as np
from jax import lax
from jax.experimental import pallas as pl
from jax.experimental.pallas import tpu as pltpu

D_MODEL = 1024
HEAD_DIM = 64
GROUPS = ((128, 1), (512, 4), (2048, 16))
HEADS_PER_GROUP = 4
GROUP_WIDTH = HEADS_PER_GROUP * HEAD_DIM
ATTN_WIDTH = len(GROUPS) * GROUP_WIDTH
CONV_DIM = D_MODEL // 2
CONV_WIDTH = 31
CONV_HALO = 32
N_BUCKETS = 32
MAX_DISTANCE = 2048
D_FF = ((8 * D_MODEL // 3 + 127) // 128) * 128
N_MEM = 256
X_HEADS = 4
X_HEAD_DIM = D_MODEL // X_HEADS
EPS = 1e-6
NEG_INF = -1e30
SPAN = 128
LANES = 128

V7X_VMEM_LIMIT_BYTES = 56 * 1024 * 1024
BF16 = jnp.bfloat16
F32 = jnp.float32


def _params(n_axes):
    return pltpu.CompilerParams(dimension_semantics=("parallel",) * n_axes,
                                vmem_limit_bytes=V7X_VMEM_LIMIT_BYTES)


def _const_spec(shape):
    return pl.BlockSpec(shape, lambda *_: (0,) * len(shape), pipeline_mode=pl.Buffered(1))


def _rows_spec(tm, width):
    return pl.BlockSpec((tm, width), lambda i: (i, 0))


def _rms(x, g):
    return x * lax.rsqrt(jnp.mean(x * x, axis=-1, keepdims=True) + EPS) * g


def _dot(a, b):
    return jnp.dot(a, b, preferred_element_type=F32)


def _dot_nt(a, b):
    return lax.dot_general(a, b, (((1,), (1,)), ((), ())), preferred_element_type=F32)


def _head_masks(width=GROUP_WIDTH):
    lane = lax.broadcasted_iota(jnp.int32, (1, width), 1)
    return [(lane >= h * HEAD_DIM) & (lane < (h + 1) * HEAD_DIM) for h in range(HEADS_PER_GROUP)]


def _ffn_kernel(*refs, pre_proj, final_norm):
    refs = list(refs)
    x_ref = refs.pop(0)
    if pre_proj:
        a_ref, wp_ref = refs.pop(0), refs.pop(0)
    g_ref, wg_ref, wu_ref, wd_ref = refs.pop(0), refs.pop(0), refs.pop(0), refs.pop(0)
    if final_norm:
        fg_ref = refs.pop(0)
    o_ref = refs.pop(0)

    x = x_ref[...]
    if pre_proj:
        x = x + _dot(a_ref[...], wp_ref[...])
    h = _rms(x, g_ref[...]).astype(BF16)
    gate = _dot(h, wg_ref[...])
    up = _dot(h, wu_ref[...])
    act = (gate * jax.nn.sigmoid(gate) * up).astype(BF16)
    x = x + 0.5 * _dot(act, wd_ref[...])
    if final_norm:
        x = _rms(x, fg_ref[...])
    o_ref[...] = x


def _ffn(x, norm_g, wg, wu, wd, *, tm, pre=None, final_g=None):
    rows = x.shape[0]
    args = [x]
    specs = [_rows_spec(tm, D_MODEL)]
    if pre is not None:
        a, wp = pre
        args += [a, wp]
        specs += [_rows_spec(tm, D_MODEL), _const_spec((D_MODEL, D_MODEL))]
    args += [norm_g, wg, wu, wd]
    specs += [_const_spec((1, D_MODEL)), _const_spec((D_MODEL, D_FF)), _const_spec((D_MODEL, D_FF)),
              _const_spec((D_FF, D_MODEL))]
    if final_g is not None:
        args.append(final_g)
        specs.append(_const_spec((1, D_MODEL)))
    return pl.pallas_call(
        functools.partial(_ffn_kernel, pre_proj=pre is not None, final_norm=final_g is not None),
        grid=(rows // tm,),
        in_specs=specs,
        out_specs=_rows_spec(tm, D_MODEL),
        out_shape=jax.ShapeDtypeStruct((rows, D_MODEL), F32),
        compiler_params=_params(1),
        name="ffn",
    )(*args)


_Q_END = ATTN_WIDTH
_K_END = 2 * ATTN_WIDTH
_V_END = 3 * ATTN_WIDTH
_UA_END = _V_END + CONV_DIM
_UB_END = _UA_END + CONV_DIM
_GA_END = _UB_END + D_MODEL
IN_WIDTH = _GA_END + D_MODEL


def _win_kernel(x_ref, g_ref, w_ref, wt_ref, *out_refs, n_t):
    q0, q1, q2, k0, k1, k2, v0, v1, v2, u_ref, ga_ref, gb_ref = out_refs[:12]
    t_refs = out_refs[12:]
    h = _rms(x_ref[...], g_ref[...]).astype(BF16)

    def seg(lo, hi):
        return _dot(h, w_ref[:, lo:hi])

    for g, (q_ref, k_ref, v_ref) in enumerate(((q0, k0, v0), (q1, k1, v1), (q2, k2, v2))):
        lo = g * GROUP_WIDTH
        q_ref[...] = seg(lo, lo + GROUP_WIDTH) * (HEAD_DIM ** -0.5)
        k_ref[...] = seg(_Q_END + lo, _Q_END + lo + GROUP_WIDTH)
        v_ref[...] = seg(_K_END + lo, _K_END + lo + GROUP_WIDTH)
    u_ref[...] = seg(_V_END, _UA_END) * jax.nn.sigmoid(seg(_UA_END, _UB_END))
    ga_ref[...] = jax.nn.sigmoid(seg(_UB_END, _GA_END))
    gb_ref[...] = jax.nn.sigmoid(seg(_GA_END, IN_WIDTH))
    for j in range(n_t):
        t_refs[j][0] = _dot_nt(wt_ref[j * GROUP_WIDTH:(j + 1) * GROUP_WIDTH, :], h)


def _win(x, norm_g, w_in, w_t, *, tm, rows_per_seq):
    rows = x.shape[0]
    n_t = w_t.shape[0] // GROUP_WIDTH
    tiles_per_seq = rows_per_seq // tm
    grp = jax.ShapeDtypeStruct((rows, GROUP_WIDTH), F32)
    out_shape = ([grp] * 9
                 + [jax.ShapeDtypeStruct((rows, CONV_DIM), F32)]
                 + [jax.ShapeDtypeStruct((rows, D_MODEL), F32)] * 2
                 + [jax.ShapeDtypeStruct((rows // rows_per_seq, GROUP_WIDTH, rows_per_seq), F32)] * n_t)
    t_spec = pl.BlockSpec((1, GROUP_WIDTH, tm), lambda i: (i // tiles_per_seq, 0, i % tiles_per_seq))
    out_specs = ([_rows_spec(tm, GROUP_WIDTH)] * 9 + [_rows_spec(tm, CONV_DIM)]
                 + [_rows_spec(tm, D_MODEL)] * 2 + [t_spec] * n_t)
    return pl.pallas_call(
        functools.partial(_win_kernel, n_t=n_t),
        grid=(rows // tm,),
        in_specs=[_rows_spec(tm, D_MODEL), _const_spec((1, D_MODEL)), _const_spec((D_MODEL, IN_WIDTH)),
                  _const_spec(w_t.shape)],
        out_specs=out_specs,
        out_shape=out_shape,
        compiler_params=_params(1),
        name="w_in",
    )(x, norm_g, w_in, w_t)


def _rel_bucket(dist):
    n = jnp.maximum(dist, 0)
    max_exact = N_BUCKETS // 2
    nf = jnp.maximum(n, 1).astype(F32)
    large = max_exact + (jnp.log(nf / max_exact) / math.log(MAX_DISTANCE / max_exact)
                         * (N_BUCKETS - max_exact)).astype(jnp.int32)
    return jnp.where(n < max_exact, n, jnp.minimum(large, N_BUCKETS - 1))


def _tap_bias(rel_bias, g, dil):
    bias_g = rel_bias[:, g * HEADS_PER_GROUP:(g + 1) * HEADS_PER_GROUP]
    return bias_g[_rel_bucket(jnp.arange(SPAN + 1) * dil)].astype(F32).T


def _band_bias(tap_bias):
    period = 3 * SPAN
    vec = jnp.concatenate([tap_bias[:, ::-1],
                           jnp.full((HEADS_PER_GROUP, period - (SPAN + 1)), NEG_INF, F32)], axis=1)
    flat = jnp.tile(vec, (1, SPAN))[:, :SPAN * (period - 1)]
    return flat.reshape(HEADS_PER_GROUP, SPAN, period - 1)[:, :, :2 * SPAN]


def _cache_bias(tap_bias, dil, cache_len, n_new):
    by_dist = lax.pad(tap_bias, jnp.float32(NEG_INF), [(0, 0, 0), (0, cache_len + n_new - 1 - SPAN * dil, dil - 1)])
    rev = by_dist[:, ::-1]
    bias_c = jnp.stack([rev[:, n_new - 1 - t:n_new - 1 - t + cache_len] for t in range(n_new)], axis=1)
    near = jnp.concatenate([rev[:, cache_len:], jnp.full((HEADS_PER_GROUP, LANES), NEG_INF, F32)], axis=1)
    bias_n = jnp.stack([near[:, n_new - 1 - t:n_new - 1 - t + LANES] for t in range(n_new)], axis=1)
    return (bias_c.reshape(HEADS_PER_GROUP * n_new, cache_len), bias_n.reshape(HEADS_PER_GROUP * n_new, LANES))


def _band_attn_kernel(q_lo, q_hi, k_lo, k_hi, v_lo, v_hi, b_ref, o_lo, o_hi, l_lo, l_hi,
                      qb_ref, kb_ref, vb_ref, ob_ref, lb_ref, *, dil, seq):
    n_blk = seq // SPAN
    halves = (slice(0, LANES), slice(LANES, 2 * LANES))
    for r in range(dil):
        rows = pl.ds(r, seq, stride=dil) if dil > 1 else slice(None)
        for dst, srcs in ((qb_ref, (q_lo, q_hi)), (kb_ref, (k_lo, k_hi)), (vb_ref, (v_lo, v_hi))):
            for lanes, src in zip(halves, srcs):
                dst[r, :, lanes] = src[rows, :].astype(BF16)
    masks = _head_masks()

    def block(q, kb, vb, key_lo):
        o = jnp.zeros((SPAN, GROUP_WIDTH), F32)
        lse = jnp.zeros((SPAN, GROUP_WIDTH), F32)
        for h in range(HEADS_PER_GROUP):
            s = _dot_nt(jnp.where(masks[h], q, jnp.zeros_like(q)), kb) + b_ref[h, :, key_lo:]
            m = jnp.max(s, axis=-1, keepdims=True)
            p = jnp.exp(s - m)
            den = jnp.sum(p, axis=-1, keepdims=True)
            acc = _dot(p.astype(BF16), vb)
            o = jnp.where(masks[h], acc * (1.0 / den), o)
            lse = jnp.where(masks[h], m + jnp.log(den), lse)
        return o, lse

    for r in range(dil):
        o, lse = block(qb_ref[r, 0:SPAN, :], kb_ref[r, 0:SPAN, :], vb_ref[r, 0:SPAN, :], SPAN)
        ob_ref[r, 0:SPAN, :] = o
        lb_ref[r, 0:SPAN, :] = lse

        if n_blk > 1:
            def body(j, carry, r=r):
                q_start = pl.multiple_of(j * SPAN, SPAN)
                k_start = pl.multiple_of((j - 1) * SPAN, SPAN)
                o, lse = block(qb_ref[r, pl.ds(q_start, SPAN), :], kb_ref[r, pl.ds(k_start, 2 * SPAN), :],
                               vb_ref[r, pl.ds(k_start, 2 * SPAN), :], 0)
                ob_ref[r, pl.ds(q_start, SPAN), :] = o
                lb_ref[r, pl.ds(q_start, SPAN), :] = lse
                return carry
            lax.fori_loop(1, n_blk, body, 0)

    for r in range(dil):
        rows = pl.ds(r, seq, stride=dil) if dil > 1 else slice(None)
        for src, dsts in ((ob_ref, (o_lo, o_hi)), (lb_ref, (l_lo, l_hi))):
            for lanes, dst in zip(halves, dsts):
                dst[rows, :] = src[r, :, lanes]


def _band_attn(q, k, v, bias, *, batch, seq_len, dil):
    seq = seq_len // dil
    view = lambda a: a.reshape(batch, seq_len, GROUP_WIDTH)
    in_halves = [pl.BlockSpec((None, seq_len, LANES), lambda b, hf=hf: (b, 0, hf)) for hf in (0, 1)]
    out_half = pl.BlockSpec((None, seq_len, LANES), lambda b: (b, 0, 0))
    o_lo, o_hi, l_lo, l_hi = pl.pallas_call(
        functools.partial(_band_attn_kernel, dil=dil, seq=seq),
        grid=(batch,),
        in_specs=in_halves * 3 + [_const_spec((HEADS_PER_GROUP, SPAN, 2 * SPAN))],
        out_specs=[out_half] * 4,
        out_shape=[jax.ShapeDtypeStruct((batch, seq_len, LANES), F32)] * 4,
        scratch_shapes=[pltpu.VMEM((dil, seq, GROUP_WIDTH), BF16)] * 3 + [pltpu.VMEM((dil, seq, GROUP_WIDTH), F32)] * 2,
        compiler_params=_params(1),
        name="band_attn_d%d" % dil,
    )(view(q), view(q), view(k), view(k), view(v), view(v), bias)
    flat = lambda a: a.reshape(batch * seq_len, LANES)
    return (flat(o_lo), flat(o_hi)), (flat(l_lo), flat(l_hi))


def _cache_attn_kernel(*refs, n_new):
    n_g = len(GROUPS)
    q_refs, kn_refs, vn_refs = refs[0:n_g], refs[n_g:2 * n_g], refs[2 * n_g:3 * n_g]
    knt_refs, vnt_refs = refs[3 * n_g:4 * n_g], refs[4 * n_g:5 * n_g]
    kc_refs, vc_refs = refs[5 * n_g:6 * n_g], refs[6 * n_g:7 * n_g]
    bc_refs, bn_refs = refs[7 * n_g:8 * n_g], refs[8 * n_g:9 * n_g]
    outs = refs[9 * n_g:]
    ko_refs, vo_refs = outs[0:n_g], outs[n_g:2 * n_g]
    o_refs, l_refs = outs[2 * n_g:4 * n_g], outs[4 * n_g:6 * n_g]
    masks = _head_masks()
    pad = jnp.zeros((SPAN - n_new, GROUP_WIDTH), F32)
    lane = lax.broadcasted_iota(jnp.int32, (1, LANES), 1)
    j = pl.program_id(0) % (LANES // n_new)
    to_tail = LANES - n_new - n_new * j

    for g in range(n_g):
        kc, vc = kc_refs[g][0], vc_refs[g][0]
        kn, vn = kn_refs[g][0], vn_refs[g][0]
        cache_len = kc.shape[1]
        for src, new_t, dst in ((kc, knt_refs[g][0], ko_refs[g]), (vc, vnt_refs[g][0], vo_refs[g])):
            shifted = pltpu.roll(src, cache_len - n_new, axis=1)
            new_tail = pltpu.roll(new_t, to_tail, axis=1)
            if cache_len > LANES:
                dst[0, :, 0:cache_len - LANES] = shifted[:, 0:cache_len - LANES]
            dst[0, :, cache_len - LANES:cache_len] = jnp.where(lane >= LANES - n_new, new_tail,
                                                              shifted[:, cache_len - LANES:cache_len])
        q = q_refs[g][0]
        qm = jnp.concatenate([jnp.where(masks[h], q, 0.0) for h in range(HEADS_PER_GROUP)], axis=0).astype(BF16)
        kn_p = jnp.concatenate([kn, pad], axis=0).astype(BF16)
        vn_p = jnp.concatenate([vn, pad], axis=0).astype(BF16)
        s_c = _dot(qm, kc.astype(BF16)) + bc_refs[g][...]
        s_n = _dot_nt(qm, kn_p) + bn_refs[g][...]
        m = jnp.maximum(jnp.max(s_c, axis=-1, keepdims=True), jnp.max(s_n, axis=-1, keepdims=True))
        p_c = jnp.exp(s_c - m)
        p_n = jnp.exp(s_n - m)
        den = jnp.sum(p_c, axis=-1, keepdims=True) + jnp.sum(p_n, axis=-1, keepdims=True)
        acc = _dot_nt(p_c.astype(BF16), vc.astype(BF16)) + _dot(p_n.astype(BF16), vn_p)
        acc = acc * (1.0 / den)
        lse_rows = m + jnp.log(den)
        o = jnp.zeros((n_new, GROUP_WIDTH), F32)
        lse = jnp.zeros((n_new, GROUP_WIDTH), F32)
        for h in range(HEADS_PER_GROUP):
            rows = slice(h * n_new, (h + 1) * n_new)
            o = jnp.where(masks[h], acc[rows], o)
            lse = jnp.where(masks[h], lse_rows[rows], lse)
        for hf in (0, 1):
            o_refs[2 * g + hf][0] = o[:, hf * LANES:(hf + 1) * LANES]
            l_refs[2 * g + hf][0] = lse[:, hf * LANES:(hf + 1) * LANES]


def _cache_attn(q, k_new, v_new, k_new_t, v_new_t, k_cache, v_cache, tap_biases, *, batch, n_new):
    n_g = len(GROUPS)
    per_tile = LANES // n_new
    new3 = lambda a: a.reshape(batch, n_new, GROUP_WIDTH)
    new_spec = pl.BlockSpec((1, n_new, GROUP_WIDTH), lambda b: (b, 0, 0))
    new_t_spec = pl.BlockSpec((1, GROUP_WIDTH, LANES), lambda b: (0, 0, b // per_tile))
    cache_specs = [pl.BlockSpec((1, GROUP_WIDTH, c.shape[2]), lambda b: (b, 0, 0)) for c in k_cache]
    biases_c, biases_n = [], []
    for g, (_, dil) in enumerate(GROUPS):
        bc, bn = _cache_bias(tap_biases[g], dil, k_cache[g].shape[2], n_new)
        biases_c.append(bc)
        biases_n.append(bn)
    args = ([new3(a) for a in q] + [new3(a) for a in k_new] + [new3(a) for a in v_new]
            + list(k_new_t) + list(v_new_t) + list(k_cache) + list(v_cache) + biases_c + biases_n)
    in_specs = ([new_spec] * (3 * n_g) + [new_t_spec] * (2 * n_g) + cache_specs * 2
                + [_const_spec(b.shape) for b in biases_c] + [_const_spec(b.shape) for b in biases_n])
    half_spec = pl.BlockSpec((1, n_new, LANES), lambda b: (b, 0, 0))
    out_shape = ([jax.ShapeDtypeStruct(c.shape, F32) for c in k_cache] * 2
                 + [jax.ShapeDtypeStruct((batch, n_new, LANES), F32)] * (4 * n_g))
    out_specs = cache_specs * 2 + [half_spec] * (4 * n_g)
    outs = pl.pallas_call(
        functools.partial(_cache_attn_kernel, n_new=n_new),
        grid=(batch,),
        in_specs=in_specs,
        out_specs=out_specs,
        out_shape=out_shape,
        compiler_params=_params(1),
        name="cache_attn",
    )(*args)
    flat = lambda a: a.reshape(batch * n_new, LANES)
    halves = lambda lst: [(flat(lst[2 * g]), flat(lst[2 * g + 1])) for g in range(n_g)]
    return outs[0:n_g], outs[n_g:2 * n_g], halves(outs[2 * n_g:4 * n_g]), halves(outs[4 * n_g:6 * n_g])


def _conv_taps(ext_ref, w_ref, first_row, n_rows):
    acc = jnp.zeros((n_rows, CONV_DIM), F32)
    for j in range(CONV_WIDTH):
        acc = acc + ext_ref[pl.ds(first_row + j, n_rows), :] * w_ref[j:j + 1, :]
    return acc


def _conv_prompt_kernel(u_ref, halo_ref, w_ref, y_ref, ext_ref, *, tc):
    i = pl.program_id(1)
    ext_ref[0:CONV_HALO, :] = jnp.where(i > 0, halo_ref[0], 0.0)
    ext_ref[CONV_HALO:CONV_HALO + tc, :] = u_ref[0]
    y_ref[0] = _conv_taps(ext_ref, w_ref, CONV_HALO - (CONV_WIDTH - 1), tc)


def _conv_prompt(u, w, *, batch, seq_len, tc):
    u3 = u.reshape(batch, seq_len, CONV_DIM)
    halo_per_tile = tc // CONV_HALO
    y = pl.pallas_call(
        functools.partial(_conv_prompt_kernel, tc=tc),
        grid=(batch, seq_len // tc),
        in_specs=[pl.BlockSpec((1, tc, CONV_DIM), lambda b, i: (b, i, 0)),
                  pl.BlockSpec((1, CONV_HALO, CONV_DIM),
                               lambda b, i: (b, jnp.maximum(i * halo_per_tile - 1, 0), 0)),
                  pl.BlockSpec((CONV_WIDTH, CONV_DIM), lambda b, i: (0, 0))],
        out_specs=pl.BlockSpec((1, tc, CONV_DIM), lambda b, i: (b, i, 0)),
        out_shape=jax.ShapeDtypeStruct((batch, seq_len, CONV_DIM), F32),
        scratch_shapes=[pltpu.VMEM((CONV_HALO + tc, CONV_DIM), F32)],
        compiler_params=_params(2),
        name="conv_prompt",
    )(u3, u3, w)
    return y.reshape(batch * seq_len, CONV_DIM)


def _conv_sample_kernel(u_ref, st_ref, w_ref, y_ref, so_ref, ext_ref, *, n_new):
    hist = CONV_WIDTH - 1
    ext_ref[0:hist, :] = st_ref[0]
    ext_ref[hist:hist + n_new, :] = u_ref[0]
    y_ref[0] = _conv_taps(ext_ref, w_ref, 0, n_new)
    so_ref[0] = ext_ref[n_new:n_new + hist, :]


def _conv_sample(u, state, w, *, batch, n_new):
    hist = CONV_WIDTH - 1
    u3 = u.reshape(batch, n_new, CONV_DIM)
    y, st = pl.pallas_call(
        functools.partial(_conv_sample_kernel, n_new=n_new),
        grid=(batch,),
        in_specs=[pl.BlockSpec((1, n_new, CONV_DIM), lambda b: (b, 0, 0)),
                  pl.BlockSpec((1, hist, CONV_DIM), lambda b: (b, 0, 0)),
                  pl.BlockSpec((CONV_WIDTH, CONV_DIM), lambda b: (0, 0))],
        out_specs=[pl.BlockSpec((1, n_new, CONV_DIM), lambda b: (b, 0, 0)),
                   pl.BlockSpec((1, hist, CONV_DIM), lambda b: (b, 0, 0))],
        out_shape=[jax.ShapeDtypeStruct((batch, n_new, CONV_DIM), F32),
                   jax.ShapeDtypeStruct((batch, hist, CONV_DIM), F32)],
        scratch_shapes=[pltpu.VMEM((hist + n_new + 2, CONV_DIM), F32)],
        compiler_params=_params(1),
        name="conv_sample",
    )(u3, state, w)
    return y.reshape(batch * n_new, CONV_DIM), st


def _mix_kernel(*refs):
    n_half = 2 * len(GROUPS)
    o_refs, l_refs = refs[0:n_half], refs[n_half:2 * n_half]
    (yc_ref, ga_ref, gb_ref, x_ref, wap_ref, cb_ref, lg_ref, lb_ref, wcp_ref, wo_ref, xg_ref, wxq_ref,
     x_out, q_out) = refs[2 * n_half:]
    merged_halves = []
    for hf in (0, 1):
        lses = [l_refs[2 * g + hf][...] for g in range(len(GROUPS))]
        m = functools.reduce(jnp.maximum, lses)
        es = [jnp.exp(l - m) for l in lses]
        num = sum(e * o_refs[2 * g + hf][...] for g, e in enumerate(es))
        merged_halves.append(num * (1.0 / sum(es)))
    attn = jnp.concatenate(merged_halves, axis=-1)
    a_branch = _dot(attn.astype(BF16), wap_ref[...])
    y = yc_ref[...] + cb_ref[...]
    mu = jnp.mean(y, axis=-1, keepdims=True)
    yc = y - mu
    var = jnp.mean(yc * yc, axis=-1, keepdims=True)
    y = yc * lax.rsqrt(var + EPS) * lg_ref[...] + lb_ref[...]
    y = y * jax.nn.sigmoid(y)
    c_branch = _dot(y.astype(BF16), wcp_ref[...])
    merged = ga_ref[...] * a_branch + gb_ref[...] * c_branch
    x = x_ref[...] + _dot(merged.astype(BF16), wo_ref[...])
    x_out[...] = x
    hq = _rms(x, xg_ref[...]).astype(BF16)
    q_out[...] = (_dot(hq, wxq_ref[...]) * (X_HEAD_DIM ** -0.5)).astype(BF16)


def _mix(o, lse, yconv, ga, gb, x, w, *, tm):
    rows = x.shape[0]
    half = _rows_spec(tm, LANES)
    wide = _rows_spec(tm, D_MODEL)
    o = [a for pair in o for a in pair]
    lse = [a for pair in lse for a in pair]
    return pl.pallas_call(
        _mix_kernel,
        grid=(rows // tm,),
        in_specs=[half] * 12 + [_rows_spec(tm, CONV_DIM), wide, wide, wide,
                              _const_spec((GROUP_WIDTH, D_MODEL)), _const_spec((1, CONV_DIM)),
                              _const_spec((1, CONV_DIM)), _const_spec((1, CONV_DIM)),
                              _const_spec((CONV_DIM, D_MODEL)), _const_spec((D_MODEL, D_MODEL)),
                              _const_spec((1, D_MODEL)), _const_spec((D_MODEL, D_MODEL))],
        out_specs=[wide, wide],
        out_shape=[jax.ShapeDtypeStruct((rows, D_MODEL), F32), jax.ShapeDtypeStruct((rows, D_MODEL), BF16)],
        compiler_params=_params(1),
        name="mix",
    )(*o, *lse, yconv, ga, gb, x, w['w_attn_proj'], w['conv_dw_b'], w['conv_ln_g'], w['conv_ln_b'],
      w['w_conv_proj'], w['w_o'], w['xattn_norm'], w['w_xq'])


def _memkv_kernel(m_ref, g_ref, w_ref, k_ref, v_ref):
    h = _rms(m_ref[...], g_ref[...]).astype(BF16)
    k_ref[...] = _dot(h, w_ref[:, 0:D_MODEL])
    v_ref[...] = _dot(h, w_ref[:, D_MODEL:2 * D_MODEL])


def _memkv(mem, norm_g, w_xkv, *, tm):
    rows = mem.shape[0]
    wide = _rows_spec(tm, D_MODEL)
    return pl.pallas_call(
        _memkv_kernel,
        grid=(rows // tm,),
        in_specs=[wide, _const_spec((1, D_MODEL)), _const_spec((D_MODEL, 2 * D_MODEL))],
        out_specs=[wide, wide],
        out_shape=[jax.ShapeDtypeStruct((rows, D_MODEL), F32)] * 2,
        compiler_params=_params(1),
        name="memory_kv",
    )(mem, norm_g, w_xkv)


def _xattn_kernel(q_ref, k_ref, v_ref, o_ref):
    for h in range(X_HEADS):
        lanes = slice(h * X_HEAD_DIM, (h + 1) * X_HEAD_DIM)
        s = _dot_nt(q_ref[0, :, lanes], k_ref[0, :, lanes].astype(BF16))
        m = jnp.max(s, axis=-1, keepdims=True)
        p = jnp.exp(s - m)
        den = jnp.sum(p, axis=-1, keepdims=True)
        acc = _dot(p.astype(BF16), v_ref[0, :, lanes].astype(BF16))
        o_ref[0, :, lanes] = (acc * (1.0 / den)).astype(BF16)


def _xattn(q, mk, mv, *, batch, tq):
    rpb = q.shape[1]
    q_spec = pl.BlockSpec((1, tq, D_MODEL), lambda b, i: (b, i, 0))
    m_spec = pl.BlockSpec((1, N_MEM, D_MODEL), lambda b, i: (b, 0, 0))
    return pl.pallas_call(
        _xattn_kernel,
        grid=(batch, rpb // tq),
        in_specs=[q_spec, m_spec, m_spec],
        out_specs=q_spec,
        out_shape=jax.ShapeDtypeStruct((batch, rpb, D_MODEL), BF16),
        compiler_params=_params(2),
        name="cross_attn",
    )(q, mk, mv)


def _layer_tail(x1, o, lse, yconv, ga, gb, mk, mv, w, final_g, *, batch, tm, tq):
    rows = x1.shape[0]
    x2, xq = _mix(o, lse, yconv, ga, gb, x1, w, tm=tm)
    xo = _xattn(xq.reshape(batch, rows // batch, D_MODEL), mk, mv, batch=batch, tq=tq)
    return _ffn(x2, w['ffn2_norm'], w['ffn2_w_gate'], w['ffn2_w_up'], w['ffn2_w_down'], tm=tm,
                pre=(xo.reshape(rows, D_MODEL), w['w_xo']), final_g=final_g)


def _to_positions_heads(a_t, batch, n_pos):
    return jnp.transpose(a_t.reshape(batch, HEADS_PER_GROUP, HEAD_DIM, n_pos), (0, 3, 1, 2))[None]


def _to_feature_major(a, batch, n_pos):
    return jnp.transpose(a[0], (0, 2, 3, 1)).reshape(batch, GROUP_WIDTH, n_pos)


def kernel(x_prompt, x_sample, mem_prompt, cache_win0_k, cache_win0_v, cache_win1_k, cache_win1_v, cache_win2_k, cache_win2_v, state_conv, cache_mem_k, cache_mem_v, rel_bias, ffn1_norm, ffn1_w_gate, ffn1_w_up, ffn1_w_down, mix_norm, w_in, w_attn_proj, conv_dw_w, conv_dw_b, conv_ln_g, conv_ln_b, w_conv_proj, w_o, xattn_norm, mem_norm, w_xq, w_xkv, w_xo, ffn2_norm, ffn2_w_gate, ffn2_w_up, ffn2_w_down, final_norm):
    batch, seq_len, _ = x_prompt.shape
    dec_batch, dec_seq, _ = x_sample.shape
    n_g = len(GROUPS)
    assert ffn1_norm.shape[0] == 1, "single layer"
    mat = lambda a: a[0].astype(BF16)
    vec = lambda a: a[0].reshape(1, -1)
    w = dict(ffn1_norm=vec(ffn1_norm), ffn1_w_gate=mat(ffn1_w_gate), ffn1_w_up=mat(ffn1_w_up),
             ffn1_w_down=mat(ffn1_w_down), mix_norm=vec(mix_norm), w_in=mat(w_in),
             w_attn_proj=mat(w_attn_proj), conv_dw_w=conv_dw_w[0], conv_dw_b=vec(conv_dw_b),
             conv_ln_g=vec(conv_ln_g), conv_ln_b=vec(conv_ln_b), w_conv_proj=mat(w_conv_proj),
             w_o=mat(w_o), xattn_norm=vec(xattn_norm), mem_norm=vec(mem_norm), w_xq=mat(w_xq),
             w_xkv=mat(w_xkv), w_xo=mat(w_xo), ffn2_norm=vec(ffn2_norm), ffn2_w_gate=mat(ffn2_w_gate),
             ffn2_w_up=mat(ffn2_w_up), ffn2_w_down=mat(ffn2_w_down))
    final_g = final_norm.reshape(1, -1)
    tap_biases = [_tap_bias(rel_bias, g, dil) for g, (_, dil) in enumerate(GROUPS)]
    w_kv_t = w_in[0, :, _Q_END:_V_END].T.astype(BF16)

    def head(x, tm, w_t, rows_per_seq):
        x1 = _ffn(x, w['ffn1_norm'], w['ffn1_w_gate'], w['ffn1_w_up'], w['ffn1_w_down'], tm=tm)
        outs = _win(x1, w['mix_norm'], w['w_in'], w_t, tm=tm, rows_per_seq=rows_per_seq)
        return x1, outs[0:3], outs[3:6], outs[6:9], outs[9], outs[10], outs[11], outs[12:]

    rows_p = batch * seq_len
    tm_p = 512
    full_groups = [g for g, (window, _) in enumerate(GROUPS) if window >= seq_len]
    w_t_p = jnp.concatenate([w_kv_t[(kv * n_g + g) * GROUP_WIDTH:(kv * n_g + g + 1) * GROUP_WIDTH]
                             for g in full_groups for kv in (0, 1)], axis=0)
    x1, q, k, v, u, ga, gb, kv_t = head(x_prompt.reshape(rows_p, D_MODEL), tm_p, w_t_p, seq_len)
    o, lse = [], []
    for g, (_, dil) in enumerate(GROUPS):
        o_g, l_g = _band_attn(q[g], k[g], v[g], _band_bias(tap_biases[g]), batch=batch, seq_len=seq_len, dil=dil)
        o.append(o_g)
        lse.append(l_g)
    yconv = _conv_prompt(u, w['conv_dw_w'], batch=batch, seq_len=seq_len, tc=256)
    mk_p, mv_p = _memkv(mem_prompt.reshape(batch * N_MEM, D_MODEL), w['mem_norm'], w['w_xkv'], tm=512)
    mk_p = mk_p.reshape(batch, N_MEM, D_MODEL)
    mv_p = mv_p.reshape(batch, N_MEM, D_MODEL)
    y_prompt = _layer_tail(x1, o, lse, yconv, ga, gb, mk_p, mv_p, w, final_g, batch=batch, tm=tm_p, tq=512)

    p_win = []
    for g, (window, _) in enumerate(GROUPS):
        keep = min(window, seq_len)
        if g in full_groups:
            j = full_groups.index(g)
            p_win += [_to_positions_heads(kv_t[2 * j], batch, seq_len), _to_positions_heads(kv_t[2 * j + 1], batch, seq_len)]
        else:
            for a in (k[g], v[g]):
                p_win.append(a.reshape(batch, seq_len, GROUP_WIDTH)[:, seq_len - keep:]
                             .reshape(1, batch, keep, HEADS_PER_GROUP, HEAD_DIM))
    p_conv = u.reshape(batch, seq_len, CONV_DIM)[:, seq_len - (CONV_WIDTH - 1):][None]
    p_mem_k = mk_p.reshape(1, batch, N_MEM, X_HEADS, X_HEAD_DIM)
    p_mem_v = mv_p.reshape(1, batch, N_MEM, X_HEADS, X_HEAD_DIM)

    rows_s = dec_batch * dec_seq
    x1, q, k, v, u, ga, gb, kv_t = head(x_sample.reshape(rows_s, D_MODEL), rows_s, w_kv_t, rows_s)
    caches_k = [_to_feature_major(c, dec_batch, c.shape[2]) for c in (cache_win0_k, cache_win1_k, cache_win2_k)]
    caches_v = [_to_feature_major(c, dec_batch, c.shape[2]) for c in (cache_win0_v, cache_win1_v, cache_win2_v)]
    new_k, new_v, o, lse = _cache_attn(q, k, v, kv_t[0:n_g], kv_t[n_g:2 * n_g], caches_k, caches_v, tap_biases,
                                       batch=dec_batch, n_new=dec_seq)
    yconv, s_conv = _conv_sample(u, state_conv[0], w['conv_dw_w'], batch=dec_batch, n_new=dec_seq)
    mk_s = cache_mem_k[0].reshape(dec_batch, N_MEM, D_MODEL)
    mv_s = cache_mem_v[0].reshape(dec_batch, N_MEM, D_MODEL)
    y_sample = _layer_tail(x1, o, lse, yconv, ga, gb, mk_s, mv_s, w, final_g, batch=dec_batch, tm=rows_s, tq=dec_seq)

    s_win = []
    for g in range(n_g):
        for a in (new_k[g], new_v[g]):
            s_win.append(_to_positions_heads(a, dec_batch, a.shape[2]))

    return (y_prompt.reshape(batch, seq_len, D_MODEL), y_sample.reshape(dec_batch, dec_seq, D_MODEL),
            *p_win, p_conv, p_mem_k, p_mem_v, *s_win, s_conv[None])
```

```python
import functools
import math

import jax
import jax.numpy as jnp
import numpy as np
from jax import lax
from jax.experimental import pallas as pl
from jax.experimental.pallas import tpu as pltpu

D_MODEL = 1024
HEAD_DIM = 64
GROUPS = ((128, 1), (512, 4), (2048, 16))
HEADS_PER_GROUP = 4
GROUP_WIDTH = HEADS_PER_GROUP * HEAD_DIM
ATTN_WIDTH = len(GROUPS) * GROUP_WIDTH
CONV_DIM = D_MODEL // 2
CONV_WIDTH = 31
CONV_HALO = 32
N_BUCKETS = 32
MAX_DISTANCE = 2048
D_FF = ((8 * D_MODEL // 3 + 127) // 128) * 128
N_MEM = 256
X_HEADS = 4
X_HEAD_DIM = D_MODEL // X_HEADS
EPS = 1e-6
NEG_INF = -1e30
SPAN = 128
LANES = 128
SUBLANES = 8

V7X_VMEM_LIMIT_BYTES = 56 * 1024 * 1024
BF16 = jnp.bfloat16
F32 = jnp.float32


def _params(n_axes):
    return pltpu.CompilerParams(dimension_semantics=("parallel",) * n_axes,
                                vmem_limit_bytes=V7X_VMEM_LIMIT_BYTES)


def _const_spec(shape):
    return pl.BlockSpec(shape, lambda *_: (0,) * len(shape), pipeline_mode=pl.Buffered(1))


def _rows_spec(tm, width):
    return pl.BlockSpec((tm, width), lambda i: (i, 0))


def _rms(x, g):
    return x * lax.rsqrt(jnp.mean(x * x, axis=-1, keepdims=True) + EPS) * g


def _dot(a, b):
    return jnp.dot(a, b, preferred_element_type=F32)


def _dot_nt(a, b):
    return lax.dot_general(a, b, (((1,), (1,)), ((), ())), preferred_element_type=F32)


def _head_masks(width=GROUP_WIDTH):
    lane = lax.broadcasted_iota(jnp.int32, (1, width), 1)
    return [(lane >= h * HEAD_DIM) & (lane < (h + 1) * HEAD_DIM) for h in range(HEADS_PER_GROUP)]


def _ffn_kernel(*refs, pre_proj, final_norm):
    refs = list(refs)
    x_ref = refs.pop(0)
    if pre_proj:
        a_ref, wp_ref = refs.pop(0), refs.pop(0)
    g_ref, wg_ref, wu_ref, wd_ref = refs.pop(0), refs.pop(0), refs.pop(0), refs.pop(0)
    if final_norm:
        fg_ref = refs.pop(0)
    o_ref = refs.pop(0)

    x = x_ref[...]
    if pre_proj:
        x = x + _dot(a_ref[...], wp_ref[...])
    h = _rms(x, g_ref[...]).astype(BF16)
    gate = _dot(h, wg_ref[...])
    up = _dot(h, wu_ref[...])
    act = (gate * jax.nn.sigmoid(gate) * up).astype(BF16)
    x = x + 0.5 * _dot(act, wd_ref[...])
    if final_norm:
        x = _rms(x, fg_ref[...])
    o_ref[...] = x


def _ffn(x, norm_g, wg, wu, wd, *, tm, pre=None, final_g=None):
    rows = x.shape[0]
    args = [x]
    specs = [_rows_spec(tm, D_MODEL)]
    if pre is not None:
        a, wp = pre
        args += [a, wp]
        specs += [_rows_spec(tm, D_MODEL), _const_spec((D_MODEL, D_MODEL))]
    args += [norm_g, wg, wu, wd]
    specs += [_const_spec((1, D_MODEL)), _const_spec((D_MODEL, D_FF)), _const_spec((D_MODEL, D_FF)),
              _const_spec((D_FF, D_MODEL))]
    if final_g is not None:
        args.append(final_g)
        specs.append(_const_spec((1, D_MODEL)))
    return pl.pallas_call(
        functools.partial(_ffn_kernel, pre_proj=pre is not None, final_norm=final_g is not None),
        grid=(rows // tm,),
        in_specs=specs,
        out_specs=_rows_spec(tm, D_MODEL),
        out_shape=jax.ShapeDtypeStruct((rows, D_MODEL), F32),
        compiler_params=_params(1),
        name="ffn",
    )(*args)


_Q_END = ATTN_WIDTH
_K_END = 2 * ATTN_WIDTH
_V_END = 3 * ATTN_WIDTH
_UA_END = _V_END + CONV_DIM
_UB_END = _UA_END + CONV_DIM
_GA_END = _UB_END + D_MODEL
IN_WIDTH = _GA_END + D_MODEL


def _win_kernel(x_ref, g_ref, w_ref, *out_refs, t_groups):
    q0, q1, q2, k0, k1, k2, v0, v1, v2, u_ref, ga_ref, gb_ref = out_refs[:12]
    t_refs = out_refs[12:]
    h = _rms(x_ref[...], g_ref[...]).astype(BF16)

    def seg(lo, hi):
        return _dot(h, w_ref[:, lo:hi])

    for g, (q_ref, k_ref, v_ref) in enumerate(((q0, k0, v0), (q1, k1, v1), (q2, k2, v2))):
        lo = g * GROUP_WIDTH
        q_ref[...] = seg(lo, lo + GROUP_WIDTH) * (HEAD_DIM ** -0.5)
        k = seg(_Q_END + lo, _Q_END + lo + GROUP_WIDTH)
        v = seg(_K_END + lo, _K_END + lo + GROUP_WIDTH)
        k_ref[...] = k
        v_ref[...] = v
        if g in t_groups:
            j = t_groups.index(g)
            t_refs[2 * j][0] = k.T
            t_refs[2 * j + 1][0] = v.T
    u_ref[...] = seg(_V_END, _UA_END) * jax.nn.sigmoid(seg(_UA_END, _UB_END))
    ga_ref[...] = jax.nn.sigmoid(seg(_UB_END, _GA_END))
    gb_ref[...] = jax.nn.sigmoid(seg(_GA_END, IN_WIDTH))


def _win(x, norm_g, w_in, *, tm, rows_per_seq, t_groups):
    rows = x.shape[0]
    n_t = 2 * len(t_groups)
    tiles_per_seq = rows_per_seq // tm
    grp = jax.ShapeDtypeStruct((rows, GROUP_WIDTH), F32)
    out_shape = ([grp] * 9
                 + [jax.ShapeDtypeStruct((rows, CONV_DIM), F32)]
                 + [jax.ShapeDtypeStruct((rows, D_MODEL), F32)] * 2
                 + [jax.ShapeDtypeStruct((rows // rows_per_seq, GROUP_WIDTH, rows_per_seq), F32)] * n_t)
    t_spec = pl.BlockSpec((1, GROUP_WIDTH, tm), lambda i: (i // tiles_per_seq, 0, i % tiles_per_seq))
    out_specs = ([_rows_spec(tm, GROUP_WIDTH)] * 9 + [_rows_spec(tm, CONV_DIM)]
                 + [_rows_spec(tm, D_MODEL)] * 2 + [t_spec] * n_t)
    return pl.pallas_call(
        functools.partial(_win_kernel, t_groups=tuple(t_groups)),
        grid=(rows // tm,),
        in_specs=[_rows_spec(tm, D_MODEL), _const_spec((1, D_MODEL)), _const_spec((D_MODEL, IN_WIDTH))],
        out_specs=out_specs,
        out_shape=out_shape,
        compiler_params=_params(1),
        name="w_in",
    )(x, norm_g, w_in)


def _rel_bucket(dist):
    n = jnp.maximum(dist, 0)
    max_exact = N_BUCKETS // 2
    nf = jnp.maximum(n, 1).astype(F32)
    large = max_exact + (jnp.log(nf / max_exact) / math.log(MAX_DISTANCE / max_exact)
                         * (N_BUCKETS - max_exact)).astype(jnp.int32)
    return jnp.where(n < max_exact, n, jnp.minimum(large, N_BUCKETS - 1))


def _tap_bias(rel_bias, g, dil):
    bias_g = rel_bias[:, g * HEADS_PER_GROUP:(g + 1) * HEADS_PER_GROUP]
    return bias_g[_rel_bucket(jnp.arange(SPAN, -1, -1) * dil)].astype(F32).T


def _band_bias(tap_bias):
    period = 3 * SPAN
    vec = jnp.concatenate([tap_bias, jnp.full((HEADS_PER_GROUP, period - (SPAN + 1)), NEG_INF, F32)], axis=1)
    flat = jnp.tile(vec, (1, SPAN))[:, :SPAN * (period - 1)]
    return flat.reshape(HEADS_PER_GROUP, SPAN, period - 1)[:, :, :2 * SPAN].reshape(HEADS_PER_GROUP * SPAN, 2 * SPAN)


def _cache_bias(tap_bias, dil, cache_len, n_new):
    rev = lax.pad(tap_bias, jnp.float32(NEG_INF), [(0, 0, 0), (cache_len + n_new - 1 - SPAN * dil, 0, dil - 1)])
    bias_c = jnp.stack([rev[:, n_new - 1 - t:n_new - 1 - t + cache_len] for t in range(n_new)], axis=1)
    near = jnp.concatenate([rev[:, cache_len:], jnp.full((HEADS_PER_GROUP, LANES), NEG_INF, F32)], axis=1)
    bias_n = jnp.stack([near[:, n_new - 1 - t:n_new - 1 - t + LANES] for t in range(n_new)], axis=1)
    return (bias_c.reshape(HEADS_PER_GROUP * n_new, cache_len), bias_n.reshape(HEADS_PER_GROUP * n_new, LANES))


MAX_ROW_STRIDE = 4


def _gather_residues(src, store, tmp_ref, dil, seq):
    if dil == 1:
        store(0, src[...])
    elif dil <= MAX_ROW_STRIDE:
        for r in range(dil):
            store(r, src[pl.ds(r, seq, stride=dil), :])
    else:
        inner = dil // MAX_ROW_STRIDE
        assert inner <= MAX_ROW_STRIDE and inner * MAX_ROW_STRIDE == dil
        n_part = seq * inner
        for c in range(MAX_ROW_STRIDE):
            tmp_ref[c * n_part:(c + 1) * n_part, :] = src[pl.ds(c, n_part, stride=MAX_ROW_STRIDE), :]
        for c in range(MAX_ROW_STRIDE):
            for c2 in range(inner):
                store(c + MAX_ROW_STRIDE * c2, tmp_ref[pl.ds(c * n_part + c2, seq, stride=inner), :])


def _scatter_residues(load, dst, tmp_ref, dil, seq):
    if dil == 1:
        dst[...] = load(0)
    elif dil <= MAX_ROW_STRIDE:
        for r in range(dil):
            dst[pl.ds(r, seq, stride=dil), :] = load(r)
    else:
        inner = dil // MAX_ROW_STRIDE
        assert inner <= MAX_ROW_STRIDE and inner * MAX_ROW_STRIDE == dil
        n_part = seq * inner
        for c in range(MAX_ROW_STRIDE):
            for c2 in range(inner):
                tmp_ref[pl.ds(c * n_part + c2, seq, stride=inner), :] = load(c + MAX_ROW_STRIDE * c2)
        for c in range(MAX_ROW_STRIDE):
            dst[pl.ds(c, n_part, stride=MAX_ROW_STRIDE), :] = tmp_ref[c * n_part:(c + 1) * n_part, :]


def _band_attn_kernel(q_lo, q_hi, k_lo, k_hi, v_lo, v_hi, b_ref, o_lo, o_hi, l_lo, l_hi,
                      qb_ref, kb_ref, vb_ref, ob_ref, lb_ref, tmp_ref, *, dil, seq):
    n_blk = seq // SPAN
    halves = (slice(0, LANES), slice(LANES, 2 * LANES))
    for dst, srcs in ((qb_ref, (q_lo, q_hi)), (kb_ref, (k_lo, k_hi)), (vb_ref, (v_lo, v_hi))):
        for lanes, src in zip(halves, srcs):
            def store(r, rows, dst=dst, lanes=lanes):
                dst[r, :, lanes] = rows.astype(BF16)
            _gather_residues(src, store, tmp_ref, dil, seq)
    masks = _head_masks()

    def block(q, kb, vb, key_lo):
        qm = jnp.concatenate([jnp.where(masks[h], q, jnp.zeros_like(q)) for h in range(HEADS_PER_GROUP)], axis=0)
        s = _dot_nt(qm, kb) + b_ref[:, key_lo:]
        m = jnp.max(s, axis=-1, keepdims=True)
        p = jnp.exp(s - m)
        den = jnp.sum(p, axis=-1, keepdims=True)
        acc = _dot(p.astype(BF16), vb) * (1.0 / den)
        lse_rows = m + jnp.log(den)
        o = acc[0:SPAN]
        lse = jnp.broadcast_to(lse_rows[0:SPAN], (SPAN, GROUP_WIDTH))
        for h in range(1, HEADS_PER_GROUP):
            o = jnp.where(masks[h], acc[h * SPAN:(h + 1) * SPAN], o)
            lse = jnp.where(masks[h], lse_rows[h * SPAN:(h + 1) * SPAN], lse)
        return o, lse

    def first_block(r):
        o, lse = block(qb_ref[r, 0:SPAN, :], kb_ref[r, 0:SPAN, :], vb_ref[r, 0:SPAN, :], SPAN)
        ob_ref[r, 0:SPAN, :] = o
        lb_ref[r, 0:SPAN, :] = lse

    def later_block(r, j):
        q_start = j * SPAN
        k_start = (j - 1) * SPAN
        if not isinstance(j, int):
            q_start = pl.multiple_of(q_start, SPAN)
            k_start = pl.multiple_of(k_start, SPAN)
        o, lse = block(qb_ref[r, pl.ds(q_start, SPAN), :], kb_ref[r, pl.ds(k_start, 2 * SPAN), :],
                       vb_ref[r, pl.ds(k_start, 2 * SPAN), :], 0)
        ob_ref[r, pl.ds(q_start, SPAN), :] = o
        lb_ref[r, pl.ds(q_start, SPAN), :] = lse

    for r in range(dil):
        first_block(r)
        for j in range(1, n_blk):
            later_block(r, j)

    for src, dsts in ((ob_ref, (o_lo, o_hi)), (lb_ref, (l_lo, l_hi))):
        for lanes, dst in zip(halves, dsts):
            _scatter_residues(lambda r, src=src, lanes=lanes: src[r, :, lanes], dst, tmp_ref, dil, seq)


def _band_attn(q, k, v, bias, *, batch, seq_len, dil):
    seq = seq_len // dil
    view = lambda a: a.reshape(batch, seq_len, GROUP_WIDTH)
    in_halves = [pl.BlockSpec((None, seq_len, LANES), lambda b, hf=hf: (b, 0, hf)) for hf in (0, 1)]
    out_half = pl.BlockSpec((None, seq_len, LANES), lambda b: (b, 0, 0))
    o_lo, o_hi, l_lo, l_hi = pl.pallas_call(
        functools.partial(_band_attn_kernel, dil=dil, seq=seq),
        grid=(batch,),
        in_specs=in_halves * 3 + [_const_spec((HEADS_PER_GROUP * SPAN, 2 * SPAN))],
        out_specs=[out_half] * 4,
        out_shape=[jax.ShapeDtypeStruct((batch, seq_len, LANES), F32)] * 4,
        scratch_shapes=([pltpu.VMEM((dil, seq, GROUP_WIDTH), BF16)] * 3 + [pltpu.VMEM((dil, seq, GROUP_WIDTH), F32)] * 2
                        + [pltpu.VMEM((seq_len, LANES), F32)]),
        compiler_params=_params(1),
        name="band_attn_d%d" % dil,
    )(view(q), view(q), view(k), view(k), view(v), view(v), bias)
    flat = lambda a: a.reshape(batch * seq_len, LANES)
    return (flat(o_lo), flat(o_hi)), (flat(l_lo), flat(l_hi))


def _cache_attn_kernel(*refs, n_new):
    n_g = len(GROUPS)
    q_refs, kn_refs, vn_refs = refs[0:n_g], refs[n_g:2 * n_g], refs[2 * n_g:3 * n_g]
    knt_refs, vnt_refs = refs[3 * n_g:4 * n_g], refs[4 * n_g:5 * n_g]
    kc_refs, vc_refs = refs[5 * n_g:6 * n_g], refs[6 * n_g:7 * n_g]
    bc_refs, bn_refs = refs[7 * n_g:8 * n_g], refs[8 * n_g:9 * n_g]
    outs = refs[9 * n_g:]
    ko_refs, vo_refs = outs[0:n_g], outs[n_g:2 * n_g]
    o_refs, l_refs = outs[2 * n_g:4 * n_g], outs[4 * n_g:6 * n_g]
    masks = _head_masks()
    pad = jnp.zeros((SPAN - n_new, GROUP_WIDTH), F32)
    lane = lax.broadcasted_iota(jnp.int32, (1, LANES), 1)
    j = pl.program_id(0) % (LANES // n_new)
    to_tail = LANES - n_new - n_new * j

    for g in range(n_g):
        kc, vc = kc_refs[g][0], vc_refs[g][0]
        kn, vn = kn_refs[g][0], vn_refs[g][0]
        cache_len = kc.shape[1]
        for src, new_t, dst in ((kc, knt_refs[g][0], ko_refs[g]), (vc, vnt_refs[g][0], vo_refs[g])):
            shifted = pltpu.roll(src, cache_len - n_new, axis=1)
            new_tail = pltpu.roll(new_t, to_tail, axis=1)
            if cache_len > LANES:
                dst[0, :, 0:cache_len - LANES] = shifted[:, 0:cache_len - LANES]
            dst[0, :, cache_len - LANES:cache_len] = jnp.where(lane >= LANES - n_new, new_tail,
                                                              shifted[:, cache_len - LANES:cache_len])
        q = q_refs[g][0]
        qm = jnp.concatenate([jnp.where(masks[h], q, 0.0) for h in range(HEADS_PER_GROUP)], axis=0).astype(BF16)
        kn_p = jnp.concatenate([kn, pad], axis=0).astype(BF16)
        vn_p = jnp.concatenate([vn, pad], axis=0).astype(BF16)
        s_c = _dot(qm, kc.astype(BF16)) + bc_refs[g][...]
        s_n = _dot_nt(qm, kn_p) + bn_refs[g][...]
        m = jnp.maximum(jnp.max(s_c, axis=-1, keepdims=True), jnp.max(s_n, axis=-1, keepdims=True))
        p_c = jnp.exp(s_c - m)
        p_n = jnp.exp(s_n - m)
        den = jnp.sum(p_c, axis=-1, keepdims=True) + jnp.sum(p_n, axis=-1, keepdims=True)
        acc = _dot_nt(p_c.astype(BF16), vc.astype(BF16)) + _dot(p_n.astype(BF16), vn_p)
        acc = acc * (1.0 / den)
        lse_rows = m + jnp.log(den)
        o = jnp.zeros((n_new, GROUP_WIDTH), F32)
        lse = jnp.zeros((n_new, GROUP_WIDTH), F32)
        for h in range(HEADS_PER_GROUP):
            rows = slice(h * n_new, (h + 1) * n_new)
            o = jnp.where(masks[h], acc[rows], o)
            lse = jnp.where(masks[h], lse_rows[rows], lse)
        for hf in (0, 1):
            o_refs[2 * g + hf][0] = o[:, hf * LANES:(hf + 1) * LANES]
            l_refs[2 * g + hf][0] = lse[:, hf * LANES:(hf + 1) * LANES]


def _cache_attn(q, k_new, v_new, k_new_t, v_new_t, k_cache, v_cache, tap_biases, *, batch, n_new):
    n_g = len(GROUPS)
    per_tile = LANES // n_new
    new3 = lambda a: a.reshape(batch, n_new, GROUP_WIDTH)
    new_spec = pl.BlockSpec((1, n_new, GROUP_WIDTH), lambda b: (b, 0, 0))
    new_t_spec = pl.BlockSpec((1, GROUP_WIDTH, LANES), lambda b: (0, 0, b // per_tile))
    cache_specs = [pl.BlockSpec((1, GROUP_WIDTH, c.shape[2]), lambda b: (b, 0, 0)) for c in k_cache]
    biases_c, biases_n = [], []
    for g, (_, dil) in enumerate(GROUPS):
        bc, bn = _cache_bias(tap_biases[g], dil, k_cache[g].shape[2], n_new)
        biases_c.append(bc)
        biases_n.append(bn)
    args = ([new3(a) for a in q] + [new3(a) for a in k_new] + [new3(a) for a in v_new]
            + list(k_new_t) + list(v_new_t) + list(k_cache) + list(v_cache) + biases_c + biases_n)
    in_specs = ([new_spec] * (3 * n_g) + [new_t_spec] * (2 * n_g) + cache_specs * 2
                + [_const_spec(b.shape) for b in biases_c] + [_const_spec(b.shape) for b in biases_n])
    half_spec = pl.BlockSpec((1, n_new, LANES), lambda b: (b, 0, 0))
    out_shape = ([jax.ShapeDtypeStruct(c.shape, F32) for c in k_cache] * 2
                 + [jax.ShapeDtypeStruct((batch, n_new, LANES), F32)] * (4 * n_g))
    out_specs = cache_specs * 2 + [half_spec] * (4 * n_g)
    outs = pl.pallas_call(
        functools.partial(_cache_attn_kernel, n_new=n_new),
        grid=(batch,),
        in_specs=in_specs,
        out_specs=out_specs,
        out_shape=out_shape,
        compiler_params=_params(1),
        name="cache_attn",
    )(*args)
    flat = lambda a: a.reshape(batch * n_new, LANES)
    halves = lambda lst: [(flat(lst[2 * g]), flat(lst[2 * g + 1])) for g in range(n_g)]
    return outs[0:n_g], outs[n_g:2 * n_g], halves(outs[2 * n_g:4 * n_g]), halves(outs[4 * n_g:6 * n_g])


def _conv_taps(ext_ref, w_ref, first_row, n_rows):
    acc = jnp.zeros((n_rows, CONV_DIM), F32)
    for j in range(CONV_WIDTH):
        acc = acc + ext_ref[pl.ds(first_row + j, n_rows), :] * w_ref[j:j + 1, :]
    return acc


def _conv_taps_aligned(ext_ref, w_ref, first_row, n_rows):
    acc = None
    for b in range(SUBLANES):
        part = None
        for a in range((first_row + CONV_WIDTH - 1 - b) // SUBLANES + 1):
            j = SUBLANES * a + b - first_row
            if j < 0:
                continue
            term = ext_ref[pl.ds(SUBLANES * a, n_rows + SUBLANES), :] * w_ref[j:j + 1, :]
            part = term if part is None else part + term
        part = part[b:b + n_rows]
        acc = part if acc is None else acc + part
    return acc


def _conv_prompt_kernel(u_ref, halo_ref, w_ref, y_ref, ext_ref, *, tc):
    i = pl.program_id(1)
    ext_ref[0:CONV_HALO, :] = jnp.where(i > 0, halo_ref[0], 0.0)
    ext_ref[CONV_HALO:CONV_HALO + tc, :] = u_ref[0]
    ext_ref[CONV_HALO + tc:CONV_HALO + tc + SUBLANES, :] = jnp.zeros((SUBLANES, CONV_DIM), F32)
    y_ref[0] = _conv_taps_aligned(ext_ref, w_ref, CONV_HALO - (CONV_WIDTH - 1), tc)


def _conv_prompt(u, w, *, batch, seq_len, tc):
    u3 = u.reshape(batch, seq_len, CONV_DIM)
    halo_per_tile = tc // CONV_HALO
    y = pl.pallas_call(
        functools.partial(_conv_prompt_kernel, tc=tc),
        grid=(batch, seq_len // tc),
        in_specs=[pl.BlockSpec((1, tc, CONV_DIM), lambda b, i: (b, i, 0)),
                  pl.BlockSpec((1, CONV_HALO, CONV_DIM),
                               lambda b, i: (b, jnp.maximum(i * halo_per_tile - 1, 0), 0)),
                  pl.BlockSpec((CONV_WIDTH, CONV_DIM), lambda b, i: (0, 0))],
        out_specs=pl.BlockSpec((1, tc, CONV_DIM), lambda b, i: (b, i, 0)),
        out_shape=jax.ShapeDtypeStruct((batch, seq_len, CONV_DIM), F32),
        scratch_shapes=[pltpu.VMEM((CONV_HALO + tc + SUBLANES, CONV_DIM), F32)],
        compiler_params=_params(2),
        name="conv_prompt",
    )(u3, u3, w)
    return y.reshape(batch * seq_len, CONV_DIM)


def _conv_sample_kernel(u_ref, st_ref, w_ref, y_ref, so_ref, ext_ref, *, n_new):
    hist = CONV_WIDTH - 1
    ext_ref[0:hist, :] = st_ref[0]
    ext_ref[hist:hist + n_new, :] = u_ref[0]
    y_ref[0] = _conv_taps(ext_ref, w_ref, 0, n_new)
    so_ref[0] = ext_ref[n_new:n_new + hist, :]


def _conv_sample(u, state, w, *, batch, n_new):
    hist = CONV_WIDTH - 1
    u3 = u.reshape(batch, n_new, CONV_DIM)
    y, st = pl.pallas_call(
        functools.partial(_conv_sample_kernel, n_new=n_new),
        grid=(batch,),
        in_specs=[pl.BlockSpec((1, n_new, CONV_DIM), lambda b: (b, 0, 0)),
                  pl.BlockSpec((1, hist, CONV_DIM), lambda b: (b, 0, 0)),
                  pl.BlockSpec((CONV_WIDTH, CONV_DIM), lambda b: (0, 0))],
        out_specs=[pl.BlockSpec((1, n_new, CONV_DIM), lambda b: (b, 0, 0)),
                   pl.BlockSpec((1, hist, CONV_DIM), lambda b: (b, 0, 0))],
        out_shape=[jax.ShapeDtypeStruct((batch, n_new, CONV_DIM), F32),
                   jax.ShapeDtypeStruct((batch, hist, CONV_DIM), F32)],
        scratch_shapes=[pltpu.VMEM((hist + n_new + 2, CONV_DIM), F32)],
        compiler_params=_params(1),
        name="conv_sample",
    )(u3, state, w)
    return y.reshape(batch * n_new, CONV_DIM), st


def _mix_kernel(*refs):
    n_half = 2 * len(GROUPS)
    o_refs, l_refs = refs[0:n_half], refs[n_half:2 * n_half]
    (yc_ref, ga_ref, gb_ref, x_ref, wap_ref, cb_ref, lg_ref, lb_ref, wcp_ref, wo_ref, xg_ref, wxq_ref,
     x_out, q_out) = refs[2 * n_half:]
    merged_halves = []
    for hf in (0, 1):
        lses = [l_refs[2 * g + hf][...] for g in range(len(GROUPS))]
        m = functools.reduce(jnp.maximum, lses)
        es = [jnp.exp(l - m) for l in lses]
        num = sum(e * o_refs[2 * g + hf][...] for g, e in enumerate(es))
        merged_halves.append(num * (1.0 / sum(es)))
    attn = jnp.concatenate(merged_halves, axis=-1)
    a_branch = _dot(attn.astype(BF16), wap_ref[...])
    y = yc_ref[...] + cb_ref[...]
    mu = jnp.mean(y, axis=-1, keepdims=True)
    yc = y - mu
    var = jnp.mean(yc * yc, axis=-1, keepdims=True)
    y = yc * lax.rsqrt(var + EPS) * lg_ref[...] + lb_ref[...]
    y = y * jax.nn.sigmoid(y)
    c_branch = _dot(y.astype(BF16), wcp_ref[...])
    merged = ga_ref[...] * a_branch + gb_ref[...] * c_branch
    x = x_ref[...] + _dot(merged.astype(BF16), wo_ref[...])
    x_out[...] = x
    hq = _rms(x, xg_ref[...]).astype(BF16)
    q_out[...] = (_dot(hq, wxq_ref[...]) * (X_HEAD_DIM ** -0.5)).astype(BF16)


def _mix(o, lse, yconv, ga, gb, x, w, *, tm):
    rows = x.shape[0]
    half = _rows_spec(tm, LANES)
    wide = _rows_spec(tm, D_MODEL)
    o = [a for pair in o for a in pair]
    lse = [a for pair in lse for a in pair]
    return pl.pallas_call(
        _mix_kernel,
        grid=(rows // tm,),
        in_specs=[half] * 12 + [_rows_spec(tm, CONV_DIM), wide, wide, wide,
                              _const_spec((GROUP_WIDTH, D_MODEL)), _const_spec((1, CONV_DIM)),
                              _const_spec((1, CONV_DIM)), _const_spec((1, CONV_DIM)),
                              _const_spec((CONV_DIM, D_MODEL)), _const_spec((D_MODEL, D_MODEL)),
                              _const_spec((1, D_MODEL)), _const_spec((D_MODEL, D_MODEL))],
        out_specs=[wide, wide],
        out_shape=[jax.ShapeDtypeStruct((rows, D_MODEL), F32), jax.ShapeDtypeStruct((rows, D_MODEL), BF16)],
        compiler_params=_params(1),
        name="mix",
    )(*o, *lse, yconv, ga, gb, x, w['w_attn_proj'], w['conv_dw_b'], w['conv_ln_g'], w['conv_ln_b'],
      w['w_conv_proj'], w['w_o'], w['xattn_norm'], w['w_xq'])


def _memkv_kernel(m_ref, g_ref, w_ref, k_ref, v_ref):
    h = _rms(m_ref[...], g_ref[...]).astype(BF16)
    for hd in range(X_HEADS):
        lo = hd * X_HEAD_DIM
        k_ref[0, 0, :, hd, :] = _dot(h, w_ref[:, lo:lo + X_HEAD_DIM])
        v_ref[0, 0, :, hd, :] = _dot(h, w_ref[:, D_MODEL + lo:D_MODEL + lo + X_HEAD_DIM])


def _memkv(mem, norm_g, w_xkv, *, batch):
    out_spec = pl.BlockSpec((1, 1, N_MEM, X_HEADS, X_HEAD_DIM), lambda b: (0, b, 0, 0, 0))
    return pl.pallas_call(
        _memkv_kernel,
        grid=(batch,),
        in_specs=[_rows_spec(N_MEM, D_MODEL), _const_spec((1, D_MODEL)), _const_spec((D_MODEL, 2 * D_MODEL))],
        out_specs=[out_spec, out_spec],
        out_shape=[jax.ShapeDtypeStruct((1, batch, N_MEM, X_HEADS, X_HEAD_DIM), F32)] * 2,
        compiler_params=_params(1),
        name="memory_kv",
    )(mem, norm_g, w_xkv)


def _xattn_kernel(q_ref, k_ref, v_ref, o_ref):
    for h in range(X_HEADS):
        lanes = slice(h * X_HEAD_DIM, (h + 1) * X_HEAD_DIM)
        s = _dot_nt(q_ref[0, :, lanes], k_ref[0, 0, :, h, :].astype(BF16))
        m = jnp.max(s, axis=-1, keepdims=True)
        p = jnp.exp(s - m)
        den = jnp.sum(p, axis=-1, keepdims=True)
        acc = _dot(p.astype(BF16), v_ref[0, 0, :, h, :].astype(BF16))
        o_ref[0, :, lanes] = (acc * (1.0 / den)).astype(BF16)


def _xattn(q, mk, mv, *, batch, tq):
    rpb = q.shape[1]
    q_spec = pl.BlockSpec((1, tq, D_MODEL), lambda b, i: (b, i, 0))
    m_spec = pl.BlockSpec((1, 1, N_MEM, X_HEADS, X_HEAD_DIM), lambda b, i: (0, b, 0, 0, 0))
    return pl.pallas_call(
        _xattn_kernel,
        grid=(batch, rpb // tq),
        in_specs=[q_spec, m_spec, m_spec],
        out_specs=q_spec,
        out_shape=jax.ShapeDtypeStruct((batch, rpb, D_MODEL), BF16),
        compiler_params=_params(2),
        name="cross_attn",
    )(q, mk, mv)


def _layer_tail(x1, o, lse, yconv, ga, gb, mk, mv, w, final_g, *, batch, tm, tq):
    rows = x1.shape[0]
    x2, xq = _mix(o, lse, yconv, ga, gb, x1, w, tm=tm)
    xo = _xattn(xq.reshape(batch, rows // batch, D_MODEL), mk, mv, batch=batch, tq=tq)
    return _ffn(x2, w['ffn2_norm'], w['ffn2_w_gate'], w['ffn2_w_up'], w['ffn2_w_down'], tm=tm,
                pre=(xo.reshape(rows, D_MODEL), w['w_xo']), final_g=final_g)


def _to_positions_heads(a_t, batch, n_pos):
    return jnp.transpose(a_t.reshape(batch, HEADS_PER_GROUP, HEAD_DIM, n_pos), (0, 3, 1, 2))[None]


def _to_feature_major(a, batch, n_pos):
    return jnp.transpose(a[0], (0, 2, 3, 1)).reshape(batch, GROUP_WIDTH, n_pos)


def kernel(x_prompt, x_sample, mem_prompt, cache_win0_k, cache_win0_v, cache_win1_k, cache_win1_v, cache_win2_k, cache_win2_v, state_conv, cache_mem_k, cache_mem_v, rel_bias, ffn1_norm, ffn1_w_gate, ffn1_w_up, ffn1_w_down, mix_norm, w_in, w_attn_proj, conv_dw_w, conv_dw_b, conv_ln_g, conv_ln_b, w_conv_proj, w_o, xattn_norm, mem_norm, w_xq, w_xkv, w_xo, ffn2_norm, ffn2_w_gate, ffn2_w_up, ffn2_w_down, final_norm):
    batch, seq_len, _ = x_prompt.shape
    dec_batch, dec_seq, _ = x_sample.shape
    n_g = len(GROUPS)
    assert ffn1_norm.shape[0] == 1, "single layer"
    mat = lambda a: a[0].astype(BF16)
    vec = lambda a: a[0].reshape(1, -1)
    w = dict(ffn1_norm=vec(ffn1_norm), ffn1_w_gate=mat(ffn1_w_gate), ffn1_w_up=mat(ffn1_w_up),
             ffn1_w_down=mat(ffn1_w_down), mix_norm=vec(mix_norm), w_in=mat(w_in),
             w_attn_proj=mat(w_attn_proj), conv_dw_w=conv_dw_w[0], conv_dw_b=vec(conv_dw_b),
             conv_ln_g=vec(conv_ln_g), conv_ln_b=vec(conv_ln_b), w_conv_proj=mat(w_conv_proj),
             w_o=mat(w_o), xattn_norm=vec(xattn_norm), mem_norm=vec(mem_norm), w_xq=mat(w_xq),
             w_xkv=mat(w_xkv), w_xo=mat(w_xo), ffn2_norm=vec(ffn2_norm), ffn2_w_gate=mat(ffn2_w_gate),
             ffn2_w_up=mat(ffn2_w_up), ffn2_w_down=mat(ffn2_w_down))
    final_g = final_norm.reshape(1, -1)
    tap_biases = [_tap_bias(rel_bias, g, dil) for g, (_, dil) in enumerate(GROUPS)]

    def head(x, tm, rows_per_seq, t_groups):
        x1 = _ffn(x, w['ffn1_norm'], w['ffn1_w_gate'], w['ffn1_w_up'], w['ffn1_w_down'], tm=tm)
        outs = _win(x1, w['mix_norm'], w['w_in'], tm=tm, rows_per_seq=rows_per_seq, t_groups=t_groups)
        return x1, outs[0:3], outs[3:6], outs[6:9], outs[9], outs[10], outs[11], outs[12:]

    rows_p = batch * seq_len
    tm_p = 512
    full_groups = [g for g, (window, _) in enumerate(GROUPS) if window >= seq_len]
    x1, q, k, v, u, ga, gb, kv_t = head(x_prompt.reshape(rows_p, D_MODEL), tm_p, seq_len, full_groups)
    o, lse = [], []
    for g, (_, dil) in enumerate(GROUPS):
        o_g, l_g = _band_attn(q[g], k[g], v[g], _band_bias(tap_biases[g]), batch=batch, seq_len=seq_len, dil=dil)
        o.append(o_g)
        lse.append(l_g)
    yconv = _conv_prompt(u, w['conv_dw_w'], batch=batch, seq_len=seq_len, tc=256)
    mk_p, mv_p = _memkv(mem_prompt.reshape(batch * N_MEM, D_MODEL), w['mem_norm'], w['w_xkv'], batch=batch)
    y_prompt = _layer_tail(x1, o, lse, yconv, ga, gb, mk_p, mv_p, w, final_g, batch=batch, tm=tm_p, tq=512)

    p_win = []
    for g, (window, _) in enumerate(GROUPS):
        keep = min(window, seq_len)
        if g in full_groups:
            j = full_groups.index(g)
            p_win += [_to_positions_heads(kv_t[2 * j], batch, seq_len), _to_positions_heads(kv_t[2 * j + 1], batch, seq_len)]
        else:
            for a in (k[g], v[g]):
                p_win.append(a.reshape(batch, seq_len, GROUP_WIDTH)[:, seq_len - keep:]
                             .reshape(1, batch, keep, HEADS_PER_GROUP, HEAD_DIM))
    p_conv = u.reshape(batch, seq_len, CONV_DIM)[:, seq_len - (CONV_WIDTH - 1):][None]

    rows_s = dec_batch * dec_seq
    x1, q, k, v, u, ga, gb, kv_t = head(x_sample.reshape(rows_s, D_MODEL), rows_s, rows_s, list(range(n_g)))
    caches_k = [_to_feature_major(c, dec_batch, c.shape[2]) for c in (cache_win0_k, cache_win1_k, cache_win2_k)]
    caches_v = [_to_feature_major(c, dec_batch, c.shape[2]) for c in (cache_win0_v, cache_win1_v, cache_win2_v)]
    new_k, new_v, o, lse = _cache_attn(q, k, v, kv_t[0::2], kv_t[1::2], caches_k, caches_v, tap_biases,
                                       batch=dec_batch, n_new=dec_seq)
    yconv, s_conv = _conv_sample(u, state_conv[0], w['conv_dw_w'], batch=dec_batch, n_new=dec_seq)
    y_sample = _layer_tail(x1, o, lse, yconv, ga, gb, cache_mem_k, cache_mem_v, w, final_g,
                           batch=dec_batch, tm=rows_s, tq=dec_seq)

    s_win = []
    for g in range(n_g):
        for a in (new_k[g], new_v[g]):
            s_win.append(_to_positions_heads(a, dec_batch, a.shape[2]))

    return (y_prompt.reshape(batch, seq_len, D_MODEL), y_sample.reshape(dec_batch, dec_seq, D_MODEL),
            *p_win, p_conv, mk_p, mv_p, *s_win, s_conv[None])
```

```python
import functools
import math

import jax
import jax.numpy as jnp
import numpy as np
from jax import lax
from jax.experimental import pallas as pl
from jax.experimental.pallas import tpu as pltpu

D_MODEL = 1024
HEAD_DIM = 64
GROUPS = ((128, 1), (512, 4), (2048, 16))
HEADS_PER_GROUP = 4
GROUP_WIDTH = HEADS_PER_GROUP * HEAD_DIM
ATTN_WIDTH = len(GROUPS) * GROUP_WIDTH
CONV_DIM = D_MODEL // 2
CONV_WIDTH = 31
CONV_HALO = 32
N_BUCKETS = 32
MAX_DISTANCE = 2048
D_FF = ((8 * D_MODEL // 3 + 127) // 128) * 128
N_MEM = 256
X_HEADS = 4
X_HEAD_DIM = D_MODEL // X_HEADS
EPS = 1e-6
NEG_INF = -1e30
SPAN = 128
LANES = 128
SUBLANES = 8

V7X_VMEM_LIMIT_BYTES = 56 * 1024 * 1024
BF16 = jnp.bfloat16
F32 = jnp.float32


def _params(n_axes):
    return pltpu.CompilerParams(dimension_semantics=("parallel",) * n_axes,
                                vmem_limit_bytes=V7X_VMEM_LIMIT_BYTES)


def _const_spec(shape):
    return pl.BlockSpec(shape, lambda *_: (0,) * len(shape), pipeline_mode=pl.Buffered(1))


def _rows_spec(tm, width):
    return pl.BlockSpec((tm, width), lambda i: (i, 0))


def _rms(x, g):
    return x * lax.rsqrt(jnp.mean(x * x, axis=-1, keepdims=True) + EPS) * g


def _dot(a, b):
    return jnp.dot(a, b, preferred_element_type=F32)


def _dot_nt(a, b):
    return lax.dot_general(a, b, (((1,), (1,)), ((), ())), preferred_element_type=F32)


def _head_masks(width=GROUP_WIDTH):
    lane = lax.broadcasted_iota(jnp.int32, (1, width), 1)
    return [(lane >= h * HEAD_DIM) & (lane < (h + 1) * HEAD_DIM) for h in range(HEADS_PER_GROUP)]


def _ffn_kernel(*refs, pre_proj, final_norm):
    refs = list(refs)
    x_ref = refs.pop(0)
    if pre_proj:
        a_ref, wp_ref = refs.pop(0), refs.pop(0)
    g_ref, wg_ref, wu_ref, wd_ref = refs.pop(0), refs.pop(0), refs.pop(0), refs.pop(0)
    if final_norm:
        fg_ref = refs.pop(0)
    o_ref = refs.pop(0)

    x = x_ref[...]
    if pre_proj:
        x = x + _dot(a_ref[...], wp_ref[...])
    h = _rms(x, g_ref[...]).astype(BF16)
    gate = _dot(h, wg_ref[...])
    up = _dot(h, wu_ref[...])
    act = (gate * jax.nn.sigmoid(gate) * up).astype(BF16)
    x = x + 0.5 * _dot(act, wd_ref[...])
    if final_norm:
        x = _rms(x, fg_ref[...])
    o_ref[...] = x


def _ffn(x, norm_g, wg, wu, wd, *, tm, pre=None, final_g=None):
    rows = x.shape[0]
    args = [x]
    specs = [_rows_spec(tm, D_MODEL)]
    if pre is not None:
        a, wp = pre
        args += [a, wp]
        specs += [_rows_spec(tm, D_MODEL), _const_spec((D_MODEL, D_MODEL))]
    args += [norm_g, wg, wu, wd]
    specs += [_const_spec((1, D_MODEL)), _const_spec((D_MODEL, D_FF)), _const_spec((D_MODEL, D_FF)),
              _const_spec((D_FF, D_MODEL))]
    if final_g is not None:
        args.append(final_g)
        specs.append(_const_spec((1, D_MODEL)))
    return pl.pallas_call(
        functools.partial(_ffn_kernel, pre_proj=pre is not None, final_norm=final_g is not None),
        grid=(rows // tm,),
        in_specs=specs,
        out_specs=_rows_spec(tm, D_MODEL),
        out_shape=jax.ShapeDtypeStruct((rows, D_MODEL), F32),
        compiler_params=_params(1),
        name="ffn",
    )(*args)


_Q_END = ATTN_WIDTH
_K_END = 2 * ATTN_WIDTH
_V_END = 3 * ATTN_WIDTH
_UA_END = _V_END + CONV_DIM
_UB_END = _UA_END + CONV_DIM
_GA_END = _UB_END + D_MODEL
IN_WIDTH = _GA_END + D_MODEL


def _win_kernel(x_ref, g_ref, w_ref, *out_refs, t_groups):
    q0, q1, q2, k0, k1, k2, v0, v1, v2, u_ref, ga_ref, gb_ref = out_refs[:12]
    t_refs = out_refs[12:]
    h = _rms(x_ref[...], g_ref[...]).astype(BF16)

    def seg(lo, hi):
        return _dot(h, w_ref[:, lo:hi])

    for g, (q_ref, k_ref, v_ref) in enumerate(((q0, k0, v0), (q1, k1, v1), (q2, k2, v2))):
        lo = g * GROUP_WIDTH
        q_ref[...] = seg(lo, lo + GROUP_WIDTH) * (HEAD_DIM ** -0.5)
        k = seg(_Q_END + lo, _Q_END + lo + GROUP_WIDTH)
        v = seg(_K_END + lo, _K_END + lo + GROUP_WIDTH)
        k_ref[...] = k
        v_ref[...] = v
        if g in t_groups:
            j = t_groups.index(g)
            t_refs[2 * j][0] = k.T
            t_refs[2 * j + 1][0] = v.T
    u_ref[...] = seg(_V_END, _UA_END) * jax.nn.sigmoid(seg(_UA_END, _UB_END))
    ga_ref[...] = jax.nn.sigmoid(seg(_UB_END, _GA_END))
    gb_ref[...] = jax.nn.sigmoid(seg(_GA_END, IN_WIDTH))


def _win(x, norm_g, w_in, *, tm, rows_per_seq, t_groups):
    rows = x.shape[0]
    n_t = 2 * len(t_groups)
    tiles_per_seq = rows_per_seq // tm
    grp = jax.ShapeDtypeStruct((rows, GROUP_WIDTH), F32)
    out_shape = ([grp] * 9
                 + [jax.ShapeDtypeStruct((rows, CONV_DIM), F32)]
                 + [jax.ShapeDtypeStruct((rows, D_MODEL), F32)] * 2
                 + [jax.ShapeDtypeStruct((rows // rows_per_seq, GROUP_WIDTH, rows_per_seq), F32)] * n_t)
    t_spec = pl.BlockSpec((1, GROUP_WIDTH, tm), lambda i: (i // tiles_per_seq, 0, i % tiles_per_seq))
    out_specs = ([_rows_spec(tm, GROUP_WIDTH)] * 9 + [_rows_spec(tm, CONV_DIM)]
                 + [_rows_spec(tm, D_MODEL)] * 2 + [t_spec] * n_t)
    return pl.pallas_call(
        functools.partial(_win_kernel, t_groups=tuple(t_groups)),
        grid=(rows // tm,),
        in_specs=[_rows_spec(tm, D_MODEL), _const_spec((1, D_MODEL)), _const_spec((D_MODEL, IN_WIDTH))],
        out_specs=out_specs,
        out_shape=out_shape,
        compiler_params=_params(1),
        name="w_in",
    )(x, norm_g, w_in)


def _rel_bucket(dist):
    n = jnp.maximum(dist, 0)
    max_exact = N_BUCKETS // 2
    nf = jnp.maximum(n, 1).astype(F32)
    large = max_exact + (jnp.log(nf / max_exact) / math.log(MAX_DISTANCE / max_exact)
                         * (N_BUCKETS - max_exact)).astype(jnp.int32)
    return jnp.where(n < max_exact, n, jnp.minimum(large, N_BUCKETS - 1))


def _tap_bias(rel_bias, g, dil):
    bias_g = rel_bias[:, g * HEADS_PER_GROUP:(g + 1) * HEADS_PER_GROUP]
    return bias_g[_rel_bucket(jnp.arange(SPAN, -1, -1) * dil)].astype(F32).T


def _band_bias(tap_bias):
    period = 3 * SPAN
    vec = jnp.concatenate([tap_bias, jnp.full((HEADS_PER_GROUP, period - (SPAN + 1)), NEG_INF, F32)], axis=1)
    flat = jnp.tile(vec, (1, SPAN))[:, :SPAN * (period - 1)]
    return flat.reshape(HEADS_PER_GROUP, SPAN, period - 1)[:, :, :2 * SPAN].reshape(HEADS_PER_GROUP * SPAN, 2 * SPAN)


def _cache_bias(tap_bias, dil, cache_len, n_new):
    rev = lax.pad(tap_bias, jnp.float32(NEG_INF), [(0, 0, 0), (cache_len + n_new - 1 - SPAN * dil, 0, dil - 1)])
    bias_c = jnp.stack([rev[:, n_new - 1 - t:n_new - 1 - t + cache_len] for t in range(n_new)], axis=1)
    near = jnp.concatenate([rev[:, cache_len:], jnp.full((HEADS_PER_GROUP, LANES), NEG_INF, F32)], axis=1)
    bias_n = jnp.stack([near[:, n_new - 1 - t:n_new - 1 - t + LANES] for t in range(n_new)], axis=1)
    return (bias_c.reshape(HEADS_PER_GROUP * n_new, cache_len), bias_n.reshape(HEADS_PER_GROUP * n_new, LANES))


MAX_ROW_STRIDE = 4


def _gather_residues(src, store, tmp_ref, dil, seq):
    if dil == 1:
        store(0, src[...])
    elif dil <= MAX_ROW_STRIDE:
        for r in range(dil):
            store(r, src[pl.ds(r, seq, stride=dil), :])
    else:
        inner = dil // MAX_ROW_STRIDE
        assert inner <= MAX_ROW_STRIDE and inner * MAX_ROW_STRIDE == dil
        n_part = seq * inner
        for c in range(MAX_ROW_STRIDE):
            tmp_ref[c * n_part:(c + 1) * n_part, :] = src[pl.ds(c, n_part, stride=MAX_ROW_STRIDE), :]
        for c in range(MAX_ROW_STRIDE):
            for c2 in range(inner):
                store(c + MAX_ROW_STRIDE * c2, tmp_ref[pl.ds(c * n_part + c2, seq, stride=inner), :])


def _scatter_residues(load, dst, tmp_ref, dil, seq):
    if dil == 1:
        dst[...] = load(0)
    elif dil <= MAX_ROW_STRIDE:
        for r in range(dil):
            dst[pl.ds(r, seq, stride=dil), :] = load(r)
    else:
        inner = dil // MAX_ROW_STRIDE
        assert inner <= MAX_ROW_STRIDE and inner * MAX_ROW_STRIDE == dil
        n_part = seq * inner
        for c in range(MAX_ROW_STRIDE):
            for c2 in range(inner):
                tmp_ref[pl.ds(c * n_part + c2, seq, stride=inner), :] = load(c + MAX_ROW_STRIDE * c2)
        for c in range(MAX_ROW_STRIDE):
            dst[pl.ds(c, n_part, stride=MAX_ROW_STRIDE), :] = tmp_ref[c * n_part:(c + 1) * n_part, :]


def _band_attn_kernel(q_lo, q_hi, k_lo, k_hi, v_lo, v_hi, b_ref, o_lo, o_hi, l_lo, l_hi,
                      qb_ref, kb_ref, vb_ref, ob_ref, lb_ref, tmp_ref, *, dil, seq):
    n_blk = seq // SPAN
    halves = (slice(0, LANES), slice(LANES, 2 * LANES))
    for dst, srcs in ((qb_ref, (q_lo, q_hi)), (kb_ref, (k_lo, k_hi)), (vb_ref, (v_lo, v_hi))):
        for lanes, src in zip(halves, srcs):
            def store(r, rows, dst=dst, lanes=lanes):
                dst[r, :, lanes] = rows.astype(BF16)
            _gather_residues(src, store, tmp_ref, dil, seq)
    masks = _head_masks()

    def block(q, kb, vb, key_lo):
        qm = jnp.concatenate([jnp.where(masks[h], q, jnp.zeros_like(q)) for h in range(HEADS_PER_GROUP)], axis=0)
        s = _dot_nt(qm, kb) + b_ref[:, key_lo:]
        m = jnp.max(s, axis=-1, keepdims=True)
        p = jnp.exp(s - m)
        den = jnp.sum(p, axis=-1, keepdims=True)
        acc = _dot(p.astype(BF16), vb) * (1.0 / den)
        lse_rows = m + jnp.log(den)
        o = acc[0:SPAN]
        lse = jnp.broadcast_to(lse_rows[0:SPAN], (SPAN, GROUP_WIDTH))
        for h in range(1, HEADS_PER_GROUP):
            o = jnp.where(masks[h], acc[h * SPAN:(h + 1) * SPAN], o)
            lse = jnp.where(masks[h], lse_rows[h * SPAN:(h + 1) * SPAN], lse)
        return o, lse

    def first_block(r):
        o, lse = block(qb_ref[r, 0:SPAN, :], kb_ref[r, 0:SPAN, :], vb_ref[r, 0:SPAN, :], SPAN)
        ob_ref[r, 0:SPAN, :] = o
        lb_ref[r, 0:SPAN, :] = lse

    def later_block(r, j):
        q_start = j * SPAN
        k_start = (j - 1) * SPAN
        if not isinstance(j, int):
            q_start = pl.multiple_of(q_start, SPAN)
            k_start = pl.multiple_of(k_start, SPAN)
        o, lse = block(qb_ref[r, pl.ds(q_start, SPAN), :], kb_ref[r, pl.ds(k_start, 2 * SPAN), :],
                       vb_ref[r, pl.ds(k_start, 2 * SPAN), :], 0)
        ob_ref[r, pl.ds(q_start, SPAN), :] = o
        lb_ref[r, pl.ds(q_start, SPAN), :] = lse

    for r in range(dil):
        first_block(r)
        for j in range(1, n_blk):
            later_block(r, j)

    for src, dsts in ((ob_ref, (o_lo, o_hi)), (lb_ref, (l_lo, l_hi))):
        for lanes, dst in zip(halves, dsts):
            _scatter_residues(lambda r, src=src, lanes=lanes: src[r, :, lanes], dst, tmp_ref, dil, seq)


def _band_attn(q, k, v, bias, *, batch, seq_len, dil):
    seq = seq_len // dil
    view = lambda a: a.reshape(batch, seq_len, GROUP_WIDTH)
    in_halves = [pl.BlockSpec((None, seq_len, LANES), lambda b, hf=hf: (b, 0, hf)) for hf in (0, 1)]
    out_half = pl.BlockSpec((None, seq_len, LANES), lambda b: (b, 0, 0))
    o_lo, o_hi, l_lo, l_hi = pl.pallas_call(
        functools.partial(_band_attn_kernel, dil=dil, seq=seq),
        grid=(batch,),
        in_specs=in_halves * 3 + [_const_spec((HEADS_PER_GROUP * SPAN, 2 * SPAN))],
        out_specs=[out_half] * 4,
        out_shape=[jax.ShapeDtypeStruct((batch, seq_len, LANES), F32)] * 4,
        scratch_shapes=([pltpu.VMEM((dil, seq, GROUP_WIDTH), BF16)] * 3 + [pltpu.VMEM((dil, seq, GROUP_WIDTH), F32)] * 2
                        + [pltpu.VMEM((seq_len, LANES), F32)]),
        compiler_params=_params(1),
        name="band_attn_d%d" % dil,
    )(view(q), view(q), view(k), view(k), view(v), view(v), bias)
    flat = lambda a: a.reshape(batch * seq_len, LANES)
    return (flat(o_lo), flat(o_hi)), (flat(l_lo), flat(l_hi))


def _cache_attn_kernel(*refs, n_new):
    n_g = len(GROUPS)
    q_refs, kn_refs, vn_refs = refs[0:n_g], refs[n_g:2 * n_g], refs[2 * n_g:3 * n_g]
    knt_refs, vnt_refs = refs[3 * n_g:4 * n_g], refs[4 * n_g:5 * n_g]
    kc_refs, vc_refs = refs[5 * n_g:6 * n_g], refs[6 * n_g:7 * n_g]
    bc_refs, bn_refs = refs[7 * n_g:8 * n_g], refs[8 * n_g:9 * n_g]
    outs = refs[9 * n_g:]
    ko_refs, vo_refs = outs[0:n_g], outs[n_g:2 * n_g]
    o_refs, l_refs = outs[2 * n_g:4 * n_g], outs[4 * n_g:6 * n_g]
    masks = _head_masks()
    pad = jnp.zeros((SPAN - n_new, GROUP_WIDTH), F32)
    lane = lax.broadcasted_iota(jnp.int32, (1, LANES), 1)
    j = pl.program_id(0) % (LANES // n_new)
    to_tail = LANES - n_new - n_new * j

    for g in range(n_g):
        kc, vc = kc_refs[g][0], vc_refs[g][0]
        kn, vn = kn_refs[g][0], vn_refs[g][0]
        cache_len = kc.shape[1]
        for src, new_t, dst in ((kc, knt_refs[g][0], ko_refs[g]), (vc, vnt_refs[g][0], vo_refs[g])):
            shifted = pltpu.roll(src, cache_len - n_new, axis=1)
            new_tail = pltpu.roll(new_t, to_tail, axis=1)
            if cache_len > LANES:
                dst[0, :, 0:cache_len - LANES] = shifted[:, 0:cache_len - LANES]
            dst[0, :, cache_len - LANES:cache_len] = jnp.where(lane >= LANES - n_new, new_tail,
                                                              shifted[:, cache_len - LANES:cache_len])
        q = q_refs[g][0]
        qm = jnp.concatenate([jnp.where(masks[h], q, 0.0) for h in range(HEADS_PER_GROUP)], axis=0).astype(BF16)
        kn_p = jnp.concatenate([kn, pad], axis=0).astype(BF16)
        vn_p = jnp.concatenate([vn, pad], axis=0).astype(BF16)
        s_c = _dot(qm, kc.astype(BF16)) + bc_refs[g][...]
        s_n = _dot_nt(qm, kn_p) + bn_refs[g][...]
        m = jnp.maximum(jnp.max(s_c, axis=-1, keepdims=True), jnp.max(s_n, axis=-1, keepdims=True))
        p_c = jnp.exp(s_c - m)
        p_n = jnp.exp(s_n - m)
        den = jnp.sum(p_c, axis=-1, keepdims=True) + jnp.sum(p_n, axis=-1, keepdims=True)
        acc = _dot_nt(p_c.astype(BF16), vc.astype(BF16)) + _dot(p_n.astype(BF16), vn_p)
        acc = acc * (1.0 / den)
        lse_rows = m + jnp.log(den)
        o = jnp.zeros((n_new, GROUP_WIDTH), F32)
        lse = jnp.zeros((n_new, GROUP_WIDTH), F32)
        for h in range(HEADS_PER_GROUP):
            rows = slice(h * n_new, (h + 1) * n_new)
            o = jnp.where(masks[h], acc[rows], o)
            lse = jnp.where(masks[h], lse_rows[rows], lse)
        for hf in (0, 1):
            o_refs[2 * g + hf][0] = o[:, hf * LANES:(hf + 1) * LANES]
            l_refs[2 * g + hf][0] = lse[:, hf * LANES:(hf + 1) * LANES]


def _cache_attn(q, k_new, v_new, k_new_t, v_new_t, k_cache, v_cache, tap_biases, *, batch, n_new):
    n_g = len(GROUPS)
    per_tile = LANES // n_new
    new3 = lambda a: a.reshape(batch, n_new, GROUP_WIDTH)
    new_spec = pl.BlockSpec((1, n_new, GROUP_WIDTH), lambda b: (b, 0, 0))
    new_t_spec = pl.BlockSpec((1, GROUP_WIDTH, LANES), lambda b: (0, 0, b // per_tile))
    cache_specs = [pl.BlockSpec((1, GROUP_WIDTH, c.shape[2]), lambda b: (b, 0, 0)) for c in k_cache]
    biases_c, biases_n = [], []
    for g, (_, dil) in enumerate(GROUPS):
        bc, bn = _cache_bias(tap_biases[g], dil, k_cache[g].shape[2], n_new)
        biases_c.append(bc)
        biases_n.append(bn)
    args = ([new3(a) for a in q] + [new3(a) for a in k_new] + [new3(a) for a in v_new]
            + list(k_new_t) + list(v_new_t) + list(k_cache) + list(v_cache) + biases_c + biases_n)
    in_specs = ([new_spec] * (3 * n_g) + [new_t_spec] * (2 * n_g) + cache_specs * 2
                + [_const_spec(b.shape) for b in biases_c] + [_const_spec(b.shape) for b in biases_n])
    half_spec = pl.BlockSpec((1, n_new, LANES), lambda b: (b, 0, 0))
    out_shape = ([jax.ShapeDtypeStruct(c.shape, F32) for c in k_cache] * 2
                 + [jax.ShapeDtypeStruct((batch, n_new, LANES), F32)] * (4 * n_g))
    out_specs = cache_specs * 2 + [half_spec] * (4 * n_g)
    outs = pl.pallas_call(
        functools.partial(_cache_attn_kernel, n_new=n_new),
        grid=(batch,),
        in_specs=in_specs,
        out_specs=out_specs,
        out_shape=out_shape,
        compiler_params=_params(1),
        name="cache_attn",
    )(*args)
    flat = lambda a: a.reshape(batch * n_new, LANES)
    halves = lambda lst: [(flat(lst[2 * g]), flat(lst[2 * g + 1])) for g in range(n_g)]
    return outs[0:n_g], outs[n_g:2 * n_g], halves(outs[2 * n_g:4 * n_g]), halves(outs[4 * n_g:6 * n_g])


def _conv_taps(ext_ref, w_ref, first_row, n_rows):
    acc = jnp.zeros((n_rows, CONV_DIM), F32)
    for j in range(CONV_WIDTH):
        acc = acc + ext_ref[pl.ds(first_row + j, n_rows), :] * w_ref[j:j + 1, :]
    return acc


def _conv_taps_aligned(ext_ref, w_ref, first_row, n_rows):
    acc = None
    for b in range(SUBLANES):
        part = None
        for a in range((first_row + CONV_WIDTH - 1 - b) // SUBLANES + 1):
            j = SUBLANES * a + b - first_row
            if j < 0:
                continue
            term = ext_ref[pl.ds(SUBLANES * a, n_rows + SUBLANES), :] * w_ref[j:j + 1, :]
            part = term if part is None else part + term
        part = part[b:b + n_rows]
        acc = part if acc is None else acc + part
    return acc


def _conv_prompt_kernel(u_ref, halo_ref, w_ref, y_ref, ext_ref, *, tc):
    i = pl.program_id(1)
    ext_ref[0:CONV_HALO, :] = jnp.where(i > 0, halo_ref[0], 0.0)
    ext_ref[CONV_HALO:CONV_HALO + tc, :] = u_ref[0]
    ext_ref[CONV_HALO + tc:CONV_HALO + tc + SUBLANES, :] = jnp.zeros((SUBLANES, CONV_DIM), F32)
    y_ref[0] = _conv_taps_aligned(ext_ref, w_ref, CONV_HALO - (CONV_WIDTH - 1), tc)


def _conv_prompt(u, w, *, batch, seq_len, tc):
    u3 = u.reshape(batch, seq_len, CONV_DIM)
    halo_per_tile = tc // CONV_HALO
    y = pl.pallas_call(
        functools.partial(_conv_prompt_kernel, tc=tc),
        grid=(batch, seq_len // tc),
        in_specs=[pl.BlockSpec((1, tc, CONV_DIM), lambda b, i: (b, i, 0)),
                  pl.BlockSpec((1, CONV_HALO, CONV_DIM),
                               lambda b, i: (b, jnp.maximum(i * halo_per_tile - 1, 0), 0)),
                  pl.BlockSpec((CONV_WIDTH, CONV_DIM), lambda b, i: (0, 0))],
        out_specs=pl.BlockSpec((1, tc, CONV_DIM), lambda b, i: (b, i, 0)),
        out_shape=jax.ShapeDtypeStruct((batch, seq_len, CONV_DIM), F32),
        scratch_shapes=[pltpu.VMEM((CONV_HALO + tc + SUBLANES, CONV_DIM), F32)],
        compiler_params=_params(2),
        name="conv_prompt",
    )(u3, u3, w)
    return y.reshape(batch * seq_len, CONV_DIM)


def _conv_sample_kernel(u_ref, st_ref, w_ref, y_ref, so_ref, ext_ref, *, n_new):
    hist = CONV_WIDTH - 1
    ext_ref[0:hist, :] = st_ref[0]
    ext_ref[hist:hist + n_new, :] = u_ref[0]
    y_ref[0] = _conv_taps(ext_ref, w_ref, 0, n_new)
    so_ref[0] = ext_ref[n_new:n_new + hist, :]


def _conv_sample(u, state, w, *, batch, n_new):
    hist = CONV_WIDTH - 1
    u3 = u.reshape(batch, n_new, CONV_DIM)
    y, st = pl.pallas_call(
        functools.partial(_conv_sample_kernel, n_new=n_new),
        grid=(batch,),
        in_specs=[pl.BlockSpec((1, n_new, CONV_DIM), lambda b: (b, 0, 0)),
                  pl.BlockSpec((1, hist, CONV_DIM), lambda b: (b, 0, 0)),
                  pl.BlockSpec((CONV_WIDTH, CONV_DIM), lambda b: (0, 0))],
        out_specs=[pl.BlockSpec((1, n_new, CONV_DIM), lambda b: (b, 0, 0)),
                   pl.BlockSpec((1, hist, CONV_DIM), lambda b: (b, 0, 0))],
        out_shape=[jax.ShapeDtypeStruct((batch, n_new, CONV_DIM), F32),
                   jax.ShapeDtypeStruct((batch, hist, CONV_DIM), F32)],
        scratch_shapes=[pltpu.VMEM((hist + n_new + 2, CONV_DIM), F32)],
        compiler_params=_params(1),
        name="conv_sample",
    )(u3, state, w)
    return y.reshape(batch * n_new, CONV_DIM), st


def _mix_kernel(*refs):
    n_half = 2 * len(GROUPS)
    o_refs, l_refs = refs[0:n_half], refs[n_half:2 * n_half]
    (yc_ref, ga_ref, gb_ref, x_ref, wap_ref, cb_ref, lg_ref, lb_ref, wcp_ref, wo_ref, xg_ref, wxq_ref,
     x_out, q_out) = refs[2 * n_half:]
    merged_halves = []
    for hf in (0, 1):
        lses = [l_refs[2 * g + hf][...] for g in range(len(GROUPS))]
        m = functools.reduce(jnp.maximum, lses)
        es = [jnp.exp(l - m) for l in lses]
        num = sum(e * o_refs[2 * g + hf][...] for g, e in enumerate(es))
        merged_halves.append(num * (1.0 / sum(es)))
    attn = jnp.concatenate(merged_halves, axis=-1)
    a_branch = _dot(attn.astype(BF16), wap_ref[...])
    y = yc_ref[...] + cb_ref[...]
    mu = jnp.mean(y, axis=-1, keepdims=True)
    yc = y - mu
    var = jnp.mean(yc * yc, axis=-1, keepdims=True)
    y = yc * lax.rsqrt(var + EPS) * lg_ref[...] + lb_ref[...]
    y = y * jax.nn.sigmoid(y)
    c_branch = _dot(y.astype(BF16), wcp_ref[...])
    merged = ga_ref[...] * a_branch + gb_ref[...] * c_branch
    x = x_ref[...] + _dot(merged.astype(BF16), wo_ref[...])
    x_out[...] = x
    hq = _rms(x, xg_ref[...]).astype(BF16)
    q_out[...] = (_dot(hq, wxq_ref[...]) * (X_HEAD_DIM ** -0.5)).astype(BF16)


def _mix(o, lse, yconv, ga, gb, x, w, *, tm):
    rows = x.shape[0]
    half = _rows_spec(tm, LANES)
    wide = _rows_spec(tm, D_MODEL)
    o = [a for pair in o for a in pair]
    lse = [a for pair in lse for a in pair]
    return pl.pallas_call(
        _mix_kernel,
        grid=(rows // tm,),
        in_specs=[half] * 12 + [_rows_spec(tm, CONV_DIM), wide, wide, wide,
                              _const_spec((GROUP_WIDTH, D_MODEL)), _const_spec((1, CONV_DIM)),
                              _const_spec((1, CONV_DIM)), _const_spec((1, CONV_DIM)),
                              _const_spec((CONV_DIM, D_MODEL)), _const_spec((D_MODEL, D_MODEL)),
                              _const_spec((1, D_MODEL)), _const_spec((D_MODEL, D_MODEL))],
        out_specs=[wide, wide],
        out_shape=[jax.ShapeDtypeStruct((rows, D_MODEL), F32), jax.ShapeDtypeStruct((rows, D_MODEL), BF16)],
        compiler_params=_params(1),
        name="mix",
    )(*o, *lse, yconv, ga, gb, x, w['w_attn_proj'], w['conv_dw_b'], w['conv_ln_g'], w['conv_ln_b'],
      w['w_conv_proj'], w['w_o'], w['xattn_norm'], w['w_xq'])


def _memkv_kernel(m_ref, g_ref, w_ref, k5_ref, v5_ref, k_ref, v_ref):
    h = _rms(m_ref[...], g_ref[...]).astype(BF16)
    for hd in range(X_HEADS):
        lo = hd * X_HEAD_DIM
        k = _dot(h, w_ref[:, lo:lo + X_HEAD_DIM])
        v = _dot(h, w_ref[:, D_MODEL + lo:D_MODEL + lo + X_HEAD_DIM])
        k5_ref[0, 0, :, hd, :] = k
        v5_ref[0, 0, :, hd, :] = v
        k_ref[:, lo:lo + X_HEAD_DIM] = k.astype(BF16)
        v_ref[:, lo:lo + X_HEAD_DIM] = v.astype(BF16)


def _memkv(mem, norm_g, w_xkv, *, batch):
    spec5 = pl.BlockSpec((1, 1, N_MEM, X_HEADS, X_HEAD_DIM), lambda b: (0, b, 0, 0, 0))
    rows = _rows_spec(N_MEM, D_MODEL)
    return pl.pallas_call(
        _memkv_kernel,
        grid=(batch,),
        in_specs=[rows, _const_spec((1, D_MODEL)), _const_spec((D_MODEL, 2 * D_MODEL))],
        out_specs=[spec5, spec5, rows, rows],
        out_shape=([jax.ShapeDtypeStruct((1, batch, N_MEM, X_HEADS, X_HEAD_DIM), F32)] * 2
                   + [jax.ShapeDtypeStruct((batch * N_MEM, D_MODEL), BF16)] * 2),
        compiler_params=_params(1),
        name="memory_kv",
    )(mem, norm_g, w_xkv)


def _xattn_kernel(q_ref, k_ref, v_ref, o_ref):
    for h in range(X_HEADS):
        lanes = slice(h * X_HEAD_DIM, (h + 1) * X_HEAD_DIM)
        s = _dot_nt(q_ref[0, :, lanes], k_ref[:, lanes])
        m = jnp.max(s, axis=-1, keepdims=True)
        p = jnp.exp(s - m)
        den = jnp.sum(p, axis=-1, keepdims=True)
        acc = _dot(p.astype(BF16), v_ref[:, lanes])
        o_ref[0, :, lanes] = (acc * (1.0 / den)).astype(BF16)


def _xattn(q, mk, mv, *, batch, tq):
    rpb = q.shape[1]
    q_spec = pl.BlockSpec((1, tq, D_MODEL), lambda b, i: (b, i, 0))
    m_spec = pl.BlockSpec((N_MEM, D_MODEL), lambda b, i: (b, 0))
    return pl.pallas_call(
        _xattn_kernel,
        grid=(batch, rpb // tq),
        in_specs=[q_spec, m_spec, m_spec],
        out_specs=q_spec,
        out_shape=jax.ShapeDtypeStruct((batch, rpb, D_MODEL), BF16),
        compiler_params=_params(2),
        name="cross_attn",
    )(q, mk, mv)


def _xattn_cached_kernel(q_ref, k_lo, k_hi, v_lo, v_hi, mask_ref, o_ref, *, n_q):
    flat = lambda ref: ref[0, 0].reshape(N_MEM * X_HEADS, LANES)
    k_all = jnp.concatenate([flat(k_lo), flat(k_hi)], axis=1).astype(BF16)
    v_all = jnp.concatenate([flat(v_lo), flat(v_hi)], axis=1).astype(BF16)
    q = q_ref[0].astype(F32)
    qs = jnp.concatenate([q[:, h * X_HEAD_DIM:(h + 1) * X_HEAD_DIM] for h in range(X_HEADS)], axis=0).astype(BF16)
    s = _dot_nt(qs, k_all) + mask_ref[...]
    m = jnp.max(s, axis=-1, keepdims=True)
    p = jnp.exp(s - m)
    den = jnp.sum(p, axis=-1, keepdims=True)
    acc = _dot(p.astype(BF16), v_all) * (1.0 / den)
    o_ref[0] = jnp.concatenate([acc[h * n_q:(h + 1) * n_q] for h in range(X_HEADS)], axis=1).astype(BF16)


def _xattn_cached(q, mk, mv, *, batch):
    n_q = q.shape[1]
    q_spec = pl.BlockSpec((1, n_q, D_MODEL), lambda b: (b, 0, 0))
    halves = [pl.BlockSpec((1, 1, N_MEM, X_HEADS, LANES), lambda b, hf=hf: (0, b, 0, 0, hf))
              for hf in range(X_HEAD_DIM // LANES)]
    own_head = (np.arange(N_MEM * X_HEADS)[None, :] % X_HEADS) == (np.arange(X_HEADS * n_q)[:, None] // n_q)
    mask = jnp.asarray(np.where(own_head, 0.0, NEG_INF), F32)
    return pl.pallas_call(
        functools.partial(_xattn_cached_kernel, n_q=n_q),
        grid=(batch,),
        in_specs=[q_spec] + halves * 2 + [_const_spec(mask.shape)],
        out_specs=q_spec,
        out_shape=jax.ShapeDtypeStruct((batch, n_q, D_MODEL), BF16),
        compiler_params=_params(1),
        name="cross_attn_cached",
    )(q, mk, mk, mv, mv, mask)


def _layer_tail(x1, o, lse, yconv, ga, gb, xattn_fn, w, final_g, *, tm):
    rows = x1.shape[0]
    x2, xq = _mix(o, lse, yconv, ga, gb, x1, w, tm=tm)
    xo = xattn_fn(xq)
    return _ffn(x2, w['ffn2_norm'], w['ffn2_w_gate'], w['ffn2_w_up'], w['ffn2_w_down'], tm=tm,
                pre=(xo.reshape(rows, D_MODEL), w['w_xo']), final_g=final_g)


def _to_positions_heads(a_t, batch, n_pos):
    return jnp.transpose(a_t.reshape(batch, HEADS_PER_GROUP, HEAD_DIM, n_pos), (0, 3, 1, 2))[None]


def _to_feature_major(a, batch, n_pos):
    return jnp.transpose(a[0], (0, 2, 3, 1)).reshape(batch, GROUP_WIDTH, n_pos)


def kernel(x_prompt, x_sample, mem_prompt, cache_win0_k, cache_win0_v, cache_win1_k, cache_win1_v, cache_win2_k, cache_win2_v, state_conv, cache_mem_k, cache_mem_v, rel_bias, ffn1_norm, ffn1_w_gate, ffn1_w_up, ffn1_w_down, mix_norm, w_in, w_attn_proj, conv_dw_w, conv_dw_b, conv_ln_g, conv_ln_b, w_conv_proj, w_o, xattn_norm, mem_norm, w_xq, w_xkv, w_xo, ffn2_norm, ffn2_w_gate, ffn2_w_up, ffn2_w_down, final_norm):
    batch, seq_len, _ = x_prompt.shape
    dec_batch, dec_seq, _ = x_sample.shape
    n_g = len(GROUPS)
    assert ffn1_norm.shape[0] == 1, "single layer"
    mat = lambda a: a[0].astype(BF16)
    vec = lambda a: a[0].reshape(1, -1)
    w = dict(ffn1_norm=vec(ffn1_norm), ffn1_w_gate=mat(ffn1_w_gate), ffn1_w_up=mat(ffn1_w_up),
             ffn1_w_down=mat(ffn1_w_down), mix_norm=vec(mix_norm), w_in=mat(w_in),
             w_attn_proj=mat(w_attn_proj), conv_dw_w=conv_dw_w[0], conv_dw_b=vec(conv_dw_b),
             conv_ln_g=vec(conv_ln_g), conv_ln_b=vec(conv_ln_b), w_conv_proj=mat(w_conv_proj),
             w_o=mat(w_o), xattn_norm=vec(xattn_norm), mem_norm=vec(mem_norm), w_xq=mat(w_xq),
             w_xkv=mat(w_xkv), w_xo=mat(w_xo), ffn2_norm=vec(ffn2_norm), ffn2_w_gate=mat(ffn2_w_gate),
             ffn2_w_up=mat(ffn2_w_up), ffn2_w_down=mat(ffn2_w_down))
    final_g = final_norm.reshape(1, -1)
    tap_biases = [_tap_bias(rel_bias, g, dil) for g, (_, dil) in enumerate(GROUPS)]

    def head(x, tm, rows_per_seq, t_groups):
        x1 = _ffn(x, w['ffn1_norm'], w['ffn1_w_gate'], w['ffn1_w_up'], w['ffn1_w_down'], tm=tm)
        outs = _win(x1, w['mix_norm'], w['w_in'], tm=tm, rows_per_seq=rows_per_seq, t_groups=t_groups)
        return x1, outs[0:3], outs[3:6], outs[6:9], outs[9], outs[10], outs[11], outs[12:]

    rows_p = batch * seq_len
    tm_p = 512
    full_groups = [g for g, (window, _) in enumerate(GROUPS) if window >= seq_len]
    x1, q, k, v, u, ga, gb, kv_t = head(x_prompt.reshape(rows_p, D_MODEL), tm_p, seq_len, full_groups)
    o, lse = [], []
    for g, (_, dil) in enumerate(GROUPS):
        o_g, l_g = _band_attn(q[g], k[g], v[g], _band_bias(tap_biases[g]), batch=batch, seq_len=seq_len, dil=dil)
        o.append(o_g)
        lse.append(l_g)
    yconv = _conv_prompt(u, w['conv_dw_w'], batch=batch, seq_len=seq_len, tc=256)
    p_mem_k, p_mem_v, mk_p, mv_p = _memkv(mem_prompt.reshape(batch * N_MEM, D_MODEL), w['mem_norm'], w['w_xkv'],
                                          batch=batch)
    y_prompt = _layer_tail(
        x1, o, lse, yconv, ga, gb,
        lambda xq: _xattn(xq.reshape(batch, seq_len, D_MODEL), mk_p, mv_p, batch=batch, tq=512),
        w, final_g, tm=tm_p)

    p_win = []
    for g, (window, _) in enumerate(GROUPS):
        keep = min(window, seq_len)
        if g in full_groups:
            j = full_groups.index(g)
            p_win += [_to_positions_heads(kv_t[2 * j], batch, seq_len), _to_positions_heads(kv_t[2 * j + 1], batch, seq_len)]
        else:
            for a in (k[g], v[g]):
                p_win.append(a.reshape(batch, seq_len, GROUP_WIDTH)[:, seq_len - keep:]
                             .reshape(1, batch, keep, HEADS_PER_GROUP, HEAD_DIM))
    p_conv = u.reshape(batch, seq_len, CONV_DIM)[:, seq_len - (CONV_WIDTH - 1):][None]

    rows_s = dec_batch * dec_seq
    x1, q, k, v, u, ga, gb, kv_t = head(x_sample.reshape(rows_s, D_MODEL), rows_s, rows_s, list(range(n_g)))
    caches_k = [_to_feature_major(c, dec_batch, c.shape[2]) for c in (cache_win0_k, cache_win1_k, cache_win2_k)]
    caches_v = [_to_feature_major(c, dec_batch, c.shape[2]) for c in (cache_win0_v, cache_win1_v, cache_win2_v)]
    new_k, new_v, o, lse = _cache_attn(q, k, v, kv_t[0::2], kv_t[1::2], caches_k, caches_v, tap_biases,
                                       batch=dec_batch, n_new=dec_seq)
    yconv, s_conv = _conv_sample(u, state_conv[0], w['conv_dw_w'], batch=dec_batch, n_new=dec_seq)
    y_sample = _layer_tail(
        x1, o, lse, yconv, ga, gb,
        lambda xq: _xattn_cached(xq.reshape(dec_batch, dec_seq, D_MODEL), cache_mem_k, cache_mem_v, batch=dec_batch),
        w, final_g, tm=rows_s)

    s_win = []
    for g in range(n_g):
        for a in (new_k[g], new_v[g]):
            s_win.append(_to_positions_heads(a, dec_batch, a.shape[2]))

    return (y_prompt.reshape(batch, seq_len, D_MODEL), y_sample.reshape(dec_batch, dec_seq, D_MODEL),
            *p_win, p_conv, p_mem_k, p_mem_v, *s_win, s_conv[None])
```

```python
import functools
import math

import jax
import jax.numpy as jnp
import numpy as np
from jax import lax
from jax.experimental import pallas as pl
from jax.experimental.pallas import tpu as pltpu

D_MODEL = 1024
HEAD_DIM = 64
GROUPS = ((128, 1), (512, 4), (2048, 16))
HEADS_PER_GROUP = 4
GROUP_WIDTH = HEADS_PER_GROUP * HEAD_DIM
ATTN_WIDTH = len(GROUPS) * GROUP_WIDTH
CONV_DIM = D_MODEL // 2
CONV_WIDTH = 31
CONV_HALO = 32
N_BUCKETS = 32
MAX_DISTANCE = 2048
D_FF = ((8 * D_MODEL // 3 + 127) // 128) * 128
N_MEM = 256
X_HEADS = 4
X_HEAD_DIM = D_MODEL // X_HEADS
EPS = 1e-6
NEG_INF = -1e30
SPAN = 128
LANES = 128
SUBLANES = 8

V7X_VMEM_LIMIT_BYTES = 56 * 1024 * 1024
BF16 = jnp.bfloat16
F32 = jnp.float32


def _params(n_axes):
    return pltpu.CompilerParams(dimension_semantics=("parallel",) * n_axes,
                                vmem_limit_bytes=V7X_VMEM_LIMIT_BYTES)


def _const_spec(shape):
    return pl.BlockSpec(shape, lambda *_: (0,) * len(shape), pipeline_mode=pl.Buffered(1))


def _rows_spec(tm, width):
    return pl.BlockSpec((tm, width), lambda i: (i, 0))


def _rms(x, g):
    return x * lax.rsqrt(jnp.mean(x * x, axis=-1, keepdims=True) + EPS) * g


def _dot(a, b):
    return jnp.dot(a, b, preferred_element_type=F32)


def _dot_nt(a, b):
    return lax.dot_general(a, b, (((1,), (1,)), ((), ())), preferred_element_type=F32)


def _head_masks(width=GROUP_WIDTH):
    lane = lax.broadcasted_iota(jnp.int32, (1, width), 1)
    return [(lane >= h * HEAD_DIM) & (lane < (h + 1) * HEAD_DIM) for h in range(HEADS_PER_GROUP)]


def _ffn_kernel(*refs, pre_proj, final_norm):
    refs = list(refs)
    x_ref = refs.pop(0)
    if pre_proj:
        a_ref, wp_ref = refs.pop(0), refs.pop(0)
    g_ref, wg_ref, wu_ref, wd_ref = refs.pop(0), refs.pop(0), refs.pop(0), refs.pop(0)
    if final_norm:
        fg_ref = refs.pop(0)
    o_ref = refs.pop(0)

    x = x_ref[...]
    if pre_proj:
        x = x + _dot(a_ref[...], wp_ref[...])
    h = _rms(x, g_ref[...]).astype(BF16)
    gate = _dot(h, wg_ref[...])
    up = _dot(h, wu_ref[...])
    act = (gate * jax.nn.sigmoid(gate) * up).astype(BF16)
    x = x + 0.5 * _dot(act, wd_ref[...])
    if final_norm:
        x = _rms(x, fg_ref[...])
    o_ref[...] = x


def _ffn(x, norm_g, wg, wu, wd, *, tm, pre=None, final_g=None):
    rows = x.shape[0]
    args = [x]
    specs = [_rows_spec(tm, D_MODEL)]
    if pre is not None:
        a, wp = pre
        args += [a, wp]
        specs += [_rows_spec(tm, D_MODEL), _const_spec((D_MODEL, D_MODEL))]
    args += [norm_g, wg, wu, wd]
    specs += [_const_spec((1, D_MODEL)), _const_spec((D_MODEL, D_FF)), _const_spec((D_MODEL, D_FF)),
              _const_spec((D_FF, D_MODEL))]
    if final_g is not None:
        args.append(final_g)
        specs.append(_const_spec((1, D_MODEL)))
    return pl.pallas_call(
        functools.partial(_ffn_kernel, pre_proj=pre is not None, final_norm=final_g is not None),
        grid=(rows // tm,),
        in_specs=specs,
        out_specs=_rows_spec(tm, D_MODEL),
        out_shape=jax.ShapeDtypeStruct((rows, D_MODEL), F32),
        compiler_params=_params(1),
        name="ffn",
    )(*args)


_Q_END = ATTN_WIDTH
_K_END = 2 * ATTN_WIDTH
_V_END = 3 * ATTN_WIDTH
_UA_END = _V_END + CONV_DIM
_UB_END = _UA_END + CONV_DIM
_GA_END = _UB_END + D_MODEL
IN_WIDTH = _GA_END + D_MODEL


def _win_kernel(*refs, t_groups, fuse_conv, tm, tiles_per_seq):
    x_ref, g_ref, w_ref = refs[:3]
    cw_ref = refs[3] if fuse_conv else None
    outs = refs[4:] if fuse_conv else refs[3:]
    q0, q1, q2, k0, k1, k2, v0, v1, v2, ga_ref, gb_ref = outs[:11]
    n_t = 2 * len(t_groups)
    t_refs = outs[11:11 + n_t]
    h = _rms(x_ref[...], g_ref[...]).astype(BF16)

    def seg(lo, hi):
        return _dot(h, w_ref[:, lo:hi])

    u = seg(_V_END, _UA_END) * jax.nn.sigmoid(seg(_UA_END, _UB_END))
    if fuse_conv:
        yc_ref, tail_ref, ext_ref = outs[11 + n_t:]
        first_tile = pl.program_id(0) % tiles_per_seq == 0

        @pl.when(first_tile)
        def _():
            ext_ref[0:CONV_HALO, :] = jnp.zeros((CONV_HALO, CONV_DIM), F32)

        @pl.when(jnp.logical_not(first_tile))
        def _():
            ext_ref[0:CONV_HALO, :] = ext_ref[tm:tm + CONV_HALO, :]

        ext_ref[CONV_HALO:CONV_HALO + tm, :] = u
        ext_ref[CONV_HALO + tm:CONV_HALO + tm + SUBLANES, :] = jnp.zeros((SUBLANES, CONV_DIM), F32)
        _conv_taps_aligned(ext_ref, cw_ref, yc_ref, CONV_HALO - (CONV_WIDTH - 1), tm)
        tail_ref[0] = u[tm - CONV_HALO:tm]
    else:
        outs[11 + n_t][...] = u
    for g, (q_ref, k_ref, v_ref) in enumerate(((q0, k0, v0), (q1, k1, v1), (q2, k2, v2))):
        lo = g * GROUP_WIDTH
        q_ref[...] = seg(lo, lo + GROUP_WIDTH) * (HEAD_DIM ** -0.5)
        k = seg(_Q_END + lo, _Q_END + lo + GROUP_WIDTH)
        v = seg(_K_END + lo, _K_END + lo + GROUP_WIDTH)
        k_ref[...] = k
        v_ref[...] = v
        if g in t_groups:
            j = t_groups.index(g)
            t_refs[2 * j][0] = k.T
            t_refs[2 * j + 1][0] = v.T
    ga_ref[...] = jax.nn.sigmoid(seg(_UB_END, _GA_END)).astype(BF16)
    gb_ref[...] = jax.nn.sigmoid(seg(_GA_END, IN_WIDTH)).astype(BF16)


def _win(x, norm_g, w_in, *, tm, rows_per_seq, t_groups, conv_w=None):
    rows = x.shape[0]
    n_seq = rows // rows_per_seq
    n_t = 2 * len(t_groups)
    tiles_per_seq = rows_per_seq // tm
    fuse_conv = conv_w is not None
    grp = jax.ShapeDtypeStruct((rows, GROUP_WIDTH), F32)
    out_shape = ([grp] * 9 + [jax.ShapeDtypeStruct((rows, D_MODEL), BF16)] * 2
                 + [jax.ShapeDtypeStruct((n_seq, GROUP_WIDTH, rows_per_seq), F32)] * n_t
                 + [jax.ShapeDtypeStruct((rows, CONV_DIM), F32)])
    t_spec = pl.BlockSpec((1, GROUP_WIDTH, tm), lambda i: (i // tiles_per_seq, 0, i % tiles_per_seq))
    out_specs = ([_rows_spec(tm, GROUP_WIDTH)] * 9 + [_rows_spec(tm, D_MODEL)] * 2 + [t_spec] * n_t
                 + [_rows_spec(tm, CONV_DIM)])
    in_specs = [_rows_spec(tm, D_MODEL), _const_spec((1, D_MODEL)), _const_spec((D_MODEL, IN_WIDTH))]
    args = [x, norm_g, w_in]
    scratch = []
    if fuse_conv:
        in_specs.append(_const_spec((CONV_WIDTH, CONV_DIM)))
        args.append(conv_w)
        out_shape.append(jax.ShapeDtypeStruct((n_seq, CONV_HALO, CONV_DIM), F32))
        out_specs.append(pl.BlockSpec((1, CONV_HALO, CONV_DIM), lambda i: (i // tiles_per_seq, 0, 0)))
        scratch.append(pltpu.VMEM((CONV_HALO + tm + SUBLANES, CONV_DIM), F32))
    return pl.pallas_call(
        functools.partial(_win_kernel, t_groups=tuple(t_groups), fuse_conv=fuse_conv, tm=tm,
                          tiles_per_seq=tiles_per_seq),
        grid=(rows // tm,),
        in_specs=in_specs,
        out_specs=out_specs,
        out_shape=out_shape,
        scratch_shapes=scratch,
        compiler_params=pltpu.CompilerParams(dimension_semantics=("arbitrary",),
                                             vmem_limit_bytes=V7X_VMEM_LIMIT_BYTES),
        name="w_in",
    )(*args)


def _rel_bucket(dist):
    n = jnp.maximum(dist, 0)
    max_exact = N_BUCKETS // 2
    nf = jnp.maximum(n, 1).astype(F32)
    large = max_exact + (jnp.log(nf / max_exact) / math.log(MAX_DISTANCE / max_exact)
                         * (N_BUCKETS - max_exact)).astype(jnp.int32)
    return jnp.where(n < max_exact, n, jnp.minimum(large, N_BUCKETS - 1))


def _tap_bias(rel_bias, g, dil):
    bias_g = rel_bias[:, g * HEADS_PER_GROUP:(g + 1) * HEADS_PER_GROUP]
    return bias_g[_rel_bucket(jnp.arange(SPAN, -1, -1) * dil)].astype(F32).T


def _band_bias(tap_bias):
    period = 3 * SPAN
    vec = jnp.concatenate([tap_bias, jnp.full((HEADS_PER_GROUP, period - (SPAN + 1)), NEG_INF, F32)], axis=1)
    flat = jnp.tile(vec, (1, SPAN))[:, :SPAN * (period - 1)]
    return flat.reshape(HEADS_PER_GROUP, SPAN, period - 1)[:, :, :2 * SPAN].reshape(HEADS_PER_GROUP * SPAN, 2 * SPAN)


def _cache_bias(tap_bias, dil, cache_len, n_new):
    rev = lax.pad(tap_bias, jnp.float32(NEG_INF), [(0, 0, 0), (cache_len + n_new - 1 - SPAN * dil, 0, dil - 1)])
    bias_c = jnp.stack([rev[:, n_new - 1 - t:n_new - 1 - t + cache_len] for t in range(n_new)], axis=1)
    near = jnp.concatenate([rev[:, cache_len:], jnp.full((HEADS_PER_GROUP, LANES), NEG_INF, F32)], axis=1)
    bias_n = jnp.stack([near[:, n_new - 1 - t:n_new - 1 - t + LANES] for t in range(n_new)], axis=1)
    return (bias_c.reshape(HEADS_PER_GROUP * n_new, cache_len), bias_n.reshape(HEADS_PER_GROUP * n_new, LANES))


MAX_ROW_STRIDE = 4


def _gather_residues(src, store, tmp_ref, dil, seq):
    if dil == 1:
        store(0, src[...])
    elif dil <= MAX_ROW_STRIDE:
        for r in range(dil):
            store(r, src[pl.ds(r, seq, stride=dil), :])
    else:
        inner = dil // MAX_ROW_STRIDE
        assert inner <= MAX_ROW_STRIDE and inner * MAX_ROW_STRIDE == dil
        n_part = seq * inner
        for c in range(MAX_ROW_STRIDE):
            tmp_ref[c * n_part:(c + 1) * n_part, :] = src[pl.ds(c, n_part, stride=MAX_ROW_STRIDE), :]
        for c in range(MAX_ROW_STRIDE):
            for c2 in range(inner):
                store(c + MAX_ROW_STRIDE * c2, tmp_ref[pl.ds(c * n_part + c2, seq, stride=inner), :])


def _scatter_residues(load, dst, tmp_ref, dil, seq):
    if dil == 1:
        dst[...] = load(0)
    elif dil <= MAX_ROW_STRIDE:
        for r in range(dil):
            dst[pl.ds(r, seq, stride=dil), :] = load(r)
    else:
        inner = dil // MAX_ROW_STRIDE
        assert inner <= MAX_ROW_STRIDE and inner * MAX_ROW_STRIDE == dil
        n_part = seq * inner
        for c in range(MAX_ROW_STRIDE):
            for c2 in range(inner):
                tmp_ref[pl.ds(c * n_part + c2, seq, stride=inner), :] = load(c + MAX_ROW_STRIDE * c2)
        for c in range(MAX_ROW_STRIDE):
            dst[pl.ds(c, n_part, stride=MAX_ROW_STRIDE), :] = tmp_ref[c * n_part:(c + 1) * n_part, :]


def _band_attn_kernel(q_lo, q_hi, k_lo, k_hi, v_lo, v_hi, b_ref, o_lo, o_hi, l_lo, l_hi,
                      qb_ref, kb_ref, vb_ref, ob_ref, lb_ref, tmp_ref, *, dil, seq):
    n_blk = seq // SPAN
    halves = (slice(0, LANES), slice(LANES, 2 * LANES))
    for dst, srcs in ((qb_ref, (q_lo, q_hi)), (kb_ref, (k_lo, k_hi)), (vb_ref, (v_lo, v_hi))):
        for lanes, src in zip(halves, srcs):
            def store(r, rows, dst=dst, lanes=lanes):
                dst[r, :, lanes] = rows.astype(BF16)
            _gather_residues(src, store, tmp_ref, dil, seq)
    masks = _head_masks()

    def block(q, kb, vb, key_lo):
        qm = jnp.concatenate([jnp.where(masks[h], q, jnp.zeros_like(q)) for h in range(HEADS_PER_GROUP)], axis=0)
        s = _dot_nt(qm, kb) + b_ref[:, key_lo:]
        m = jnp.max(s, axis=-1, keepdims=True)
        p = jnp.exp(s - m)
        den = jnp.sum(p, axis=-1, keepdims=True)
        acc = _dot(p.astype(BF16), vb) * (1.0 / den)
        lse_rows = m + jnp.log(den)
        o = acc[0:SPAN]
        lse = jnp.broadcast_to(lse_rows[0:SPAN], (SPAN, GROUP_WIDTH))
        for h in range(1, HEADS_PER_GROUP):
            o = jnp.where(masks[h], acc[h * SPAN:(h + 1) * SPAN], o)
            lse = jnp.where(masks[h], lse_rows[h * SPAN:(h + 1) * SPAN], lse)
        return o, lse

    def first_block(r):
        o, lse = block(qb_ref[r, 0:SPAN, :], kb_ref[r, 0:SPAN, :], vb_ref[r, 0:SPAN, :], SPAN)
        ob_ref[r, 0:SPAN, :] = o
        lb_ref[r, 0:SPAN, :] = lse

    def later_block(r, j):
        q_start = j * SPAN
        k_start = (j - 1) * SPAN
        if not isinstance(j, int):
            q_start = pl.multiple_of(q_start, SPAN)
            k_start = pl.multiple_of(k_start, SPAN)
        o, lse = block(qb_ref[r, pl.ds(q_start, SPAN), :], kb_ref[r, pl.ds(k_start, 2 * SPAN), :],
                       vb_ref[r, pl.ds(k_start, 2 * SPAN), :], 0)
        ob_ref[r, pl.ds(q_start, SPAN), :] = o
        lb_ref[r, pl.ds(q_start, SPAN), :] = lse

    for r in range(dil):
        first_block(r)
        for j in range(1, n_blk):
            later_block(r, j)

    for src, dsts in ((ob_ref, (o_lo, o_hi)), (lb_ref, (l_lo, l_hi))):
        for lanes, dst in zip(halves, dsts):
            _scatter_residues(lambda r, src=src, lanes=lanes: src[r, :, lanes], dst, tmp_ref, dil, seq)


def _band_attn(q, k, v, bias, *, batch, seq_len, dil):
    seq = seq_len // dil
    view = lambda a: a.reshape(batch, seq_len, GROUP_WIDTH)
    in_halves = [pl.BlockSpec((None, seq_len, LANES), lambda b, hf=hf: (b, 0, hf)) for hf in (0, 1)]
    out_half = pl.BlockSpec((None, seq_len, LANES), lambda b: (b, 0, 0))
    o_lo, o_hi, l_lo, l_hi = pl.pallas_call(
        functools.partial(_band_attn_kernel, dil=dil, seq=seq),
        grid=(batch,),
        in_specs=in_halves * 3 + [_const_spec((HEADS_PER_GROUP * SPAN, 2 * SPAN))],
        out_specs=[out_half] * 4,
        out_shape=[jax.ShapeDtypeStruct((batch, seq_len, LANES), F32)] * 4,
        scratch_shapes=([pltpu.VMEM((dil, seq, GROUP_WIDTH), BF16)] * 3 + [pltpu.VMEM((dil, seq, GROUP_WIDTH), F32)] * 2
                        + [pltpu.VMEM((seq_len, LANES), F32)]),
        compiler_params=_params(1),
        name="band_attn_d%d" % dil,
    )(view(q), view(q), view(k), view(k), view(v), view(v), bias)
    flat = lambda a: a.reshape(batch * seq_len, LANES)
    return (flat(o_lo), flat(o_hi)), (flat(l_lo), flat(l_hi))


def _cache_attn_kernel(*refs, n_new):
    n_g = len(GROUPS)
    q_refs, kn_refs, vn_refs = refs[0:n_g], refs[n_g:2 * n_g], refs[2 * n_g:3 * n_g]
    knt_refs, vnt_refs = refs[3 * n_g:4 * n_g], refs[4 * n_g:5 * n_g]
    kc_refs, vc_refs = refs[5 * n_g:6 * n_g], refs[6 * n_g:7 * n_g]
    bc_refs, bn_refs = refs[7 * n_g:8 * n_g], refs[8 * n_g:9 * n_g]
    outs = refs[9 * n_g:]
    ko_refs, vo_refs = outs[0:n_g], outs[n_g:2 * n_g]
    o_refs, l_refs = outs[2 * n_g:4 * n_g], outs[4 * n_g:6 * n_g]
    masks = _head_masks()
    pad = jnp.zeros((SPAN - n_new, GROUP_WIDTH), F32)
    lane = lax.broadcasted_iota(jnp.int32, (1, LANES), 1)
    j = pl.program_id(0) % (LANES // n_new)
    to_tail = LANES - n_new - n_new * j

    for g in range(n_g):
        kc, vc = kc_refs[g][0], vc_refs[g][0]
        kn, vn = kn_refs[g][0], vn_refs[g][0]
        cache_len = kc.shape[1]
        for src, new_t, dst in ((kc, knt_refs[g][0], ko_refs[g]), (vc, vnt_refs[g][0], vo_refs[g])):
            shifted = pltpu.roll(src, cache_len - n_new, axis=1)
            new_tail = pltpu.roll(new_t, to_tail, axis=1)
            if cache_len > LANES:
                dst[0, :, 0:cache_len - LANES] = shifted[:, 0:cache_len - LANES]
            dst[0, :, cache_len - LANES:cache_len] = jnp.where(lane >= LANES - n_new, new_tail,
                                                              shifted[:, cache_len - LANES:cache_len])
        q = q_refs[g][0]
        qm = jnp.concatenate([jnp.where(masks[h], q, 0.0) for h in range(HEADS_PER_GROUP)], axis=0).astype(BF16)
        kn_p = jnp.concatenate([kn, pad], axis=0).astype(BF16)
        vn_p = jnp.concatenate([vn, pad], axis=0).astype(BF16)
        s_c = _dot(qm, kc.astype(BF16)) + bc_refs[g][...]
        s_n = _dot_nt(qm, kn_p) + bn_refs[g][...]
        m = jnp.maximum(jnp.max(s_c, axis=-1, keepdims=True), jnp.max(s_n, axis=-1, keepdims=True))
        p_c = jnp.exp(s_c - m)
        p_n = jnp.exp(s_n - m)
        den = jnp.sum(p_c, axis=-1, keepdims=True) + jnp.sum(p_n, axis=-1, keepdims=True)
        acc = _dot_nt(p_c.astype(BF16), vc.astype(BF16)) + _dot(p_n.astype(BF16), vn_p)
        acc = acc * (1.0 / den)
        lse_rows = m + jnp.log(den)
        o = jnp.zeros((n_new, GROUP_WIDTH), F32)
        lse = jnp.zeros((n_new, GROUP_WIDTH), F32)
        for h in range(HEADS_PER_GROUP):
            rows = slice(h * n_new, (h + 1) * n_new)
            o = jnp.where(masks[h], acc[rows], o)
            lse = jnp.where(masks[h], lse_rows[rows], lse)
        for hf in (0, 1):
            o_refs[2 * g + hf][0] = o[:, hf * LANES:(hf + 1) * LANES]
            l_refs[2 * g + hf][0] = lse[:, hf * LANES:(hf + 1) * LANES]


def _cache_attn(q, k_new, v_new, k_new_t, v_new_t, k_cache, v_cache, tap_biases, *, batch, n_new):
    n_g = len(GROUPS)
    per_tile = LANES // n_new
    new3 = lambda a: a.reshape(batch, n_new, GROUP_WIDTH)
    new_spec = pl.BlockSpec((1, n_new, GROUP_WIDTH), lambda b: (b, 0, 0))
    new_t_spec = pl.BlockSpec((1, GROUP_WIDTH, LANES), lambda b: (0, 0, b // per_tile))
    cache_specs = [pl.BlockSpec((1, GROUP_WIDTH, c.shape[2]), lambda b: (b, 0, 0)) for c in k_cache]
    biases_c, biases_n = [], []
    for g, (_, dil) in enumerate(GROUPS):
        bc, bn = _cache_bias(tap_biases[g], dil, k_cache[g].shape[2], n_new)
        biases_c.append(bc)
        biases_n.append(bn)
    args = ([new3(a) for a in q] + [new3(a) for a in k_new] + [new3(a) for a in v_new]
            + list(k_new_t) + list(v_new_t) + list(k_cache) + list(v_cache) + biases_c + biases_n)
    in_specs = ([new_spec] * (3 * n_g) + [new_t_spec] * (2 * n_g) + cache_specs * 2
                + [_const_spec(b.shape) for b in biases_c] + [_const_spec(b.shape) for b in biases_n])
    half_spec = pl.BlockSpec((1, n_new, LANES), lambda b: (b, 0, 0))
    out_shape = ([jax.ShapeDtypeStruct(c.shape, F32) for c in k_cache] * 2
                 + [jax.ShapeDtypeStruct((batch, n_new, LANES), F32)] * (4 * n_g))
    out_specs = cache_specs * 2 + [half_spec] * (4 * n_g)
    outs = pl.pallas_call(
        functools.partial(_cache_attn_kernel, n_new=n_new),
        grid=(batch,),
        in_specs=in_specs,
        out_specs=out_specs,
        out_shape=out_shape,
        compiler_params=_params(1),
        name="cache_attn",
    )(*args)
    flat = lambda a: a.reshape(batch * n_new, LANES)
    halves = lambda lst: [(flat(lst[2 * g]), flat(lst[2 * g + 1])) for g in range(n_g)]
    return outs[0:n_g], outs[n_g:2 * n_g], halves(outs[2 * n_g:4 * n_g]), halves(outs[4 * n_g:6 * n_g])


def _conv_taps(ext_ref, w_ref, first_row, n_rows):
    acc = jnp.zeros((n_rows, CONV_DIM), F32)
    for j in range(CONV_WIDTH):
        acc = acc + ext_ref[pl.ds(first_row + j, n_rows), :] * w_ref[j:j + 1, :]
    return acc


CONV_CHUNK_ROWS = 64


def _conv_taps_aligned(ext_ref, w_ref, y_ref, first_row, n_rows):
    for r0 in range(0, n_rows, CONV_CHUNK_ROWS):
        for l0 in range(0, CONV_DIM, LANES):
            lanes = slice(l0, l0 + LANES)
            acc = None
            for b in range(SUBLANES):
                part = None
                for a in range((first_row + CONV_WIDTH - 1 - b) // SUBLANES + 1):
                    j = SUBLANES * a + b - first_row
                    if j < 0:
                        continue
                    term = ext_ref[pl.ds(r0 + SUBLANES * a, CONV_CHUNK_ROWS + SUBLANES), lanes] * w_ref[j:j + 1, lanes]
                    part = term if part is None else part + term
                part = part[b:b + CONV_CHUNK_ROWS]
                acc = part if acc is None else acc + part
            y_ref[r0:r0 + CONV_CHUNK_ROWS, lanes] = acc


def _conv_sample_kernel(u_ref, st_ref, w_ref, y_ref, so_ref, ext_ref, *, n_new):
    hist = CONV_WIDTH - 1
    ext_ref[0:hist, :] = st_ref[0]
    ext_ref[hist:hist + n_new, :] = u_ref[0]
    y_ref[0] = _conv_taps(ext_ref, w_ref, 0, n_new)
    so_ref[0] = ext_ref[n_new:n_new + hist, :]


def _conv_sample(u, state, w, *, batch, n_new):
    hist = CONV_WIDTH - 1
    u3 = u.reshape(batch, n_new, CONV_DIM)
    y, st = pl.pallas_call(
        functools.partial(_conv_sample_kernel, n_new=n_new),
        grid=(batch,),
        in_specs=[pl.BlockSpec((1, n_new, CONV_DIM), lambda b: (b, 0, 0)),
                  pl.BlockSpec((1, hist, CONV_DIM), lambda b: (b, 0, 0)),
                  pl.BlockSpec((CONV_WIDTH, CONV_DIM), lambda b: (0, 0))],
        out_specs=[pl.BlockSpec((1, n_new, CONV_DIM), lambda b: (b, 0, 0)),
                   pl.BlockSpec((1, hist, CONV_DIM), lambda b: (b, 0, 0))],
        out_shape=[jax.ShapeDtypeStruct((batch, n_new, CONV_DIM), F32),
                   jax.ShapeDtypeStruct((batch, hist, CONV_DIM), F32)],
        scratch_shapes=[pltpu.VMEM((hist + n_new + 2, CONV_DIM), F32)],
        compiler_params=_params(1),
        name="conv_sample",
    )(u3, state, w)
    return y.reshape(batch * n_new, CONV_DIM), st


def _merge_groups(o_refs, l_refs):
    merged_halves = []
    for hf in (0, 1):
        lses = [l_refs[2 * g + hf][...] for g in range(len(GROUPS))]
        m = functools.reduce(jnp.maximum, lses)
        es = [jnp.exp(l - m) for l in lses]
        num = sum(e * o_refs[2 * g + hf][...] for g, e in enumerate(es))
        merged_halves.append(num * (1.0 / sum(es)))
    return jnp.concatenate(merged_halves, axis=-1)


def _mix_branches(attn, yconv, gate_a, gate_b, x, wap_ref, cb_ref, lg_ref, lb_ref, wcp_ref, wo_ref):
    a_branch = _dot(attn.astype(BF16), wap_ref[...])
    y = yconv + cb_ref[...]
    mu = jnp.mean(y, axis=-1, keepdims=True)
    yc = y - mu
    var = jnp.mean(yc * yc, axis=-1, keepdims=True)
    y = yc * lax.rsqrt(var + EPS) * lg_ref[...] + lb_ref[...]
    y = y * jax.nn.sigmoid(y)
    c_branch = _dot(y.astype(BF16), wcp_ref[...])
    merged = gate_a * a_branch + gate_b * c_branch
    return x + _dot(merged.astype(BF16), wo_ref[...])


def _xattn_query(x, xg_ref, wxq_ref):
    hq = _rms(x, xg_ref[...]).astype(BF16)
    return (_dot(hq, wxq_ref[...]) * (X_HEAD_DIM ** -0.5)).astype(BF16)


def _xattn_heads(q, k_ref, v_ref):
    outs = []
    for h in range(X_HEADS):
        lanes = slice(h * X_HEAD_DIM, (h + 1) * X_HEAD_DIM)
        s = _dot_nt(q[:, lanes], k_ref[:, lanes])
        m = jnp.max(s, axis=-1, keepdims=True)
        p = jnp.exp(s - m)
        den = jnp.sum(p, axis=-1, keepdims=True)
        outs.append((_dot(p.astype(BF16), v_ref[:, lanes]) * (1.0 / den)).astype(BF16))
    return jnp.concatenate(outs, axis=-1)


def _mix_sample_kernel(*refs):
    n_half = 2 * len(GROUPS)
    o_refs, l_refs = refs[0:n_half], refs[n_half:2 * n_half]
    (yc_ref, ga_ref, gb_ref, x_ref, wap_ref, cb_ref, lg_ref, lb_ref, wcp_ref, wo_ref, xg_ref, wxq_ref,
     x_out, q_out) = refs[2 * n_half:]
    x = _mix_branches(_merge_groups(o_refs, l_refs), yc_ref[...], ga_ref[...], gb_ref[...], x_ref[...],
                      wap_ref, cb_ref, lg_ref, lb_ref, wcp_ref, wo_ref)
    x_out[...] = x
    q_out[...] = _xattn_query(x, xg_ref, wxq_ref)


def _mix_prompt_kernel(*refs):
    n_half = 2 * len(GROUPS)
    o_refs, l_refs = refs[0:n_half], refs[n_half:2 * n_half]
    (yc_ref, ga_ref, gb_ref, x_ref, wap_ref, cb_ref, lg_ref, lb_ref, wcp_ref, wo_ref,
     xg_ref, wxq_ref, mk_ref, mv_ref, wxo_ref, x_out) = refs[2 * n_half:]
    x = _mix_branches(_merge_groups(o_refs, l_refs), yc_ref[...], ga_ref[...], gb_ref[...], x_ref[...],
                      wap_ref, cb_ref, lg_ref, lb_ref, wcp_ref, wo_ref)
    attn = _xattn_heads(_xattn_query(x, xg_ref, wxq_ref), mk_ref, mv_ref)
    x_out[...] = x + _dot(attn, wxo_ref[...])


def _mix_weight_specs():
    return [_const_spec((GROUP_WIDTH, D_MODEL)), _const_spec((1, CONV_DIM)), _const_spec((1, CONV_DIM)),
            _const_spec((1, CONV_DIM)), _const_spec((CONV_DIM, D_MODEL)), _const_spec((D_MODEL, D_MODEL)),
            _const_spec((1, D_MODEL)), _const_spec((D_MODEL, D_MODEL))]


def _mix_weights(w):
    return (w['w_attn_proj'], w['conv_dw_b'], w['conv_ln_g'], w['conv_ln_b'], w['w_conv_proj'], w['w_o'],
            w['xattn_norm'], w['w_xq'])


def _mix_sample(o, lse, yconv, ga, gb, x, w, *, tm):
    rows = x.shape[0]
    half = _rows_spec(tm, LANES)
    wide = _rows_spec(tm, D_MODEL)
    o = [a for pair in o for a in pair]
    lse = [a for pair in lse for a in pair]
    return pl.pallas_call(
        _mix_sample_kernel,
        grid=(rows // tm,),
        in_specs=[half] * 12 + [_rows_spec(tm, CONV_DIM), wide, wide, wide] + _mix_weight_specs(),
        out_specs=[wide, wide],
        out_shape=[jax.ShapeDtypeStruct((rows, D_MODEL), F32), jax.ShapeDtypeStruct((rows, D_MODEL), BF16)],
        compiler_params=_params(1),
        name="mix_sample",
    )(*o, *lse, yconv, ga, gb, x, *_mix_weights(w))


def _mix_prompt(o, lse, yconv, ga, gb, x, mk, mv, w, *, tm, seq_len):
    rows = x.shape[0]
    tiles_per_seq = seq_len // tm
    half = _rows_spec(tm, LANES)
    wide = _rows_spec(tm, D_MODEL)
    mem = pl.BlockSpec((N_MEM, D_MODEL), lambda i: (i // tiles_per_seq, 0))
    o = [a for pair in o for a in pair]
    lse = [a for pair in lse for a in pair]
    return pl.pallas_call(
        _mix_prompt_kernel,
        grid=(rows // tm,),
        in_specs=([half] * 12 + [_rows_spec(tm, CONV_DIM), wide, wide, wide]
                  + _mix_weight_specs() + [mem, mem, _const_spec((D_MODEL, D_MODEL))]),
        out_specs=wide,
        out_shape=jax.ShapeDtypeStruct((rows, D_MODEL), F32),
        compiler_params=_params(1),
        name="mix_prompt",
    )(*o, *lse, yconv, ga, gb, x, *_mix_weights(w), mk, mv, w['w_xo'])


def _memkv_kernel(m_ref, g_ref, w_ref, k5_ref, v5_ref, k_ref, v_ref):
    h = _rms(m_ref[...], g_ref[...]).astype(BF16)
    for hd in range(X_HEADS):
        lo = hd * X_HEAD_DIM
        k = _dot(h, w_ref[:, lo:lo + X_HEAD_DIM])
        v = _dot(h, w_ref[:, D_MODEL + lo:D_MODEL + lo + X_HEAD_DIM])
        k5_ref[0, 0, :, hd, :] = k
        v5_ref[0, 0, :, hd, :] = v
        k_ref[:, lo:lo + X_HEAD_DIM] = k.astype(BF16)
        v_ref[:, lo:lo + X_HEAD_DIM] = v.astype(BF16)


def _memkv(mem, norm_g, w_xkv, *, batch):
    spec5 = pl.BlockSpec((1, 1, N_MEM, X_HEADS, X_HEAD_DIM), lambda b: (0, b, 0, 0, 0))
    rows = _rows_spec(N_MEM, D_MODEL)
    return pl.pallas_call(
        _memkv_kernel,
        grid=(batch,),
        in_specs=[rows, _const_spec((1, D_MODEL)), _const_spec((D_MODEL, 2 * D_MODEL))],
        out_specs=[spec5, spec5, rows, rows],
        out_shape=([jax.ShapeDtypeStruct((1, batch, N_MEM, X_HEADS, X_HEAD_DIM), F32)] * 2
                   + [jax.ShapeDtypeStruct((batch * N_MEM, D_MODEL), BF16)] * 2),
        compiler_params=_params(1),
        name="memory_kv",
    )(mem, norm_g, w_xkv)


def _xattn_cached_kernel(q_ref, k_lo, k_hi, v_lo, v_hi, mask_ref, o_ref, *, n_q):
    flat = lambda ref: ref[0, 0].reshape(N_MEM * X_HEADS, LANES)
    k_all = jnp.concatenate([flat(k_lo), flat(k_hi)], axis=1).astype(BF16)
    v_all = jnp.concatenate([flat(v_lo), flat(v_hi)], axis=1).astype(BF16)
    q = q_ref[0].astype(F32)
    qs = jnp.concatenate([q[:, h * X_HEAD_DIM:(h + 1) * X_HEAD_DIM] for h in range(X_HEADS)], axis=0).astype(BF16)
    s = _dot_nt(qs, k_all) + mask_ref[...]
    m = jnp.max(s, axis=-1, keepdims=True)
    p = jnp.exp(s - m)
    den = jnp.sum(p, axis=-1, keepdims=True)
    acc = _dot(p.astype(BF16), v_all) * (1.0 / den)
    o_ref[0] = jnp.concatenate([acc[h * n_q:(h + 1) * n_q] for h in range(X_HEADS)], axis=1).astype(BF16)


def _xattn_cached(q, mk, mv, *, batch):
    n_q = q.shape[1]
    q_spec = pl.BlockSpec((1, n_q, D_MODEL), lambda b: (b, 0, 0))
    halves = [pl.BlockSpec((1, 1, N_MEM, X_HEADS, LANES), lambda b, hf=hf: (0, b, 0, 0, hf))
              for hf in range(X_HEAD_DIM // LANES)]
    own_head = (np.arange(N_MEM * X_HEADS)[None, :] % X_HEADS) == (np.arange(X_HEADS * n_q)[:, None] // n_q)
    mask = jnp.asarray(np.where(own_head, 0.0, NEG_INF), F32)
    return pl.pallas_call(
        functools.partial(_xattn_cached_kernel, n_q=n_q),
        grid=(batch,),
        in_specs=[q_spec] + halves * 2 + [_const_spec(mask.shape)],
        out_specs=q_spec,
        out_shape=jax.ShapeDtypeStruct((batch, n_q, D_MODEL), BF16),
        compiler_params=_params(1),
        name="cross_attn_cached",
    )(q, mk, mk, mv, mv, mask)


def _to_positions_heads(a_t, batch, n_pos):
    return jnp.transpose(a_t.reshape(batch, HEADS_PER_GROUP, HEAD_DIM, n_pos), (0, 3, 1, 2))[None]


def _to_feature_major(a, batch, n_pos):
    return jnp.transpose(a[0], (0, 2, 3, 1)).reshape(batch, GROUP_WIDTH, n_pos)


def kernel(x_prompt, x_sample, mem_prompt, cache_win0_k, cache_win0_v, cache_win1_k, cache_win1_v, cache_win2_k, cache_win2_v, state_conv, cache_mem_k, cache_mem_v, rel_bias, ffn1_norm, ffn1_w_gate, ffn1_w_up, ffn1_w_down, mix_norm, w_in, w_attn_proj, conv_dw_w, conv_dw_b, conv_ln_g, conv_ln_b, w_conv_proj, w_o, xattn_norm, mem_norm, w_xq, w_xkv, w_xo, ffn2_norm, ffn2_w_gate, ffn2_w_up, ffn2_w_down, final_norm):
    batch, seq_len, _ = x_prompt.shape
    dec_batch, dec_seq, _ = x_sample.shape
    n_g = len(GROUPS)
    assert ffn1_norm.shape[0] == 1, "single layer"
    mat = lambda a: a[0].astype(BF16)
    vec = lambda a: a[0].reshape(1, -1)
    w = dict(ffn1_norm=vec(ffn1_norm), ffn1_w_gate=mat(ffn1_w_gate), ffn1_w_up=mat(ffn1_w_up),
             ffn1_w_down=mat(ffn1_w_down), mix_norm=vec(mix_norm), w_in=mat(w_in),
             w_attn_proj=mat(w_attn_proj), conv_dw_w=conv_dw_w[0], conv_dw_b=vec(conv_dw_b),
             conv_ln_g=vec(conv_ln_g), conv_ln_b=vec(conv_ln_b), w_conv_proj=mat(w_conv_proj),
             w_o=mat(w_o), xattn_norm=vec(xattn_norm), mem_norm=vec(mem_norm), w_xq=mat(w_xq),
             w_xkv=mat(w_xkv), w_xo=mat(w_xo), ffn2_norm=vec(ffn2_norm), ffn2_w_gate=mat(ffn2_w_gate),
             ffn2_w_up=mat(ffn2_w_up), ffn2_w_down=mat(ffn2_w_down))
    final_g = final_norm.reshape(1, -1)
    tap_biases = [_tap_bias(rel_bias, g, dil) for g, (_, dil) in enumerate(GROUPS)]

    def head(x, tm, rows_per_seq, t_groups, conv_w=None):
        x1 = _ffn(x, w['ffn1_norm'], w['ffn1_w_gate'], w['ffn1_w_up'], w['ffn1_w_down'], tm=tm)
        outs = _win(x1, w['mix_norm'], w['w_in'], tm=tm, rows_per_seq=rows_per_seq, t_groups=t_groups, conv_w=conv_w)
        n_t = 2 * len(t_groups)
        return x1, outs[0:3], outs[3:6], outs[6:9], outs[9], outs[10], outs[11:11 + n_t], outs[11 + n_t:]

    rows_p = batch * seq_len
    tm_p = 512
    full_groups = [g for g, (window, _) in enumerate(GROUPS) if window >= seq_len]
    x1, q, k, v, ga, gb, kv_t, (yconv, u_tail) = head(x_prompt.reshape(rows_p, D_MODEL), tm_p, seq_len, full_groups,
                                                      conv_w=w['conv_dw_w'])
    o, lse = [], []
    for g, (_, dil) in enumerate(GROUPS):
        o_g, l_g = _band_attn(q[g], k[g], v[g], _band_bias(tap_biases[g]), batch=batch, seq_len=seq_len, dil=dil)
        o.append(o_g)
        lse.append(l_g)
    p_mem_k, p_mem_v, mk_p, mv_p = _memkv(mem_prompt.reshape(batch * N_MEM, D_MODEL), w['mem_norm'], w['w_xkv'],
                                          batch=batch)
    x3 = _mix_prompt(o, lse, yconv, ga, gb, x1, mk_p, mv_p, w, tm=tm_p, seq_len=seq_len)
    y_prompt = _ffn(x3, w['ffn2_norm'], w['ffn2_w_gate'], w['ffn2_w_up'], w['ffn2_w_down'], tm=tm_p,
                    final_g=final_g)

    p_win = []
    for g, (window, _) in enumerate(GROUPS):
        keep = min(window, seq_len)
        if g in full_groups:
            j = full_groups.index(g)
            p_win += [_to_positions_heads(kv_t[2 * j], batch, seq_len), _to_positions_heads(kv_t[2 * j + 1], batch, seq_len)]
        else:
            for a in (k[g], v[g]):
                p_win.append(a.reshape(batch, seq_len, GROUP_WIDTH)[:, seq_len - keep:]
                             .reshape(1, batch, keep, HEADS_PER_GROUP, HEAD_DIM))
    p_conv = u_tail[:, CONV_HALO - (CONV_WIDTH - 1):][None]

    rows_s = dec_batch * dec_seq
    x1, q, k, v, ga, gb, kv_t, (u,) = head(x_sample.reshape(rows_s, D_MODEL), rows_s, rows_s, list(range(n_g)))
    caches_k = [_to_feature_major(c, dec_batch, c.shape[2]) for c in (cache_win0_k, cache_win1_k, cache_win2_k)]
    caches_v = [_to_feature_major(c, dec_batch, c.shape[2]) for c in (cache_win0_v, cache_win1_v, cache_win2_v)]
    new_k, new_v, o, lse = _cache_attn(q, k, v, kv_t[0::2], kv_t[1::2], caches_k, caches_v, tap_biases,
                                       batch=dec_batch, n_new=dec_seq)
    yconv, s_conv = _conv_sample(u, state_conv[0], w['conv_dw_w'], batch=dec_batch, n_new=dec_seq)
    x2, xq = _mix_sample(o, lse, yconv, ga, gb, x1, w, tm=rows_s)
    xo = _xattn_cached(xq.reshape(dec_batch, dec_seq, D_MODEL), cache_mem_k, cache_mem_v, batch=dec_batch)
    y_sample = _ffn(x2, w['ffn2_norm'], w['ffn2_w_gate'], w['ffn2_w_up'], w['ffn2_w_down'], tm=rows_s,
                    pre=(xo.reshape(rows_s, D_MODEL), w['w_xo']), final_g=final_g)

    s_win = []
    for g in range(n_g):
        for a in (new_k[g], new_v[g]):
            s_win.append(_to_positions_heads(a, dec_batch, a.shape[2]))

    return (y_prompt.reshape(batch, seq_len, D_MODEL), y_sample.reshape(dec_batch, dec_seq, D_MODEL),
            *p_win, p_conv, p_mem_k, p_mem_v, *s_win, s_conv[None])
```

```python
import functools
import math

import jax
import jax.numpy as jnp
import numpy as np
from jax import lax
from jax.experimental import pallas as pl
from jax.experimental.pallas import tpu as pltpu

D_MODEL = 1024
HEAD_DIM = 64
GROUPS = ((128, 1), (512, 4), (2048, 16))
HEADS_PER_GROUP = 4
GROUP_WIDTH = HEADS_PER_GROUP * HEAD_DIM
ATTN_WIDTH = len(GROUPS) * GROUP_WIDTH
CONV_DIM = D_MODEL // 2
CONV_WIDTH = 31
CONV_HALO = 32
N_BUCKETS = 32
MAX_DISTANCE = 2048
D_FF = ((8 * D_MODEL // 3 + 127) // 128) * 128
N_MEM = 256
X_HEADS = 4
X_HEAD_DIM = D_MODEL // X_HEADS
EPS = 1e-6
NEG_INF = -1e30
SPAN = 128
LANES = 128
SUBLANES = 8

V7X_VMEM_LIMIT_BYTES = 56 * 1024 * 1024
BF16 = jnp.bfloat16
F32 = jnp.float32


def _params(n_axes):
    return pltpu.CompilerParams(dimension_semantics=("parallel",) * n_axes,
                                vmem_limit_bytes=V7X_VMEM_LIMIT_BYTES)


def _const_spec(shape):
    return pl.BlockSpec(shape, lambda *_: (0,) * len(shape), pipeline_mode=pl.Buffered(1))


def _rows_spec(tm, width):
    return pl.BlockSpec((tm, width), lambda i: (i, 0))


def _rms(x, g):
    return x * lax.rsqrt(jnp.mean(x * x, axis=-1, keepdims=True) + EPS) * g


def _dot(a, b):
    return jnp.dot(a, b, preferred_element_type=F32)


def _dot_nt(a, b):
    return lax.dot_general(a, b, (((1,), (1,)), ((), ())), preferred_element_type=F32)


def _head_masks(width=GROUP_WIDTH):
    lane = lax.broadcasted_iota(jnp.int32, (1, width), 1)
    return [(lane >= h * HEAD_DIM) & (lane < (h + 1) * HEAD_DIM) for h in range(HEADS_PER_GROUP)]


def _ffn_kernel(*refs, pre_proj, final_norm):
    refs = list(refs)
    x_ref = refs.pop(0)
    if pre_proj:
        a_ref, wp_ref = refs.pop(0), refs.pop(0)
    g_ref, wg_ref, wu_ref, wd_ref = refs.pop(0), refs.pop(0), refs.pop(0), refs.pop(0)
    if final_norm:
        fg_ref = refs.pop(0)
    o_ref = refs.pop(0)

    x = x_ref[...]
    if pre_proj:
        x = x + _dot(a_ref[...], wp_ref[...])
    h = _rms(x, g_ref[...]).astype(BF16)
    gate = _dot(h, wg_ref[...])
    up = _dot(h, wu_ref[...])
    act = (gate * jax.nn.sigmoid(gate) * up).astype(BF16)
    x = x + 0.5 * _dot(act, wd_ref[...])
    if final_norm:
        x = _rms(x, fg_ref[...])
    o_ref[...] = x


def _ffn(x, norm_g, wg, wu, wd, *, tm, pre=None, final_g=None):
    rows = x.shape[0]
    args = [x]
    specs = [_rows_spec(tm, D_MODEL)]
    if pre is not None:
        a, wp = pre
        args += [a, wp]
        specs += [_rows_spec(tm, D_MODEL), _const_spec((D_MODEL, D_MODEL))]
    args += [norm_g, wg, wu, wd]
    specs += [_const_spec((1, D_MODEL)), _const_spec((D_MODEL, D_FF)), _const_spec((D_MODEL, D_FF)),
              _const_spec((D_FF, D_MODEL))]
    if final_g is not None:
        args.append(final_g)
        specs.append(_const_spec((1, D_MODEL)))
    return pl.pallas_call(
        functools.partial(_ffn_kernel, pre_proj=pre is not None, final_norm=final_g is not None),
        grid=(rows // tm,),
        in_specs=specs,
        out_specs=_rows_spec(tm, D_MODEL),
        out_shape=jax.ShapeDtypeStruct((rows, D_MODEL), F32),
        compiler_params=_params(1),
        name="ffn",
    )(*args)


_Q_END = ATTN_WIDTH
_K_END = 2 * ATTN_WIDTH
_V_END = 3 * ATTN_WIDTH
_UA_END = _V_END + CONV_DIM
_UB_END = _UA_END + CONV_DIM
_GA_END = _UB_END + D_MODEL
IN_WIDTH = _GA_END + D_MODEL


def _win_kernel(*refs, t_groups, fuse_conv, tm, tiles_per_seq):
    x_ref, g_ref, w_ref = refs[:3]
    cw_ref = refs[3] if fuse_conv else None
    outs = refs[4:] if fuse_conv else refs[3:]
    q_ref, k_ref, v_ref, ga_ref, gb_ref = outs[:5]
    n_t = 2 * len(t_groups)
    t_refs = outs[5:5 + n_t]
    h = _rms(x_ref[...], g_ref[...]).astype(BF16)

    def seg(lo, hi):
        return _dot(h, w_ref[:, lo:hi])

    u = seg(_V_END, _UA_END) * jax.nn.sigmoid(seg(_UA_END, _UB_END))
    if fuse_conv:
        yc_ref, tail_ref, ext_ref = outs[5 + n_t:]
        first_tile = pl.program_id(0) % tiles_per_seq == 0

        @pl.when(first_tile)
        def _():
            ext_ref[0:CONV_HALO, :] = jnp.zeros((CONV_HALO, CONV_DIM), F32)

        @pl.when(jnp.logical_not(first_tile))
        def _():
            ext_ref[0:CONV_HALO, :] = ext_ref[tm:tm + CONV_HALO, :]

        ext_ref[CONV_HALO:CONV_HALO + tm, :] = u
        ext_ref[CONV_HALO + tm:CONV_HALO + tm + SUBLANES, :] = jnp.zeros((SUBLANES, CONV_DIM), F32)
        _conv_taps_aligned(ext_ref, cw_ref, yc_ref, CONV_HALO - (CONV_WIDTH - 1), tm)
        tail_ref[0] = u[tm - CONV_HALO:tm]
    else:
        outs[5 + n_t][...] = u
    for g in range(len(GROUPS)):
        lo = g * GROUP_WIDTH
        q_ref[g] = seg(lo, lo + GROUP_WIDTH) * (HEAD_DIM ** -0.5)
        k = seg(_Q_END + lo, _Q_END + lo + GROUP_WIDTH)
        v = seg(_K_END + lo, _K_END + lo + GROUP_WIDTH)
        k_ref[g] = k
        v_ref[g] = v
        if g in t_groups:
            j = t_groups.index(g)
            t_refs[2 * j][0] = k.T
            t_refs[2 * j + 1][0] = v.T
    ga_ref[...] = jax.nn.sigmoid(seg(_UB_END, _GA_END)).astype(BF16)
    gb_ref[...] = jax.nn.sigmoid(seg(_GA_END, IN_WIDTH)).astype(BF16)


def _win(x, norm_g, w_in, *, tm, rows_per_seq, t_groups, conv_w=None):
    rows = x.shape[0]
    n_seq = rows // rows_per_seq
    n_t = 2 * len(t_groups)
    tiles_per_seq = rows_per_seq // tm
    fuse_conv = conv_w is not None
    n_g = len(GROUPS)
    out_shape = ([jax.ShapeDtypeStruct((n_g, rows, GROUP_WIDTH), F32)] * 3
                 + [jax.ShapeDtypeStruct((rows, D_MODEL), BF16)] * 2
                 + [jax.ShapeDtypeStruct((n_seq, GROUP_WIDTH, rows_per_seq), F32)] * n_t
                 + [jax.ShapeDtypeStruct((rows, CONV_DIM), F32)])
    t_spec = pl.BlockSpec((1, GROUP_WIDTH, tm), lambda i: (i // tiles_per_seq, 0, i % tiles_per_seq))
    out_specs = ([pl.BlockSpec((n_g, tm, GROUP_WIDTH), lambda i: (0, i, 0))] * 3
                 + [_rows_spec(tm, D_MODEL)] * 2 + [t_spec] * n_t
                 + [_rows_spec(tm, CONV_DIM)])
    in_specs = [_rows_spec(tm, D_MODEL), _const_spec((1, D_MODEL)), _const_spec((D_MODEL, IN_WIDTH))]
    args = [x, norm_g, w_in]
    scratch = []
    if fuse_conv:
        in_specs.append(_const_spec((CONV_WIDTH, CONV_DIM)))
        args.append(conv_w)
        out_shape.append(jax.ShapeDtypeStruct((n_seq, CONV_HALO, CONV_DIM), F32))
        out_specs.append(pl.BlockSpec((1, CONV_HALO, CONV_DIM), lambda i: (i // tiles_per_seq, 0, 0)))
        scratch.append(pltpu.VMEM((CONV_HALO + tm + SUBLANES, CONV_DIM), F32))
    return pl.pallas_call(
        functools.partial(_win_kernel, t_groups=tuple(t_groups), fuse_conv=fuse_conv, tm=tm,
                          tiles_per_seq=tiles_per_seq),
        grid=(rows // tm,),
        in_specs=in_specs,
        out_specs=out_specs,
        out_shape=out_shape,
        scratch_shapes=scratch,
        compiler_params=pltpu.CompilerParams(dimension_semantics=("arbitrary",),
                                             vmem_limit_bytes=V7X_VMEM_LIMIT_BYTES),
        name="w_in",
    )(*args)


def _rel_bucket(dist):
    n = jnp.maximum(dist, 0)
    max_exact = N_BUCKETS // 2
    nf = jnp.maximum(n, 1).astype(F32)
    large = max_exact + (jnp.log(nf / max_exact) / math.log(MAX_DISTANCE / max_exact)
                         * (N_BUCKETS - max_exact)).astype(jnp.int32)
    return jnp.where(n < max_exact, n, jnp.minimum(large, N_BUCKETS - 1))


def _tap_bias(rel_bias, g, dil):
    bias_g = rel_bias[:, g * HEADS_PER_GROUP:(g + 1) * HEADS_PER_GROUP]
    return bias_g[_rel_bucket(jnp.arange(SPAN, -1, -1) * dil)].astype(F32).T


def _band_bias(tap_bias):
    period = 3 * SPAN
    vec = jnp.concatenate([tap_bias, jnp.full((HEADS_PER_GROUP, period - (SPAN + 1)), NEG_INF, F32)], axis=1)
    flat = jnp.tile(vec, (1, SPAN))[:, :SPAN * (period - 1)]
    return flat.reshape(HEADS_PER_GROUP, SPAN, period - 1)[:, :, :2 * SPAN].reshape(HEADS_PER_GROUP * SPAN, 2 * SPAN)


def _cache_bias(tap_bias, dil, cache_len, n_new):
    rev = lax.pad(tap_bias, jnp.float32(NEG_INF), [(0, 0, 0), (cache_len + n_new - 1 - SPAN * dil, 0, dil - 1)])
    bias_c = jnp.stack([rev[:, n_new - 1 - t:n_new - 1 - t + cache_len] for t in range(n_new)], axis=1)
    near = jnp.concatenate([rev[:, cache_len:], jnp.full((HEADS_PER_GROUP, LANES), NEG_INF, F32)], axis=1)
    bias_n = jnp.stack([near[:, n_new - 1 - t:n_new - 1 - t + LANES] for t in range(n_new)], axis=1)
    return (bias_c.reshape(HEADS_PER_GROUP * n_new, cache_len), bias_n.reshape(HEADS_PER_GROUP * n_new, LANES))


MAX_ROW_STRIDE = 4


def _gather_residues(src, store, tmp_ref, dil, seq):
    if dil == 1:
        store(0, src[...])
    elif dil <= MAX_ROW_STRIDE:
        for r in range(dil):
            store(r, src[pl.ds(r, seq, stride=dil), :])
    else:
        inner = dil // MAX_ROW_STRIDE
        assert inner <= MAX_ROW_STRIDE and inner * MAX_ROW_STRIDE == dil
        n_part = seq * inner
        for c in range(MAX_ROW_STRIDE):
            tmp_ref[c * n_part:(c + 1) * n_part, :] = src[pl.ds(c, n_part, stride=MAX_ROW_STRIDE), :]
        for c in range(MAX_ROW_STRIDE):
            for c2 in range(inner):
                store(c + MAX_ROW_STRIDE * c2, tmp_ref[pl.ds(c * n_part + c2, seq, stride=inner), :])


def _scatter_residues(load, dst, tmp_ref, dil, seq):
    if dil == 1:
        dst[...] = load(0)
    elif dil <= MAX_ROW_STRIDE:
        for r in range(dil):
            dst[pl.ds(r, seq, stride=dil), :] = load(r)
    else:
        inner = dil // MAX_ROW_STRIDE
        assert inner <= MAX_ROW_STRIDE and inner * MAX_ROW_STRIDE == dil
        n_part = seq * inner
        for c in range(MAX_ROW_STRIDE):
            for c2 in range(inner):
                tmp_ref[pl.ds(c * n_part + c2, seq, stride=inner), :] = load(c + MAX_ROW_STRIDE * c2)
        for c in range(MAX_ROW_STRIDE):
            dst[pl.ds(c, n_part, stride=MAX_ROW_STRIDE), :] = tmp_ref[c * n_part:(c + 1) * n_part, :]


def _merge_groups(outs, lses):
    m = functools.reduce(jnp.maximum, lses)
    es = [jnp.exp(l - m) for l in lses]
    num = sum(e * o for e, o in zip(es, outs))
    return num * (1.0 / sum(es))


def _band_group(q_lo, q_hi, k_lo, k_hi, v_lo, v_hi, b_ref, o_dst, l_dst, qb_ref, kb_ref, vb_ref, ob_ref, lb_ref,
                tmp_ref, *, dil, seq):
    n_blk = seq // SPAN
    halves = (slice(0, LANES), slice(LANES, 2 * LANES))
    for dst, srcs in ((qb_ref, (q_lo, q_hi)), (kb_ref, (k_lo, k_hi)), (vb_ref, (v_lo, v_hi))):
        for lanes, src in zip(halves, srcs):
            def store(r, rows, dst=dst, lanes=lanes):
                dst[r * seq:(r + 1) * seq, lanes] = rows.astype(BF16)
            _gather_residues(src, store, tmp_ref, dil, seq)
    masks = _head_masks()

    def block(q, kb, vb, key_lo):
        qm = jnp.concatenate([jnp.where(masks[h], q, jnp.zeros_like(q)) for h in range(HEADS_PER_GROUP)], axis=0)
        s = _dot_nt(qm, kb) + b_ref[:, key_lo:]
        m = jnp.max(s, axis=-1, keepdims=True)
        p = jnp.exp(s - m)
        den = jnp.sum(p, axis=-1, keepdims=True)
        acc = _dot(p.astype(BF16), vb) * (1.0 / den)
        lse_rows = m + jnp.log(den)
        o = acc[0:SPAN]
        lse = jnp.broadcast_to(lse_rows[0:SPAN], (SPAN, GROUP_WIDTH))
        for h in range(1, HEADS_PER_GROUP):
            o = jnp.where(masks[h], acc[h * SPAN:(h + 1) * SPAN], o)
            lse = jnp.where(masks[h], lse_rows[h * SPAN:(h + 1) * SPAN], lse)
        return o, lse

    for r in range(dil):
        for j in range(n_blk):
            q_rows = slice(r * seq + j * SPAN, r * seq + (j + 1) * SPAN)
            k_rows = slice(r * seq + max(j - 1, 0) * SPAN, r * seq + (j + 1) * SPAN)
            o, lse = block(qb_ref[q_rows, :], kb_ref[k_rows, :], vb_ref[k_rows, :], SPAN if j == 0 else 0)
            ob_ref[q_rows, :] = o
            lb_ref[q_rows, :] = lse

    for src, dsts in ((ob_ref, o_dst), (lb_ref, l_dst)):
        for lanes, dst in zip(halves, dsts):
            _scatter_residues(lambda r, src=src, lanes=lanes: src[r * seq:(r + 1) * seq, lanes], dst, tmp_ref, dil, seq)


MERGE_CHUNK_ROWS = 256


def _band_attn_kernel(q_lo, q_hi, k_lo, k_hi, v_lo, v_hi, b_ref, out_ref,
                      qb_ref, kb_ref, vb_ref, ob_ref, lb_ref, tmp_ref, *nat_refs, seq_len):
    n_half = 2 * len(GROUPS)
    on_refs, ln_refs = nat_refs[:n_half], nat_refs[n_half:]
    g = pl.program_id(1)
    for gi, (_, dil) in enumerate(GROUPS):
        @pl.when(g == gi)
        def _(gi=gi, dil=dil):
            _band_group(q_lo, q_hi, k_lo, k_hi, v_lo, v_hi, b_ref,
                        on_refs[2 * gi:2 * gi + 2], ln_refs[2 * gi:2 * gi + 2],
                        qb_ref, kb_ref, vb_ref, ob_ref, lb_ref, tmp_ref, dil=dil, seq=seq_len // dil)

    @pl.when(g == len(GROUPS) - 1)
    def _():
        for r0 in range(0, seq_len, MERGE_CHUNK_ROWS):
            rows = slice(r0, r0 + MERGE_CHUNK_ROWS)
            for hf in (0, 1):
                merged = _merge_groups([on_refs[2 * gi + hf][rows, :] for gi in range(len(GROUPS))],
                                       [ln_refs[2 * gi + hf][rows, :] for gi in range(len(GROUPS))])
                out_ref[rows, hf * LANES:(hf + 1) * LANES] = merged.astype(BF16)


def _band_attn(q, k, v, biases, *, batch, seq_len):
    n_g = len(GROUPS)
    view = lambda a: a.reshape(n_g, batch, seq_len, GROUP_WIDTH)
    in_halves = [pl.BlockSpec((None, None, seq_len, LANES), lambda b, g, hf=hf: (g, b, 0, hf)) for hf in (0, 1)]
    out = pl.pallas_call(
        functools.partial(_band_attn_kernel, seq_len=seq_len),
        grid=(batch, n_g),
        in_specs=in_halves * 3 + [pl.BlockSpec((None, HEADS_PER_GROUP * SPAN, 2 * SPAN), lambda b, g: (g, 0, 0))],
        out_specs=pl.BlockSpec((None, seq_len, GROUP_WIDTH), lambda b, g: (b, 0, 0)),
        out_shape=jax.ShapeDtypeStruct((batch, seq_len, GROUP_WIDTH), BF16),
        scratch_shapes=([pltpu.VMEM((seq_len, GROUP_WIDTH), BF16)] * 3 + [pltpu.VMEM((seq_len, GROUP_WIDTH), F32)] * 2
                        + [pltpu.VMEM((seq_len, LANES), F32)]
                        + [pltpu.VMEM((seq_len, LANES), F32)] * (4 * n_g)),
        compiler_params=pltpu.CompilerParams(dimension_semantics=("parallel", "arbitrary"),
                                             vmem_limit_bytes=V7X_VMEM_LIMIT_BYTES),
        name="band_attn",
    )(view(q), view(q), view(k), view(k), view(v), view(v), biases)
    return out.reshape(batch * seq_len, GROUP_WIDTH)


def _cache_attn_kernel(*refs, n_new):
    n_g = len(GROUPS)
    q_refs, kn_refs, vn_refs = refs[0:n_g], refs[n_g:2 * n_g], refs[2 * n_g:3 * n_g]
    knt_refs, vnt_refs = refs[3 * n_g:4 * n_g], refs[4 * n_g:5 * n_g]
    kc_refs, vc_refs = refs[5 * n_g:6 * n_g], refs[6 * n_g:7 * n_g]
    bc_refs, bn_refs = refs[7 * n_g:8 * n_g], refs[8 * n_g:9 * n_g]
    outs = refs[9 * n_g:]
    ko_refs, vo_refs = outs[0:n_g], outs[n_g:2 * n_g]
    attn_ref = outs[2 * n_g]
    group_out, group_lse = [], []
    masks = _head_masks()
    pad = jnp.zeros((SPAN - n_new, GROUP_WIDTH), F32)
    lane = lax.broadcasted_iota(jnp.int32, (1, LANES), 1)
    j = pl.program_id(0) % (LANES // n_new)
    to_tail = LANES - n_new - n_new * j

    for g in range(n_g):
        kc, vc = kc_refs[g][0], vc_refs[g][0]
        kn, vn = kn_refs[g][0], vn_refs[g][0]
        cache_len = kc.shape[1]
        for src, new_t, dst in ((kc, knt_refs[g][0], ko_refs[g]), (vc, vnt_refs[g][0], vo_refs[g])):
            shifted = pltpu.roll(src, cache_len - n_new, axis=1)
            new_tail = pltpu.roll(new_t, to_tail, axis=1)
            if cache_len > LANES:
                dst[0, :, 0:cache_len - LANES] = shifted[:, 0:cache_len - LANES]
            dst[0, :, cache_len - LANES:cache_len] = jnp.where(lane >= LANES - n_new, new_tail,
                                                              shifted[:, cache_len - LANES:cache_len])
        q = q_refs[g][0]
        qm = jnp.concatenate([jnp.where(masks[h], q, 0.0) for h in range(HEADS_PER_GROUP)], axis=0).astype(BF16)
        kn_p = jnp.concatenate([kn, pad], axis=0).astype(BF16)
        vn_p = jnp.concatenate([vn, pad], axis=0).astype(BF16)
        s_c = _dot(qm, kc.astype(BF16)) + bc_refs[g][...]
        s_n = _dot_nt(qm, kn_p) + bn_refs[g][...]
        m = jnp.maximum(jnp.max(s_c, axis=-1, keepdims=True), jnp.max(s_n, axis=-1, keepdims=True))
        p_c = jnp.exp(s_c - m)
        p_n = jnp.exp(s_n - m)
        den = jnp.sum(p_c, axis=-1, keepdims=True) + jnp.sum(p_n, axis=-1, keepdims=True)
        acc = _dot_nt(p_c.astype(BF16), vc.astype(BF16)) + _dot(p_n.astype(BF16), vn_p)
        acc = acc * (1.0 / den)
        lse_rows = m + jnp.log(den)
        o = jnp.zeros((n_new, GROUP_WIDTH), F32)
        lse = jnp.zeros((n_new, GROUP_WIDTH), F32)
        for h in range(HEADS_PER_GROUP):
            rows = slice(h * n_new, (h + 1) * n_new)
            o = jnp.where(masks[h], acc[rows], o)
            lse = jnp.where(masks[h], lse_rows[rows], lse)
        group_out.append(o)
        group_lse.append(lse)
    attn_ref[0] = _merge_groups(group_out, group_lse).astype(BF16)


def _cache_attn(q, k_new, v_new, k_new_t, v_new_t, k_cache, v_cache, tap_biases, *, batch, n_new):
    n_g = len(GROUPS)
    per_tile = LANES // n_new
    new3 = lambda a: a.reshape(batch, n_new, GROUP_WIDTH)
    new_spec = pl.BlockSpec((1, n_new, GROUP_WIDTH), lambda b: (b, 0, 0))
    new_t_spec = pl.BlockSpec((1, GROUP_WIDTH, LANES), lambda b: (0, 0, b // per_tile))
    cache_specs = [pl.BlockSpec((1, GROUP_WIDTH, c.shape[2]), lambda b: (b, 0, 0)) for c in k_cache]
    biases_c, biases_n = [], []
    for g, (_, dil) in enumerate(GROUPS):
        bc, bn = _cache_bias(tap_biases[g], dil, k_cache[g].shape[2], n_new)
        biases_c.append(bc)
        biases_n.append(bn)
    args = ([new3(a) for a in q] + [new3(a) for a in k_new] + [new3(a) for a in v_new]
            + list(k_new_t) + list(v_new_t) + list(k_cache) + list(v_cache) + biases_c + biases_n)
    in_specs = ([new_spec] * (3 * n_g) + [new_t_spec] * (2 * n_g) + cache_specs * 2
                + [_const_spec(b.shape) for b in biases_c] + [_const_spec(b.shape) for b in biases_n])
    out_shape = ([jax.ShapeDtypeStruct(c.shape, F32) for c in k_cache] * 2
                 + [jax.ShapeDtypeStruct((batch, n_new, GROUP_WIDTH), BF16)])
    out_specs = cache_specs * 2 + [new_spec]
    outs = pl.pallas_call(
        functools.partial(_cache_attn_kernel, n_new=n_new),
        grid=(batch,),
        in_specs=in_specs,
        out_specs=out_specs,
        out_shape=out_shape,
        compiler_params=_params(1),
        name="cache_attn",
    )(*args)
    return outs[0:n_g], outs[n_g:2 * n_g], outs[2 * n_g].reshape(batch * n_new, GROUP_WIDTH)


def _conv_taps(ext_ref, w_ref, first_row, n_rows):
    acc = jnp.zeros((n_rows, CONV_DIM), F32)
    for j in range(CONV_WIDTH):
        acc = acc + ext_ref[pl.ds(first_row + j, n_rows), :] * w_ref[j:j + 1, :]
    return acc


CONV_CHUNK_ROWS = 64


def _conv_taps_aligned(ext_ref, w_ref, y_ref, first_row, n_rows):
    for r0 in range(0, n_rows, CONV_CHUNK_ROWS):
        for l0 in range(0, CONV_DIM, LANES):
            lanes = slice(l0, l0 + LANES)
            acc = None
            for b in range(SUBLANES):
                part = None
                for a in range((first_row + CONV_WIDTH - 1 - b) // SUBLANES + 1):
                    j = SUBLANES * a + b - first_row
                    if j < 0:
                        continue
                    term = ext_ref[pl.ds(r0 + SUBLANES * a, CONV_CHUNK_ROWS + SUBLANES), lanes] * w_ref[j:j + 1, lanes]
                    part = term if part is None else part + term
                part = part[b:b + CONV_CHUNK_ROWS]
                acc = part if acc is None else acc + part
            y_ref[r0:r0 + CONV_CHUNK_ROWS, lanes] = acc


def _conv_sample_kernel(u_ref, st_ref, w_ref, y_ref, so_ref, ext_ref, *, n_new, n_seq):
    hist = CONV_WIDTH - 1
    for b in range(n_seq):
        ext_ref[b, 0:hist, :] = st_ref[b]
        ext_ref[b, hist:hist + n_new, :] = u_ref[b]
        y_ref[b] = _conv_taps(ext_ref.at[b], w_ref, 0, n_new)
        so_ref[b] = ext_ref[b, n_new:n_new + hist, :]


def _conv_sample(u, state, w, *, batch, n_new, seqs_per_step=8):
    hist = CONV_WIDTH - 1
    u3 = u.reshape(batch, n_new, CONV_DIM)
    n_seq = math.gcd(batch, seqs_per_step)
    y, st = pl.pallas_call(
        functools.partial(_conv_sample_kernel, n_new=n_new, n_seq=n_seq),
        grid=(batch // n_seq,),
        in_specs=[pl.BlockSpec((n_seq, n_new, CONV_DIM), lambda b: (b, 0, 0)),
                  pl.BlockSpec((n_seq, hist, CONV_DIM), lambda b: (b, 0, 0)),
                  pl.BlockSpec((CONV_WIDTH, CONV_DIM), lambda b: (0, 0))],
        out_specs=[pl.BlockSpec((n_seq, n_new, CONV_DIM), lambda b: (b, 0, 0)),
                   pl.BlockSpec((n_seq, hist, CONV_DIM), lambda b: (b, 0, 0))],
        out_shape=[jax.ShapeDtypeStruct((batch, n_new, CONV_DIM), F32),
                   jax.ShapeDtypeStruct((batch, hist, CONV_DIM), F32)],
        scratch_shapes=[pltpu.VMEM((n_seq, hist + n_new + 2, CONV_DIM), F32)],
        compiler_params=_params(1),
        name="conv_sample",
    )(u3, state, w)
    return y.reshape(batch * n_new, CONV_DIM), st


def _mix_branches(attn, yconv, gate_a, gate_b, x, wap_ref, cb_ref, lg_ref, lb_ref, wcp_ref, wo_ref):
    a_branch = _dot(attn, wap_ref[...])
    y = yconv + cb_ref[...]
    mu = jnp.mean(y, axis=-1, keepdims=True)
    yc = y - mu
    var = jnp.mean(yc * yc, axis=-1, keepdims=True)
    y = yc * lax.rsqrt(var + EPS) * lg_ref[...] + lb_ref[...]
    y = y * jax.nn.sigmoid(y)
    c_branch = _dot(y.astype(BF16), wcp_ref[...])
    merged = gate_a * a_branch + gate_b * c_branch
    return x + _dot(merged.astype(BF16), wo_ref[...])


def _xattn_query(x, xg_ref, wxq_ref):
    hq = _rms(x, xg_ref[...]).astype(BF16)
    return (_dot(hq, wxq_ref[...]) * (X_HEAD_DIM ** -0.5)).astype(BF16)


def _xattn_heads(q, k_ref, v_ref):
    outs = []
    for h in range(X_HEADS):
        lanes = slice(h * X_HEAD_DIM, (h + 1) * X_HEAD_DIM)
        s = _dot_nt(q[:, lanes], k_ref[:, lanes])
        m = jnp.max(s, axis=-1, keepdims=True)
        p = jnp.exp(s - m)
        den = jnp.sum(p, axis=-1, keepdims=True)
        outs.append((_dot(p.astype(BF16), v_ref[:, lanes]) * (1.0 / den)).astype(BF16))
    return jnp.concatenate(outs, axis=-1)


def _mix_sample_kernel(at_ref, yc_ref, ga_ref, gb_ref, x_ref, wap_ref, cb_ref, lg_ref, lb_ref, wcp_ref, wo_ref,
                       xg_ref, wxq_ref, x_out, q_out):
    x = _mix_branches(at_ref[...], yc_ref[...], ga_ref[...], gb_ref[...], x_ref[...],
                      wap_ref, cb_ref, lg_ref, lb_ref, wcp_ref, wo_ref)
    x_out[...] = x
    q_out[...] = _xattn_query(x, xg_ref, wxq_ref)


def _mix_prompt_kernel(at_ref, yc_ref, ga_ref, gb_ref, x_ref, wap_ref, cb_ref, lg_ref, lb_ref, wcp_ref, wo_ref,
                       xg_ref, wxq_ref, mk_ref, mv_ref, wxo_ref, x_out):
    x = _mix_branches(at_ref[...], yc_ref[...], ga_ref[...], gb_ref[...], x_ref[...],
                      wap_ref, cb_ref, lg_ref, lb_ref, wcp_ref, wo_ref)
    attn = _xattn_heads(_xattn_query(x, xg_ref, wxq_ref), mk_ref, mv_ref)
    x_out[...] = x + _dot(attn, wxo_ref[...])


def _mix_weight_specs():
    return [_const_spec((GROUP_WIDTH, D_MODEL)), _const_spec((1, CONV_DIM)), _const_spec((1, CONV_DIM)),
            _const_spec((1, CONV_DIM)), _const_spec((CONV_DIM, D_MODEL)), _const_spec((D_MODEL, D_MODEL)),
            _const_spec((1, D_MODEL)), _const_spec((D_MODEL, D_MODEL))]


def _mix_weights(w):
    return (w['w_attn_proj'], w['conv_dw_b'], w['conv_ln_g'], w['conv_ln_b'], w['w_conv_proj'], w['w_o'],
            w['xattn_norm'], w['w_xq'])


def _mix_sample(attn, yconv, ga, gb, x, w, *, tm):
    rows = x.shape[0]
    wide = _rows_spec(tm, D_MODEL)
    return pl.pallas_call(
        _mix_sample_kernel,
        grid=(rows // tm,),
        in_specs=[_rows_spec(tm, GROUP_WIDTH), _rows_spec(tm, CONV_DIM), wide, wide, wide] + _mix_weight_specs(),
        out_specs=[wide, wide],
        out_shape=[jax.ShapeDtypeStruct((rows, D_MODEL), F32), jax.ShapeDtypeStruct((rows, D_MODEL), BF16)],
        compiler_params=_params(1),
        name="mix_sample",
    )(attn, yconv, ga, gb, x, *_mix_weights(w))


def _mix_prompt(attn, yconv, ga, gb, x, mk, mv, w, *, tm, seq_len):
    rows = x.shape[0]
    tiles_per_seq = seq_len // tm
    wide = _rows_spec(tm, D_MODEL)
    mem = pl.BlockSpec((N_MEM, D_MODEL), lambda i: (i // tiles_per_seq, 0))
    return pl.pallas_call(
        _mix_prompt_kernel,
        grid=(rows // tm,),
        in_specs=([_rows_spec(tm, GROUP_WIDTH), _rows_spec(tm, CONV_DIM), wide, wide, wide]
                  + _mix_weight_specs() + [mem, mem, _const_spec((D_MODEL, D_MODEL))]),
        out_specs=wide,
        out_shape=jax.ShapeDtypeStruct((rows, D_MODEL), F32),
        compiler_params=_params(1),
        name="mix_prompt",
    )(attn, yconv, ga, gb, x, *_mix_weights(w), mk, mv, w['w_xo'])


def _memkv_kernel(m_ref, g_ref, w_ref, k5_ref, v5_ref, k_ref, v_ref):
    h = _rms(m_ref[...], g_ref[...]).astype(BF16)
    for hd in range(X_HEADS):
        lo = hd * X_HEAD_DIM
        k = _dot(h, w_ref[:, lo:lo + X_HEAD_DIM])
        v = _dot(h, w_ref[:, D_MODEL + lo:D_MODEL + lo + X_HEAD_DIM])
        k5_ref[0, 0, :, hd, :] = k
        v5_ref[0, 0, :, hd, :] = v
        k_ref[:, lo:lo + X_HEAD_DIM] = k.astype(BF16)
        v_ref[:, lo:lo + X_HEAD_DIM] = v.astype(BF16)


def _memkv(mem, norm_g, w_xkv, *, batch):
    spec5 = pl.BlockSpec((1, 1, N_MEM, X_HEADS, X_HEAD_DIM), lambda b: (0, b, 0, 0, 0))
    rows = _rows_spec(N_MEM, D_MODEL)
    return pl.pallas_call(
        _memkv_kernel,
        grid=(batch,),
        in_specs=[rows, _const_spec((1, D_MODEL)), _const_spec((D_MODEL, 2 * D_MODEL))],
        out_specs=[spec5, spec5, rows, rows],
        out_shape=([jax.ShapeDtypeStruct((1, batch, N_MEM, X_HEADS, X_HEAD_DIM), F32)] * 2
                   + [jax.ShapeDtypeStruct((batch * N_MEM, D_MODEL), BF16)] * 2),
        compiler_params=_params(1),
        name="memory_kv",
    )(mem, norm_g, w_xkv)


def _xattn_cached_kernel(q_ref, k_lo, k_hi, v_lo, v_hi, mask_ref, o_ref, *, n_q, n_seq):
    for b in range(n_seq):
        flat = lambda ref: ref[0, b].reshape(N_MEM * X_HEADS, LANES)
        k_all = jnp.concatenate([flat(k_lo), flat(k_hi)], axis=1).astype(BF16)
        v_all = jnp.concatenate([flat(v_lo), flat(v_hi)], axis=1).astype(BF16)
        q = q_ref[b].astype(F32)
        qs = jnp.concatenate([q[:, h * X_HEAD_DIM:(h + 1) * X_HEAD_DIM] for h in range(X_HEADS)],
                             axis=0).astype(BF16)
        s = _dot_nt(qs, k_all) + mask_ref[...]
        m = jnp.max(s, axis=-1, keepdims=True)
        p = jnp.exp(s - m)
        den = jnp.sum(p, axis=-1, keepdims=True)
        acc = _dot(p.astype(BF16), v_all) * (1.0 / den)
        o_ref[b] = jnp.concatenate([acc[h * n_q:(h + 1) * n_q] for h in range(X_HEADS)], axis=1).astype(BF16)


def _xattn_cached(q, mk, mv, *, batch, seqs_per_step=2):
    n_q = q.shape[1]
    n_seq = math.gcd(batch, seqs_per_step)
    q_spec = pl.BlockSpec((n_seq, n_q, D_MODEL), lambda b: (b, 0, 0))
    halves = [pl.BlockSpec((1, n_seq, N_MEM, X_HEADS, LANES), lambda b, hf=hf: (0, b, 0, 0, hf))
              for hf in range(X_HEAD_DIM // LANES)]
    own_head = (np.arange(N_MEM * X_HEADS)[None, :] % X_HEADS) == (np.arange(X_HEADS * n_q)[:, None] // n_q)
    mask = jnp.asarray(np.where(own_head, 0.0, NEG_INF), F32)
    return pl.pallas_call(
        functools.partial(_xattn_cached_kernel, n_q=n_q, n_seq=n_seq),
        grid=(batch // n_seq,),
        in_specs=[q_spec] + halves * 2 + [_const_spec(mask.shape)],
        out_specs=q_spec,
        out_shape=jax.ShapeDtypeStruct((batch, n_q, D_MODEL), BF16),
        compiler_params=_params(1),
        name="cross_attn_cached",
    )(q, mk, mk, mv, mv, mask)


def _to_positions_heads(a_t, batch, n_pos):
    return jnp.transpose(a_t.reshape(batch, HEADS_PER_GROUP, HEAD_DIM, n_pos), (0, 3, 1, 2))[None]


def _to_feature_major(a, batch, n_pos):
    return jnp.transpose(a[0], (0, 2, 3, 1)).reshape(batch, GROUP_WIDTH, n_pos)


def kernel(x_prompt, x_sample, mem_prompt, cache_win0_k, cache_win0_v, cache_win1_k, cache_win1_v, cache_win2_k, cache_win2_v, state_conv, cache_mem_k, cache_mem_v, rel_bias, ffn1_norm, ffn1_w_gate, ffn1_w_up, ffn1_w_down, mix_norm, w_in, w_attn_proj, conv_dw_w, conv_dw_b, conv_ln_g, conv_ln_b, w_conv_proj, w_o, xattn_norm, mem_norm, w_xq, w_xkv, w_xo, ffn2_norm, ffn2_w_gate, ffn2_w_up, ffn2_w_down, final_norm):
    batch, seq_len, _ = x_prompt.shape
    dec_batch, dec_seq, _ = x_sample.shape
    n_g = len(GROUPS)
    assert ffn1_norm.shape[0] == 1, "single layer"
    mat = lambda a: a[0].astype(BF16)
    vec = lambda a: a[0].reshape(1, -1)
    w = dict(ffn1_norm=vec(ffn1_norm), ffn1_w_gate=mat(ffn1_w_gate), ffn1_w_up=mat(ffn1_w_up),
             ffn1_w_down=mat(ffn1_w_down), mix_norm=vec(mix_norm), w_in=mat(w_in),
             w_attn_proj=mat(w_attn_proj), conv_dw_w=conv_dw_w[0], conv_dw_b=vec(conv_dw_b),
             conv_ln_g=vec(conv_ln_g), conv_ln_b=vec(conv_ln_b), w_conv_proj=mat(w_conv_proj),
             w_o=mat(w_o), xattn_norm=vec(xattn_norm), mem_norm=vec(mem_norm), w_xq=mat(w_xq),
             w_xkv=mat(w_xkv), w_xo=mat(w_xo), ffn2_norm=vec(ffn2_norm), ffn2_w_gate=mat(ffn2_w_gate),
             ffn2_w_up=mat(ffn2_w_up), ffn2_w_down=mat(ffn2_w_down))
    final_g = final_norm.reshape(1, -1)
    tap_biases = [_tap_bias(rel_bias, g, dil) for g, (_, dil) in enumerate(GROUPS)]

    def head(x, tm, rows_per_seq, t_groups, conv_w=None):
        x1 = _ffn(x, w['ffn1_norm'], w['ffn1_w_gate'], w['ffn1_w_up'], w['ffn1_w_down'], tm=tm)
        outs = _win(x1, w['mix_norm'], w['w_in'], tm=tm, rows_per_seq=rows_per_seq, t_groups=t_groups, conv_w=conv_w)
        n_t = 2 * len(t_groups)
        return x1, outs[0], outs[1], outs[2], outs[3], outs[4], outs[5:5 + n_t], outs[5 + n_t:]

    rows_p = batch * seq_len
    tm_p = 512
    full_groups = [g for g, (window, _) in enumerate(GROUPS) if window >= seq_len]
    x1, q, k, v, ga, gb, kv_t, (yconv, u_tail) = head(x_prompt.reshape(rows_p, D_MODEL), tm_p, seq_len, full_groups,
                                                      conv_w=w['conv_dw_w'])
    attn = _band_attn(q, k, v, jnp.stack([_band_bias(t) for t in tap_biases]), batch=batch, seq_len=seq_len)
    p_mem_k, p_mem_v, mk_p, mv_p = _memkv(mem_prompt.reshape(batch * N_MEM, D_MODEL), w['mem_norm'], w['w_xkv'],
                                          batch=batch)
    x3 = _mix_prompt(attn, yconv, ga, gb, x1, mk_p, mv_p, w, tm=tm_p, seq_len=seq_len)
    y_prompt = _ffn(x3, w['ffn2_norm'], w['ffn2_w_gate'], w['ffn2_w_up'], w['ffn2_w_down'], tm=tm_p,
                    final_g=final_g)

    p_win = []
    for g, (window, _) in enumerate(GROUPS):
        keep = min(window, seq_len)
        if g in full_groups:
            j = full_groups.index(g)
            p_win += [_to_positions_heads(kv_t[2 * j], batch, seq_len), _to_positions_heads(kv_t[2 * j + 1], batch, seq_len)]
        else:
            for a in (k[g], v[g]):
                p_win.append(a.reshape(batch, seq_len, GROUP_WIDTH)[:, seq_len - keep:]
                             .reshape(1, batch, keep, HEADS_PER_GROUP, HEAD_DIM))
    p_conv = u_tail[:, CONV_HALO - (CONV_WIDTH - 1):][None]

    rows_s = dec_batch * dec_seq
    x1, q, k, v, ga, gb, kv_t, (u,) = head(x_sample.reshape(rows_s, D_MODEL), rows_s, rows_s, list(range(n_g)))
    caches_k = [_to_feature_major(c, dec_batch, c.shape[2]) for c in (cache_win0_k, cache_win1_k, cache_win2_k)]
    caches_v = [_to_feature_major(c, dec_batch, c.shape[2]) for c in (cache_win0_v, cache_win1_v, cache_win2_v)]
    new_k, new_v, attn = _cache_attn(list(q), list(k), list(v), kv_t[0::2], kv_t[1::2], caches_k, caches_v,
                                     tap_biases, batch=dec_batch, n_new=dec_seq)
    yconv, s_conv = _conv_sample(u, state_conv[0], w['conv_dw_w'], batch=dec_batch, n_new=dec_seq)
    x2, xq = _mix_sample(attn, yconv, ga, gb, x1, w, tm=rows_s)
    xo = _xattn_cached(xq.reshape(dec_batch, dec_seq, D_MODEL), cache_mem_k, cache_mem_v, batch=dec_batch)
    y_sample = _ffn(x2, w['ffn2_norm'], w['ffn2_w_gate'], w['ffn2_w_up'], w['ffn2_w_down'], tm=rows_s,
                    pre=(xo.reshape(rows_s, D_MODEL), w['w_xo']), final_g=final_g)

    s_win = []
    for g in range(n_g):
        for a in (new_k[g], new_v[g]):
            s_win.append(_to_positions_heads(a, dec_batch, a.shape[2]))

    return (y_prompt.reshape(batch, seq_len, D_MODEL), y_sample.reshape(dec_batch, dec_seq, D_MODEL),
            *p_win, p_conv, p_mem_k, p_mem_v, *s_win, s_conv[None])
```

```python
import functools
import math

import jax
import jax.numpy as jnp
import numpy as np
from jax import lax
from jax.experimental import pallas as pl
from jax.experimental.pallas import tpu as pltpu

D_MODEL = 1024
HEAD_DIM = 64
GROUPS = ((128, 1), (512, 4), (2048, 16))
HEADS_PER_GROUP = 4
GROUP_WIDTH = HEADS_PER_GROUP * HEAD_DIM
ATTN_WIDTH = len(GROUPS) * GROUP_WIDTH
CONV_DIM = D_MODEL // 2
CONV_WIDTH = 31
CONV_HALO = 32
N_BUCKETS = 32
MAX_DISTANCE = 2048
D_FF = ((8 * D_MODEL // 3 + 127) // 128) * 128
N_MEM = 256
X_HEADS = 4
X_HEAD_DIM = D_MODEL // X_HEADS
EPS = 1e-6
NEG_INF = -1e30
SPAN = 128
LANES = 128
SUBLANES = 8

V7X_VMEM_LIMIT_BYTES = 56 * 1024 * 1024
BF16 = jnp.bfloat16
F32 = jnp.float32


def _params(n_axes):
    return pltpu.CompilerParams(dimension_semantics=("parallel",) * n_axes,
                                vmem_limit_bytes=V7X_VMEM_LIMIT_BYTES)


def _const_spec(shape):
    return pl.BlockSpec(shape, lambda *_: (0,) * len(shape), pipeline_mode=pl.Buffered(1))


def _rows_spec(tm, width):
    return pl.BlockSpec((tm, width), lambda i: (i, 0))


def _rms(x, g):
    return x * lax.rsqrt(jnp.mean(x * x, axis=-1, keepdims=True) + EPS) * g


def _dot(a, b):
    return jnp.dot(a, b, preferred_element_type=F32)


def _dot_nt(a, b):
    return lax.dot_general(a, b, (((1,), (1,)), ((), ())), preferred_element_type=F32)


def _head_masks(width=GROUP_WIDTH):
    lane = lax.broadcasted_iota(jnp.int32, (1, width), 1)
    return [(lane >= h * HEAD_DIM) & (lane < (h + 1) * HEAD_DIM) for h in range(HEADS_PER_GROUP)]


def _ffn_kernel(*refs, pre_proj, final_norm):
    refs = list(refs)
    x_ref = refs.pop(0)
    if pre_proj:
        a_ref, wp_ref = refs.pop(0), refs.pop(0)
    g_ref, wg_ref, wu_ref, wd_ref = refs.pop(0), refs.pop(0), refs.pop(0), refs.pop(0)
    if final_norm:
        fg_ref = refs.pop(0)
    o_ref = refs.pop(0)

    x = x_ref[...]
    if pre_proj:
        x = x + _dot(a_ref[...], wp_ref[...])
    h = _rms(x, g_ref[...]).astype(BF16)
    gate = _dot(h, wg_ref[...])
    up = _dot(h, wu_ref[...])
    act = (gate * jax.nn.sigmoid(gate) * up).astype(BF16)
    x = x + 0.5 * _dot(act, wd_ref[...])
    if final_norm:
        x = _rms(x, fg_ref[...])
    o_ref[...] = x


def _ffn(x, norm_g, wg, wu, wd, *, tm, pre=None, final_g=None):
    rows = x.shape[0]
    args = [x]
    specs = [_rows_spec(tm, D_MODEL)]
    if pre is not None:
        a, wp = pre
        args += [a, wp]
        specs += [_rows_spec(tm, D_MODEL), _const_spec((D_MODEL, D_MODEL))]
    args += [norm_g, wg, wu, wd]
    specs += [_const_spec((1, D_MODEL)), _const_spec((D_MODEL, D_FF)), _const_spec((D_MODEL, D_FF)),
              _const_spec((D_FF, D_MODEL))]
    if final_g is not None:
        args.append(final_g)
        specs.append(_const_spec((1, D_MODEL)))
    return pl.pallas_call(
        functools.partial(_ffn_kernel, pre_proj=pre is not None, final_norm=final_g is not None),
        grid=(rows // tm,),
        in_specs=specs,
        out_specs=_rows_spec(tm, D_MODEL),
        out_shape=jax.ShapeDtypeStruct((rows, D_MODEL), F32),
        compiler_params=_params(1),
        name="ffn",
    )(*args)


_Q_END = ATTN_WIDTH
_K_END = 2 * ATTN_WIDTH
_V_END = 3 * ATTN_WIDTH
_UA_END = _V_END + CONV_DIM
_UB_END = _UA_END + CONV_DIM
_GA_END = _UB_END + D_MODEL
IN_WIDTH = _GA_END + D_MODEL


def _win_kernel(*refs, t_groups, fuse_conv, tm, tiles_per_seq):
    x_ref, g_ref, w_ref = refs[:3]
    cw_ref = refs[3] if fuse_conv else None
    outs = refs[4:] if fuse_conv else refs[3:]
    q_ref, k_ref, v_ref, ga_ref, gb_ref = outs[:5]
    n_t = 2 * len(t_groups)
    t_refs = outs[5:5 + n_t]
    h = _rms(x_ref[...], g_ref[...]).astype(BF16)

    def seg(lo, hi):
        return _dot(h, w_ref[:, lo:hi])

    u = seg(_V_END, _UA_END) * jax.nn.sigmoid(seg(_UA_END, _UB_END))
    if fuse_conv:
        yc_ref, tail_ref, ext_ref = outs[5 + n_t:]
        first_tile = pl.program_id(0) % tiles_per_seq == 0

        @pl.when(first_tile)
        def _():
            ext_ref[0:CONV_HALO, :] = jnp.zeros((CONV_HALO, CONV_DIM), F32)

        @pl.when(jnp.logical_not(first_tile))
        def _():
            ext_ref[0:CONV_HALO, :] = ext_ref[tm:tm + CONV_HALO, :]

        ext_ref[CONV_HALO:CONV_HALO + tm, :] = u
        ext_ref[CONV_HALO + tm:CONV_HALO + tm + SUBLANES, :] = jnp.zeros((SUBLANES, CONV_DIM), F32)
        _conv_taps_aligned(ext_ref, cw_ref, yc_ref, CONV_HALO - (CONV_WIDTH - 1), tm)
        tail_ref[0] = u[tm - CONV_HALO:tm]
    else:
        outs[5 + n_t][...] = u
    for g in range(len(GROUPS)):
        lo = g * GROUP_WIDTH
        q = seg(lo, lo + GROUP_WIDTH) * (HEAD_DIM ** -0.5)
        k = seg(_Q_END + lo, _Q_END + lo + GROUP_WIDTH)
        v = seg(_K_END + lo, _K_END + lo + GROUP_WIDTH)
        for hf in range(GROUP_WIDTH // LANES):
            lanes = slice(hf * LANES, (hf + 1) * LANES)
            q_ref[g, hf] = q[:, lanes]
            k_ref[g, hf] = k[:, lanes]
            v_ref[g, hf] = v[:, lanes]
        if g in t_groups:
            j = t_groups.index(g)
            t_refs[2 * j][0] = k.T
            t_refs[2 * j + 1][0] = v.T
    ga_ref[...] = jax.nn.sigmoid(seg(_UB_END, _GA_END)).astype(BF16)
    gb_ref[...] = jax.nn.sigmoid(seg(_GA_END, IN_WIDTH)).astype(BF16)


def _win(x, norm_g, w_in, *, tm, rows_per_seq, t_groups, conv_w=None):
    rows = x.shape[0]
    n_seq = rows // rows_per_seq
    n_t = 2 * len(t_groups)
    tiles_per_seq = rows_per_seq // tm
    fuse_conv = conv_w is not None
    n_g = len(GROUPS)
    n_hf = GROUP_WIDTH // LANES
    out_shape = ([jax.ShapeDtypeStruct((n_g, n_hf, rows, LANES), F32)] * 3
                 + [jax.ShapeDtypeStruct((rows, D_MODEL), BF16)] * 2
                 + [jax.ShapeDtypeStruct((n_seq, GROUP_WIDTH, rows_per_seq), F32)] * n_t
                 + [jax.ShapeDtypeStruct((rows, CONV_DIM), F32)])
    t_spec = pl.BlockSpec((1, GROUP_WIDTH, tm), lambda i: (i // tiles_per_seq, 0, i % tiles_per_seq))
    out_specs = ([pl.BlockSpec((n_g, n_hf, tm, LANES), lambda i: (0, 0, i, 0))] * 3
                 + [_rows_spec(tm, D_MODEL)] * 2 + [t_spec] * n_t
                 + [_rows_spec(tm, CONV_DIM)])
    in_specs = [_rows_spec(tm, D_MODEL), _const_spec((1, D_MODEL)), _const_spec((D_MODEL, IN_WIDTH))]
    args = [x, norm_g, w_in]
    scratch = []
    if fuse_conv:
        in_specs.append(_const_spec((CONV_WIDTH, CONV_DIM)))
        args.append(conv_w)
        out_shape.append(jax.ShapeDtypeStruct((n_seq, CONV_HALO, CONV_DIM), F32))
        out_specs.append(pl.BlockSpec((1, CONV_HALO, CONV_DIM), lambda i: (i // tiles_per_seq, 0, 0)))
        scratch.append(pltpu.VMEM((CONV_HALO + tm + SUBLANES, CONV_DIM), F32))
    return pl.pallas_call(
        functools.partial(_win_kernel, t_groups=tuple(t_groups), fuse_conv=fuse_conv, tm=tm,
                          tiles_per_seq=tiles_per_seq),
        grid=(rows // tm,),
        in_specs=in_specs,
        out_specs=out_specs,
        out_shape=out_shape,
        scratch_shapes=scratch,
        compiler_params=pltpu.CompilerParams(dimension_semantics=("arbitrary",),
                                             vmem_limit_bytes=V7X_VMEM_LIMIT_BYTES),
        name="w_in",
    )(*args)


def _rel_bucket(dist):
    n = jnp.maximum(dist, 0)
    max_exact = N_BUCKETS // 2
    nf = jnp.maximum(n, 1).astype(F32)
    large = max_exact + (jnp.log(nf / max_exact) / math.log(MAX_DISTANCE / max_exact)
                         * (N_BUCKETS - max_exact)).astype(jnp.int32)
    return jnp.where(n < max_exact, n, jnp.minimum(large, N_BUCKETS - 1))


def _tap_bias(rel_bias, g, dil):
    bias_g = rel_bias[:, g * HEADS_PER_GROUP:(g + 1) * HEADS_PER_GROUP]
    return bias_g[_rel_bucket(jnp.arange(SPAN, -1, -1) * dil)].astype(F32).T


def _toeplitz(vec, n_rows, n_cols):
    n_heads, period = vec.shape
    flat = jnp.tile(vec, (1, n_rows))[:, :n_rows * (period - 1)]
    return flat.reshape(n_heads, n_rows, period - 1)[:, :, :n_cols]


def _band_bias(tap_bias):
    vec = jnp.concatenate([tap_bias, jnp.full((HEADS_PER_GROUP, 2 * SPAN - 1), NEG_INF, F32)], axis=1)
    return _toeplitz(vec, SPAN, 2 * SPAN).reshape(HEADS_PER_GROUP * SPAN, 2 * SPAN)


def _cache_bias(tap_bias, dil, cache_len, n_new):
    rev = lax.pad(tap_bias, jnp.float32(NEG_INF), [(0, 0, 0), (cache_len + n_new - 1 - SPAN * dil, 0, dil - 1)])
    bias_c = _toeplitz(jnp.roll(rev, -(n_new - 1), axis=1), n_new, cache_len)
    near = jnp.concatenate([rev[:, cache_len:], jnp.full((HEADS_PER_GROUP, LANES), NEG_INF, F32)], axis=1)
    bias_n = _toeplitz(jnp.roll(near, -(n_new - 1), axis=1), n_new, LANES)
    return (bias_c.reshape(HEADS_PER_GROUP * n_new, cache_len), bias_n.reshape(HEADS_PER_GROUP * n_new, LANES))


MAX_ROW_STRIDE = 4


def _gather_residues(src, store, tmp_ref, dil, seq):
    if dil == 1:
        store(0, src[...])
    elif dil <= MAX_ROW_STRIDE:
        for r in range(dil):
            store(r, src[pl.ds(r, seq, stride=dil), :])
    else:
        inner = dil // MAX_ROW_STRIDE
        assert inner <= MAX_ROW_STRIDE and inner * MAX_ROW_STRIDE == dil
        n_part = seq * inner
        for c in range(MAX_ROW_STRIDE):
            tmp_ref[c * n_part:(c + 1) * n_part, :] = src[pl.ds(c, n_part, stride=MAX_ROW_STRIDE), :]
        for c in range(MAX_ROW_STRIDE):
            for c2 in range(inner):
                store(c + MAX_ROW_STRIDE * c2, tmp_ref[pl.ds(c * n_part + c2, seq, stride=inner), :])


def _scatter_residues(load, dst, tmp_ref, dil, seq):
    if dil == 1:
        dst[...] = load(0)
    elif dil <= MAX_ROW_STRIDE:
        for r in range(dil):
            dst[pl.ds(r, seq, stride=dil), :] = load(r)
    else:
        inner = dil // MAX_ROW_STRIDE
        assert inner <= MAX_ROW_STRIDE and inner * MAX_ROW_STRIDE == dil
        n_part = seq * inner
        for c in range(MAX_ROW_STRIDE):
            for c2 in range(inner):
                tmp_ref[pl.ds(c * n_part + c2, seq, stride=inner), :] = load(c + MAX_ROW_STRIDE * c2)
        for c in range(MAX_ROW_STRIDE):
            dst[pl.ds(c, n_part, stride=MAX_ROW_STRIDE), :] = tmp_ref[c * n_part:(c + 1) * n_part, :]


def _merge_groups(outs, lses):
    m = functools.reduce(jnp.maximum, lses)
    es = [jnp.exp(l - m) for l in lses]
    num = sum(e * o for e, o in zip(es, outs))
    return num * (1.0 / sum(es))


def _band_group(q_lo, q_hi, k_lo, k_hi, v_lo, v_hi, b_ref, o_dst, l_dst, qb_ref, kb_ref, vb_ref, ob_ref, lb_ref,
                tmp_ref, *, dil, seq):
    n_blk = seq // SPAN
    halves = (slice(0, LANES), slice(LANES, 2 * LANES))
    for dst, srcs in ((qb_ref, (q_lo, q_hi)), (kb_ref, (k_lo, k_hi)), (vb_ref, (v_lo, v_hi))):
        for lanes, src in zip(halves, srcs):
            def store(r, rows, dst=dst, lanes=lanes):
                dst[r * seq:(r + 1) * seq, lanes] = rows.astype(BF16)
            _gather_residues(src, store, tmp_ref, dil, seq)
    masks = _head_masks()

    def block(q, kb, vb, key_lo):
        qm = jnp.concatenate([jnp.where(masks[h], q, jnp.zeros_like(q)) for h in range(HEADS_PER_GROUP)], axis=0)
        s = _dot_nt(qm, kb) + b_ref[:, key_lo:]
        m = jnp.max(s, axis=-1, keepdims=True)
        p = jnp.exp(s - m)
        den = jnp.sum(p, axis=-1, keepdims=True)
        acc = _dot(p.astype(BF16), vb) * (1.0 / den)
        lse_rows = m + jnp.log(den)
        o = acc[0:SPAN]
        lse = jnp.broadcast_to(lse_rows[0:SPAN], (SPAN, GROUP_WIDTH))
        for h in range(1, HEADS_PER_GROUP):
            o = jnp.where(masks[h], acc[h * SPAN:(h + 1) * SPAN], o)
            lse = jnp.where(masks[h], lse_rows[h * SPAN:(h + 1) * SPAN], lse)
        return o, lse

    for r in range(dil):
        for j in range(n_blk):
            q_rows = slice(r * seq + j * SPAN, r * seq + (j + 1) * SPAN)
            k_rows = slice(r * seq + max(j - 1, 0) * SPAN, r * seq + (j + 1) * SPAN)
            o, lse = block(qb_ref[q_rows, :], kb_ref[k_rows, :], vb_ref[k_rows, :], SPAN if j == 0 else 0)
            ob_ref[q_rows, :] = o
            lb_ref[q_rows, :] = lse

    for src, dsts in ((ob_ref, o_dst), (lb_ref, l_dst)):
        for lanes, dst in zip(halves, dsts):
            _scatter_residues(lambda r, src=src, lanes=lanes: src[r * seq:(r + 1) * seq, lanes], dst, tmp_ref, dil, seq)


MERGE_CHUNK_ROWS = 256


def _band_attn_kernel(q_lo, q_hi, k_lo, k_hi, v_lo, v_hi, b_ref, out_ref,
                      qb_ref, kb_ref, vb_ref, ob_ref, lb_ref, tmp_ref, *nat_refs, seq_len):
    n_half = 2 * len(GROUPS)
    on_refs, ln_refs = nat_refs[:n_half], nat_refs[n_half:]
    g = pl.program_id(1)
    for gi, (_, dil) in enumerate(GROUPS):
        @pl.when(g == gi)
        def _(gi=gi, dil=dil):
            _band_group(q_lo, q_hi, k_lo, k_hi, v_lo, v_hi, b_ref,
                        on_refs[2 * gi:2 * gi + 2], ln_refs[2 * gi:2 * gi + 2],
                        qb_ref, kb_ref, vb_ref, ob_ref, lb_ref, tmp_ref, dil=dil, seq=seq_len // dil)

    @pl.when(g == len(GROUPS) - 1)
    def _():
        for r0 in range(0, seq_len, MERGE_CHUNK_ROWS):
            rows = slice(r0, r0 + MERGE_CHUNK_ROWS)
            for hf in (0, 1):
                merged = _merge_groups([on_refs[2 * gi + hf][rows, :] for gi in range(len(GROUPS))],
                                       [ln_refs[2 * gi + hf][rows, :] for gi in range(len(GROUPS))])
                out_ref[rows, hf * LANES:(hf + 1) * LANES] = merged.astype(BF16)


def _band_attn(q, k, v, biases, *, batch, seq_len):
    n_g = len(GROUPS)
    view = lambda a: a.reshape(n_g, 2, batch, seq_len, LANES)
    in_halves = [pl.BlockSpec((None, None, None, seq_len, LANES), lambda b, g, hf=hf: (g, hf, b, 0, 0))
                 for hf in (0, 1)]
    out = pl.pallas_call(
        functools.partial(_band_attn_kernel, seq_len=seq_len),
        grid=(batch, n_g),
        in_specs=in_halves * 3 + [pl.BlockSpec((None, HEADS_PER_GROUP * SPAN, 2 * SPAN), lambda b, g: (g, 0, 0))],
        out_specs=pl.BlockSpec((None, seq_len, GROUP_WIDTH), lambda b, g: (b, 0, 0)),
        out_shape=jax.ShapeDtypeStruct((batch, seq_len, GROUP_WIDTH), BF16),
        scratch_shapes=([pltpu.VMEM((seq_len, GROUP_WIDTH), BF16)] * 3 + [pltpu.VMEM((seq_len, GROUP_WIDTH), F32)] * 2
                        + [pltpu.VMEM((seq_len, LANES), F32)]
                        + [pltpu.VMEM((seq_len, LANES), F32)] * (4 * n_g)),
        compiler_params=pltpu.CompilerParams(dimension_semantics=("parallel", "arbitrary"),
                                             vmem_limit_bytes=V7X_VMEM_LIMIT_BYTES),
        name="band_attn",
    )(view(q), view(q), view(k), view(k), view(v), view(v), biases)
    return out.reshape(batch * seq_len, GROUP_WIDTH)


def _cache_attn_kernel(*refs, n_new):
    n_g = len(GROUPS)
    q_refs, kn_refs, vn_refs = refs[0:n_g], refs[n_g:2 * n_g], refs[2 * n_g:3 * n_g]
    knt_refs, vnt_refs = refs[3 * n_g:4 * n_g], refs[4 * n_g:5 * n_g]
    kc_refs, vc_refs = refs[5 * n_g:6 * n_g], refs[6 * n_g:7 * n_g]
    bc_refs, bn_refs = refs[7 * n_g:8 * n_g], refs[8 * n_g:9 * n_g]
    outs = refs[9 * n_g:]
    ko_refs, vo_refs = outs[0:n_g], outs[n_g:2 * n_g]
    attn_ref = outs[2 * n_g]
    group_out, group_lse = [], []
    masks = _head_masks()
    pad = jnp.zeros((SPAN - n_new, GROUP_WIDTH), F32)
    lane = lax.broadcasted_iota(jnp.int32, (1, LANES), 1)
    j = pl.program_id(0) % (LANES // n_new)
    to_tail = LANES - n_new - n_new * j

    for g in range(n_g):
        kc, vc = kc_refs[g][0], vc_refs[g][0]
        kn, vn = kn_refs[g][0], vn_refs[g][0]
        cache_len = kc.shape[1]
        for src, new_t, dst in ((kc, knt_refs[g][0], ko_refs[g]), (vc, vnt_refs[g][0], vo_refs[g])):
            shifted = pltpu.roll(src, cache_len - n_new, axis=1)
            new_tail = pltpu.roll(new_t, to_tail, axis=1)
            if cache_len > LANES:
                dst[0, :, 0:cache_len - LANES] = shifted[:, 0:cache_len - LANES]
            dst[0, :, cache_len - LANES:cache_len] = jnp.where(lane >= LANES - n_new, new_tail,
                                                              shifted[:, cache_len - LANES:cache_len])
        q = q_refs[g][0]
        qm = jnp.concatenate([jnp.where(masks[h], q, 0.0) for h in range(HEADS_PER_GROUP)], axis=0).astype(BF16)
        kn_p = jnp.concatenate([kn, pad], axis=0).astype(BF16)
        vn_p = jnp.concatenate([vn, pad], axis=0).astype(BF16)
        s_c = _dot(qm, kc.astype(BF16)) + bc_refs[g][...]
        s_n = _dot_nt(qm, kn_p) + bn_refs[g][...]
        m = jnp.maximum(jnp.max(s_c, axis=-1, keepdims=True), jnp.max(s_n, axis=-1, keepdims=True))
        p_c = jnp.exp(s_c - m)
        p_n = jnp.exp(s_n - m)
        den = jnp.sum(p_c, axis=-1, keepdims=True) + jnp.sum(p_n, axis=-1, keepdims=True)
        acc = _dot_nt(p_c.astype(BF16), vc.astype(BF16)) + _dot(p_n.astype(BF16), vn_p)
        acc = acc * (1.0 / den)
        lse_rows = m + jnp.log(den)
        o = jnp.zeros((n_new, GROUP_WIDTH), F32)
        lse = jnp.zeros((n_new, GROUP_WIDTH), F32)
        for h in range(HEADS_PER_GROUP):
            rows = slice(h * n_new, (h + 1) * n_new)
            o = jnp.where(masks[h], acc[rows], o)
            lse = jnp.where(masks[h], lse_rows[rows], lse)
        group_out.append(o)
        group_lse.append(lse)
    attn_ref[0] = _merge_groups(group_out, group_lse).astype(BF16)


def _cache_attn(q, k_new, v_new, k_new_t, v_new_t, k_cache, v_cache, tap_biases, *, batch, n_new):
    n_g = len(GROUPS)
    per_tile = LANES // n_new
    new3 = lambda a: a.reshape(batch, n_new, GROUP_WIDTH)
    new_spec = pl.BlockSpec((1, n_new, GROUP_WIDTH), lambda b: (b, 0, 0))
    new_t_spec = pl.BlockSpec((1, GROUP_WIDTH, LANES), lambda b: (0, 0, b // per_tile))
    cache_specs = [pl.BlockSpec((1, GROUP_WIDTH, c.shape[2]), lambda b: (b, 0, 0)) for c in k_cache]
    biases_c, biases_n = [], []
    for g, (_, dil) in enumerate(GROUPS):
        bc, bn = _cache_bias(tap_biases[g], dil, k_cache[g].shape[2], n_new)
        biases_c.append(bc)
        biases_n.append(bn)
    args = ([new3(a) for a in q] + [new3(a) for a in k_new] + [new3(a) for a in v_new]
            + list(k_new_t) + list(v_new_t) + list(k_cache) + list(v_cache) + biases_c + biases_n)
    in_specs = ([new_spec] * (3 * n_g) + [new_t_spec] * (2 * n_g) + cache_specs * 2
                + [_const_spec(b.shape) for b in biases_c] + [_const_spec(b.shape) for b in biases_n])
    out_shape = ([jax.ShapeDtypeStruct(c.shape, F32) for c in k_cache] * 2
                 + [jax.ShapeDtypeStruct((batch, n_new, GROUP_WIDTH), BF16)])
    out_specs = cache_specs * 2 + [new_spec]
    outs = pl.pallas_call(
        functools.partial(_cache_attn_kernel, n_new=n_new),
        grid=(batch,),
        in_specs=in_specs,
        out_specs=out_specs,
        out_shape=out_shape,
        compiler_params=_params(1),
        name="cache_attn",
    )(*args)
    return outs[0:n_g], outs[n_g:2 * n_g], outs[2 * n_g].reshape(batch * n_new, GROUP_WIDTH)


def _conv_taps(ext_ref, w_ref, first_row, n_rows):
    acc = jnp.zeros((n_rows, CONV_DIM), F32)
    for j in range(CONV_WIDTH):
        acc = acc + ext_ref[pl.ds(first_row + j, n_rows), :] * w_ref[j:j + 1, :]
    return acc


CONV_CHUNK_ROWS = 64


def _conv_taps_aligned(ext_ref, w_ref, y_ref, first_row, n_rows):
    for r0 in range(0, n_rows, CONV_CHUNK_ROWS):
        for l0 in range(0, CONV_DIM, LANES):
            lanes = slice(l0, l0 + LANES)
            acc = None
            for b in range(SUBLANES):
                part = None
                for a in range((first_row + CONV_WIDTH - 1 - b) // SUBLANES + 1):
                    j = SUBLANES * a + b - first_row
                    if j < 0:
                        continue
                    term = ext_ref[pl.ds(r0 + SUBLANES * a, CONV_CHUNK_ROWS + SUBLANES), lanes] * w_ref[j:j + 1, lanes]
                    part = term if part is None else part + term
                part = part[b:b + CONV_CHUNK_ROWS]
                acc = part if acc is None else acc + part
            y_ref[r0:r0 + CONV_CHUNK_ROWS, lanes] = acc


def _conv_sample_kernel(u_ref, st_ref, w_ref, y_ref, so_ref, ext_ref, *, n_new, n_seq):
    hist = CONV_WIDTH - 1
    for b in range(n_seq):
        ext_ref[b, 0:hist, :] = st_ref[b]
        ext_ref[b, hist:hist + n_new, :] = u_ref[b]
        y_ref[b] = _conv_taps(ext_ref.at[b], w_ref, 0, n_new)
        so_ref[b] = ext_ref[b, n_new:n_new + hist, :]


def _conv_sample(u, state, w, *, batch, n_new, seqs_per_step=8):
    hist = CONV_WIDTH - 1
    u3 = u.reshape(batch, n_new, CONV_DIM)
    n_seq = math.gcd(batch, seqs_per_step)
    y, st = pl.pallas_call(
        functools.partial(_conv_sample_kernel, n_new=n_new, n_seq=n_seq),
        grid=(batch // n_seq,),
        in_specs=[pl.BlockSpec((n_seq, n_new, CONV_DIM), lambda b: (b, 0, 0)),
                  pl.BlockSpec((n_seq, hist, CONV_DIM), lambda b: (b, 0, 0)),
                  pl.BlockSpec((CONV_WIDTH, CONV_DIM), lambda b: (0, 0))],
        out_specs=[pl.BlockSpec((n_seq, n_new, CONV_DIM), lambda b: (b, 0, 0)),
                   pl.BlockSpec((n_seq, hist, CONV_DIM), lambda b: (b, 0, 0))],
        out_shape=[jax.ShapeDtypeStruct((batch, n_new, CONV_DIM), F32),
                   jax.ShapeDtypeStruct((batch, hist, CONV_DIM), F32)],
        scratch_shapes=[pltpu.VMEM((n_seq, hist + n_new + 2, CONV_DIM), F32)],
        compiler_params=_params(1),
        name="conv_sample",
    )(u3, state, w)
    return y.reshape(batch * n_new, CONV_DIM), st


def _mix_branches(attn, yconv, gate_a, gate_b, x, wap_ref, cb_ref, lg_ref, lb_ref, wcp_ref, wo_ref):
    a_branch = _dot(attn, wap_ref[...])
    y = yconv + cb_ref[...]
    mu = jnp.mean(y, axis=-1, keepdims=True)
    yc = y - mu
    var = jnp.mean(yc * yc, axis=-1, keepdims=True)
    y = yc * lax.rsqrt(var + EPS) * lg_ref[...] + lb_ref[...]
    y = y * jax.nn.sigmoid(y)
    c_branch = _dot(y.astype(BF16), wcp_ref[...])
    merged = gate_a * a_branch + gate_b * c_branch
    return x + _dot(merged.astype(BF16), wo_ref[...])


def _xattn_query(x, xg_ref, wxq_ref):
    hq = _rms(x, xg_ref[...]).astype(BF16)
    return (_dot(hq, wxq_ref[...]) * (X_HEAD_DIM ** -0.5)).astype(BF16)


def _xattn_heads(q, k_ref, v_ref):
    outs = []
    for h in range(X_HEADS):
        lanes = slice(h * X_HEAD_DIM, (h + 1) * X_HEAD_DIM)
        s = _dot_nt(q[:, lanes], k_ref[:, lanes])
        m = jnp.max(s, axis=-1, keepdims=True)
        p = jnp.exp(s - m)
        den = jnp.sum(p, axis=-1, keepdims=True)
        outs.append((_dot(p.astype(BF16), v_ref[:, lanes]) * (1.0 / den)).astype(BF16))
    return jnp.concatenate(outs, axis=-1)


def _mix_sample_kernel(at_ref, yc_ref, ga_ref, gb_ref, x_ref, wap_ref, cb_ref, lg_ref, lb_ref, wcp_ref, wo_ref,
                       xg_ref, wxq_ref, x_out, q_out):
    x = _mix_branches(at_ref[...], yc_ref[...], ga_ref[...], gb_ref[...], x_ref[...],
                      wap_ref, cb_ref, lg_ref, lb_ref, wcp_ref, wo_ref)
    x_out[...] = x
    q_out[...] = _xattn_query(x, xg_ref, wxq_ref)


def _mix_prompt_kernel(at_ref, yc_ref, ga_ref, gb_ref, x_ref, wap_ref, cb_ref, lg_ref, lb_ref, wcp_ref, wo_ref,
                       xg_ref, wxq_ref, mk_ref, mv_ref, wxo_ref, x_out):
    x = _mix_branches(at_ref[...], yc_ref[...], ga_ref[...], gb_ref[...], x_ref[...],
                      wap_ref, cb_ref, lg_ref, lb_ref, wcp_ref, wo_ref)
    attn = _xattn_heads(_xattn_query(x, xg_ref, wxq_ref), mk_ref, mv_ref)
    x_out[...] = x + _dot(attn, wxo_ref[...])


def _mix_weight_specs():
    return [_const_spec((GROUP_WIDTH, D_MODEL)), _const_spec((1, CONV_DIM)), _const_spec((1, CONV_DIM)),
            _const_spec((1, CONV_DIM)), _const_spec((CONV_DIM, D_MODEL)), _const_spec((D_MODEL, D_MODEL)),
            _const_spec((1, D_MODEL)), _const_spec((D_MODEL, D_MODEL))]


def _mix_weights(w):
    return (w['w_attn_proj'], w['conv_dw_b'], w['conv_ln_g'], w['conv_ln_b'], w['w_conv_proj'], w['w_o'],
            w['xattn_norm'], w['w_xq'])


def _mix_sample(attn, yconv, ga, gb, x, w, *, tm):
    rows = x.shape[0]
    wide = _rows_spec(tm, D_MODEL)
    return pl.pallas_call(
        _mix_sample_kernel,
        grid=(rows // tm,),
        in_specs=[_rows_spec(tm, GROUP_WIDTH), _rows_spec(tm, CONV_DIM), wide, wide, wide] + _mix_weight_specs(),
        out_specs=[wide, wide],
        out_shape=[jax.ShapeDtypeStruct((rows, D_MODEL), F32), jax.ShapeDtypeStruct((rows, D_MODEL), BF16)],
        compiler_params=_params(1),
        name="mix_sample",
    )(attn, yconv, ga, gb, x, *_mix_weights(w))


def _mix_prompt(attn, yconv, ga, gb, x, mk, mv, w, *, tm, seq_len):
    rows = x.shape[0]
    tiles_per_seq = seq_len // tm
    wide = _rows_spec(tm, D_MODEL)
    mem = pl.BlockSpec((N_MEM, D_MODEL), lambda i: (i // tiles_per_seq, 0))
    return pl.pallas_call(
        _mix_prompt_kernel,
        grid=(rows // tm,),
        in_specs=([_rows_spec(tm, GROUP_WIDTH), _rows_spec(tm, CONV_DIM), wide, wide, wide]
                  + _mix_weight_specs() + [mem, mem, _const_spec((D_MODEL, D_MODEL))]),
        out_specs=wide,
        out_shape=jax.ShapeDtypeStruct((rows, D_MODEL), F32),
        compiler_params=_params(1),
        name="mix_prompt",
    )(attn, yconv, ga, gb, x, *_mix_weights(w), mk, mv, w['w_xo'])


def _memkv_kernel(m_ref, g_ref, w_ref, k5_ref, v5_ref, k_ref, v_ref):
    h = _rms(m_ref[...], g_ref[...]).astype(BF16)
    for hd in range(X_HEADS):
        lo = hd * X_HEAD_DIM
        k = _dot(h, w_ref[:, lo:lo + X_HEAD_DIM])
        v = _dot(h, w_ref[:, D_MODEL + lo:D_MODEL + lo + X_HEAD_DIM])
        k5_ref[0, 0, :, hd, :] = k
        v5_ref[0, 0, :, hd, :] = v
        k_ref[:, lo:lo + X_HEAD_DIM] = k.astype(BF16)
        v_ref[:, lo:lo + X_HEAD_DIM] = v.astype(BF16)


def _memkv(mem, norm_g, w_xkv, *, batch):
    spec5 = pl.BlockSpec((1, 1, N_MEM, X_HEADS, X_HEAD_DIM), lambda b: (0, b, 0, 0, 0))
    rows = _rows_spec(N_MEM, D_MODEL)
    return pl.pallas_call(
        _memkv_kernel,
        grid=(batch,),
        in_specs=[rows, _const_spec((1, D_MODEL)), _const_spec((D_MODEL, 2 * D_MODEL))],
        out_specs=[spec5, spec5, rows, rows],
        out_shape=([jax.ShapeDtypeStruct((1, batch, N_MEM, X_HEADS, X_HEAD_DIM), F32)] * 2
                   + [jax.ShapeDtypeStruct((batch * N_MEM, D_MODEL), BF16)] * 2),
        compiler_params=_params(1),
        name="memory_kv",
    )(mem, norm_g, w_xkv)


def _xattn_cached_kernel(q_ref, k_lo, k_hi, v_lo, v_hi, mask_ref, o_ref, *, n_q, n_seq):
    for b in range(n_seq):
        flat = lambda ref: ref[0, b].reshape(N_MEM * X_HEADS, LANES)
        k_all = jnp.concatenate([flat(k_lo), flat(k_hi)], axis=1).astype(BF16)
        v_all = jnp.concatenate([flat(v_lo), flat(v_hi)], axis=1).astype(BF16)
        q = q_ref[b].astype(F32)
        qs = jnp.concatenate([q[:, h * X_HEAD_DIM:(h + 1) * X_HEAD_DIM] for h in range(X_HEADS)],
                             axis=0).astype(BF16)
        s = _dot_nt(qs, k_all) + mask_ref[...]
        m = jnp.max(s, axis=-1, keepdims=True)
        p = jnp.exp(s - m)
        den = jnp.sum(p, axis=-1, keepdims=True)
        acc = _dot(p.astype(BF16), v_all) * (1.0 / den)
        o_ref[b] = jnp.concatenate([acc[h * n_q:(h + 1) * n_q] for h in range(X_HEADS)], axis=1).astype(BF16)


def _xattn_cached(q, mk, mv, *, batch, seqs_per_step=2):
    n_q = q.shape[1]
    n_seq = math.gcd(batch, seqs_per_step)
    q_spec = pl.BlockSpec((n_seq, n_q, D_MODEL), lambda b: (b, 0, 0))
    halves = [pl.BlockSpec((1, n_seq, N_MEM, X_HEADS, LANES), lambda b, hf=hf: (0, b, 0, 0, hf))
              for hf in range(X_HEAD_DIM // LANES)]
    own_head = (np.arange(N_MEM * X_HEADS)[None, :] % X_HEADS) == (np.arange(X_HEADS * n_q)[:, None] // n_q)
    mask = jnp.asarray(np.where(own_head, 0.0, NEG_INF), F32)
    return pl.pallas_call(
        functools.partial(_xattn_cached_kernel, n_q=n_q, n_seq=n_seq),
        grid=(batch // n_seq,),
        in_specs=[q_spec] + halves * 2 + [_const_spec(mask.shape)],
        out_specs=q_spec,
        out_shape=jax.ShapeDtypeStruct((batch, n_q, D_MODEL), BF16),
        compiler_params=_params(1),
        name="cross_attn_cached",
    )(q, mk, mk, mv, mv, mask)


def _to_positions_heads(a_t, batch, n_pos):
    return jnp.transpose(a_t.reshape(batch, HEADS_PER_GROUP, HEAD_DIM, n_pos), (0, 3, 1, 2))[None]


def _to_feature_major(a, batch, n_pos):
    return jnp.transpose(a[0], (0, 2, 3, 1)).reshape(batch, GROUP_WIDTH, n_pos)


def kernel(x_prompt, x_sample, mem_prompt, cache_win0_k, cache_win0_v, cache_win1_k, cache_win1_v, cache_win2_k, cache_win2_v, state_conv, cache_mem_k, cache_mem_v, rel_bias, ffn1_norm, ffn1_w_gate, ffn1_w_up, ffn1_w_down, mix_norm, w_in, w_attn_proj, conv_dw_w, conv_dw_b, conv_ln_g, conv_ln_b, w_conv_proj, w_o, xattn_norm, mem_norm, w_xq, w_xkv, w_xo, ffn2_norm, ffn2_w_gate, ffn2_w_up, ffn2_w_down, final_norm):
    batch, seq_len, _ = x_prompt.shape
    dec_batch, dec_seq, _ = x_sample.shape
    n_g = len(GROUPS)
    assert ffn1_norm.shape[0] == 1, "single layer"
    mat = lambda a: a[0].astype(BF16)
    vec = lambda a: a[0].reshape(1, -1)
    w = dict(ffn1_norm=vec(ffn1_norm), ffn1_w_gate=mat(ffn1_w_gate), ffn1_w_up=mat(ffn1_w_up),
             ffn1_w_down=mat(ffn1_w_down), mix_norm=vec(mix_norm), w_in=mat(w_in),
             w_attn_proj=mat(w_attn_proj), conv_dw_w=conv_dw_w[0], conv_dw_b=vec(conv_dw_b),
             conv_ln_g=vec(conv_ln_g), conv_ln_b=vec(conv_ln_b), w_conv_proj=mat(w_conv_proj),
             w_o=mat(w_o), xattn_norm=vec(xattn_norm), mem_norm=vec(mem_norm), w_xq=mat(w_xq),
             w_xkv=mat(w_xkv), w_xo=mat(w_xo), ffn2_norm=vec(ffn2_norm), ffn2_w_gate=mat(ffn2_w_gate),
             ffn2_w_up=mat(ffn2_w_up), ffn2_w_down=mat(ffn2_w_down))
    final_g = final_norm.reshape(1, -1)
    tap_biases = [_tap_bias(rel_bias, g, dil) for g, (_, dil) in enumerate(GROUPS)]

    def head(x, tm, rows_per_seq, t_groups, conv_w=None):
        x1 = _ffn(x, w['ffn1_norm'], w['ffn1_w_gate'], w['ffn1_w_up'], w['ffn1_w_down'], tm=tm)
        outs = _win(x1, w['mix_norm'], w['w_in'], tm=tm, rows_per_seq=rows_per_seq, t_groups=t_groups, conv_w=conv_w)
        n_t = 2 * len(t_groups)
        return x1, outs[0], outs[1], outs[2], outs[3], outs[4], outs[5:5 + n_t], outs[5 + n_t:]

    rows_p = batch * seq_len
    tm_p = 512
    full_groups = [g for g, (window, _) in enumerate(GROUPS) if window >= seq_len]
    x1, q, k, v, ga, gb, kv_t, (yconv, u_tail) = head(x_prompt.reshape(rows_p, D_MODEL), tm_p, seq_len, full_groups,
                                                      conv_w=w['conv_dw_w'])
    attn = _band_attn(q, k, v, jnp.stack([_band_bias(t) for t in tap_biases]), batch=batch, seq_len=seq_len)
    p_mem_k, p_mem_v, mk_p, mv_p = _memkv(mem_prompt.reshape(batch * N_MEM, D_MODEL), w['mem_norm'], w['w_xkv'],
                                          batch=batch)
    x3 = _mix_prompt(attn, yconv, ga, gb, x1, mk_p, mv_p, w, tm=tm_p, seq_len=seq_len)
    y_prompt = _ffn(x3, w['ffn2_norm'], w['ffn2_w_gate'], w['ffn2_w_up'], w['ffn2_w_down'], tm=tm_p,
                    final_g=final_g)

    p_win = []
    for g, (window, _) in enumerate(GROUPS):
        keep = min(window, seq_len)
        if g in full_groups:
            j = full_groups.index(g)
            p_win += [_to_positions_heads(kv_t[2 * j], batch, seq_len), _to_positions_heads(kv_t[2 * j + 1], batch, seq_len)]
        else:
            for a in (k, v):
                tail = lax.slice(a.reshape(n_g, 2, batch, seq_len, LANES), (g, 0, 0, seq_len - keep, 0),
                                 (g + 1, 2, batch, seq_len, LANES))
                tail = jnp.transpose(tail[0], (1, 2, 0, 3))
                p_win.append(tail.reshape(1, batch, keep, HEADS_PER_GROUP, HEAD_DIM))
    p_conv = u_tail[:, CONV_HALO - (CONV_WIDTH - 1):][None]

    rows_s = dec_batch * dec_seq
    x1, q, k, v, ga, gb, kv_t, (u,) = head(x_sample.reshape(rows_s, D_MODEL), rows_s, rows_s, list(range(n_g)))
    caches_k = [_to_feature_major(c, dec_batch, c.shape[2]) for c in (cache_win0_k, cache_win1_k, cache_win2_k)]
    caches_v = [_to_feature_major(c, dec_batch, c.shape[2]) for c in (cache_win0_v, cache_win1_v, cache_win2_v)]
    groups = lambda a: [jnp.concatenate([a[g, 0], a[g, 1]], axis=-1) for g in range(n_g)]
    new_k, new_v, attn = _cache_attn(groups(q), groups(k), groups(v), kv_t[0::2], kv_t[1::2], caches_k, caches_v,
                                     tap_biases, batch=dec_batch, n_new=dec_seq)
    yconv, s_conv = _conv_sample(u, state_conv[0], w['conv_dw_w'], batch=dec_batch, n_new=dec_seq)
    x2, xq = _mix_sample(attn, yconv, ga, gb, x1, w, tm=rows_s)
    xo = _xattn_cached(xq.reshape(dec_batch, dec_seq, D_MODEL), cache_mem_k, cache_mem_v, batch=dec_batch)
    y_sample = _ffn(x2, w['ffn2_norm'], w['ffn2_w_gate'], w['ffn2_w_up'], w['ffn2_w_down'], tm=rows_s,
                    pre=(xo.reshape(rows_s, D_MODEL), w['w_xo']), final_g=final_g)

    s_win = []
    for g in range(n_g):
        for a in (new_k[g], new_v[g]):
            s_win.append(_to_positions_heads(a, dec_batch, a.shape[2]))

    return (y_prompt.reshape(batch, seq_len, D_MODEL), y_sample.reshape(dec_batch, dec_seq, D_MODEL),
            *p_win, p_conv, p_mem_k, p_mem_v, *s_win, s_conv[None])
```

```python
import functools
import math

import jax
import jax.numpy as jnp
import numpy as np
from jax import lax
from jax.experimental import pallas as pl
from jax.experimental.pallas import tpu as pltpu

D_MODEL = 1024
HEAD_DIM = 64
GROUPS = ((128, 1), (512, 4), (2048, 16))
HEADS_PER_GROUP = 4
GROUP_WIDTH = HEADS_PER_GROUP * HEAD_DIM
ATTN_WIDTH = len(GROUPS) * GROUP_WIDTH
CONV_DIM = D_MODEL // 2
CONV_WIDTH = 31
CONV_HALO = 32
N_BUCKETS = 32
MAX_DISTANCE = 2048
D_FF = ((8 * D_MODEL // 3 + 127) // 128) * 128
N_MEM = 256
X_HEADS = 4
X_HEAD_DIM = D_MODEL // X_HEADS
EPS = 1e-6
NEG_INF = -1e30
SPAN = 128
LANES = 128
SUBLANES = 8

V7X_VMEM_LIMIT_BYTES = 56 * 1024 * 1024
BF16 = jnp.bfloat16
F32 = jnp.float32


def _params(n_axes):
    return pltpu.CompilerParams(dimension_semantics=("parallel",) * n_axes,
                                vmem_limit_bytes=V7X_VMEM_LIMIT_BYTES)


def _const_spec(shape):
    return pl.BlockSpec(shape, lambda *_: (0,) * len(shape), pipeline_mode=pl.Buffered(1))


def _rows_spec(tm, width):
    return pl.BlockSpec((tm, width), lambda i: (i, 0))


def _rms(x, g):
    return x * lax.rsqrt(jnp.mean(x * x, axis=-1, keepdims=True) + EPS) * g


def _dot(a, b):
    return jnp.dot(a, b, preferred_element_type=F32)


def _dot_nt(a, b):
    return lax.dot_general(a, b, (((1,), (1,)), ((), ())), preferred_element_type=F32)


def _head_masks(width=GROUP_WIDTH):
    lane = lax.broadcasted_iota(jnp.int32, (1, width), 1)
    return [(lane >= h * HEAD_DIM) & (lane < (h + 1) * HEAD_DIM) for h in range(HEADS_PER_GROUP)]


def _ffn_kernel(*refs, pre_proj, final_norm):
    refs = list(refs)
    x_ref = refs.pop(0)
    if pre_proj:
        a_ref, wp_ref = refs.pop(0), refs.pop(0)
    g_ref, wg_ref, wu_ref, wd_ref = refs.pop(0), refs.pop(0), refs.pop(0), refs.pop(0)
    if final_norm:
        fg_ref = refs.pop(0)
    o_ref = refs.pop(0)

    x = x_ref[...]
    if pre_proj:
        x = x + _dot(a_ref[...], wp_ref[...])
    h = _rms(x, g_ref[...]).astype(BF16)
    gate = _dot(h, wg_ref[...])
    up = _dot(h, wu_ref[...])
    act = (gate * jax.nn.sigmoid(gate) * up).astype(BF16)
    x = x + 0.5 * _dot(act, wd_ref[...])
    if final_norm:
        x = _rms(x, fg_ref[...])
    o_ref[...] = x


def _ffn(x, norm_g, wg, wu, wd, *, tm, pre=None, final_g=None):
    rows = x.shape[0]
    args = [x]
    specs = [_rows_spec(tm, D_MODEL)]
    if pre is not None:
        a, wp = pre
        args += [a, wp]
        specs += [_rows_spec(tm, D_MODEL), _const_spec((D_MODEL, D_MODEL))]
    args += [norm_g, wg, wu, wd]
    specs += [_const_spec((1, D_MODEL)), _const_spec((D_MODEL, D_FF)), _const_spec((D_MODEL, D_FF)),
              _const_spec((D_FF, D_MODEL))]
    if final_g is not None:
        args.append(final_g)
        specs.append(_const_spec((1, D_MODEL)))
    return pl.pallas_call(
        functools.partial(_ffn_kernel, pre_proj=pre is not None, final_norm=final_g is not None),
        grid=(rows // tm,),
        in_specs=specs,
        out_specs=_rows_spec(tm, D_MODEL),
        out_shape=jax.ShapeDtypeStruct((rows, D_MODEL), F32),
        compiler_params=_params(1),
        name="ffn",
    )(*args)


_Q_END = ATTN_WIDTH
_K_END = 2 * ATTN_WIDTH
_V_END = 3 * ATTN_WIDTH
_UA_END = _V_END + CONV_DIM
_UB_END = _UA_END + CONV_DIM
_GA_END = _UB_END + D_MODEL
IN_WIDTH = _GA_END + D_MODEL


def _win_kernel(*refs, t_groups, fuse_conv, tm, tiles_per_seq):
    x_ref, g_ref, w_ref = refs[:3]
    cw_ref = refs[3] if fuse_conv else None
    outs = refs[4:] if fuse_conv else refs[3:]
    q_ref, k_ref, v_ref, ga_ref, gb_ref = outs[:5]
    n_t = 2 * len(t_groups)
    t_refs = outs[5:5 + n_t]
    h = _rms(x_ref[...], g_ref[...]).astype(BF16)

    def seg(lo, hi):
        return _dot(h, w_ref[:, lo:hi])

    u = seg(_V_END, _UA_END) * jax.nn.sigmoid(seg(_UA_END, _UB_END))
    if fuse_conv:
        yc_ref, tail_ref, ext_ref = outs[5 + n_t:]
        first_tile = pl.program_id(0) % tiles_per_seq == 0

        @pl.when(first_tile)
        def _():
            ext_ref[0:CONV_HALO, :] = jnp.zeros((CONV_HALO, CONV_DIM), F32)

        @pl.when(jnp.logical_not(first_tile))
        def _():
            ext_ref[0:CONV_HALO, :] = ext_ref[tm:tm + CONV_HALO, :]

        ext_ref[CONV_HALO:CONV_HALO + tm, :] = u
        ext_ref[CONV_HALO + tm:CONV_HALO + tm + SUBLANES, :] = jnp.zeros((SUBLANES, CONV_DIM), F32)
        tail_ref[0] = u[tm - CONV_HALO:tm]
        conv_pieces = [(r0, l0) for r0 in range(0, tm, CONV_CHUNK_ROWS) for l0 in range(0, CONV_DIM, LANES)]
    else:
        outs[5 + n_t][...] = u
        conv_pieces = []
    n_proj_left = [3 * len(GROUPS) + 2 * (D_MODEL // GROUP_WIDTH)]

    def conv_step():
        n = -(-len(conv_pieces) // n_proj_left[0])
        n_proj_left[0] -= 1
        if n:
            _conv_taps_aligned(ext_ref, cw_ref, yc_ref, CONV_HALO - (CONV_WIDTH - 1), conv_pieces[:n])
            del conv_pieces[:n]

    for g in range(len(GROUPS)):
        lo = g * GROUP_WIDTH
        q = seg(lo, lo + GROUP_WIDTH) * (HEAD_DIM ** -0.5)
        conv_step()
        k = seg(_Q_END + lo, _Q_END + lo + GROUP_WIDTH)
        conv_step()
        v = seg(_K_END + lo, _K_END + lo + GROUP_WIDTH)
        conv_step()
        for hf in range(GROUP_WIDTH // LANES):
            lanes = slice(hf * LANES, (hf + 1) * LANES)
            q_ref[g, hf] = q[:, lanes]
            k_ref[g, hf] = k[:, lanes]
            v_ref[g, hf] = v[:, lanes]
        if g in t_groups:
            j = t_groups.index(g)
            t_refs[2 * j][0] = k.T
            t_refs[2 * j + 1][0] = v.T
    for gate_ref, first in ((ga_ref, _UB_END), (gb_ref, _GA_END)):
        for c in range(0, D_MODEL, GROUP_WIDTH):
            gate_ref[:, c:c + GROUP_WIDTH] = jax.nn.sigmoid(seg(first + c, first + c + GROUP_WIDTH)).astype(BF16)
            conv_step()


def _win(x, norm_g, w_in, *, tm, rows_per_seq, t_groups, conv_w=None):
    rows = x.shape[0]
    n_seq = rows // rows_per_seq
    n_t = 2 * len(t_groups)
    tiles_per_seq = rows_per_seq // tm
    fuse_conv = conv_w is not None
    n_g = len(GROUPS)
    n_hf = GROUP_WIDTH // LANES
    out_shape = ([jax.ShapeDtypeStruct((n_g, n_hf, rows, LANES), F32)] * 3
                 + [jax.ShapeDtypeStruct((rows, D_MODEL), BF16)] * 2
                 + [jax.ShapeDtypeStruct((n_seq, GROUP_WIDTH, rows_per_seq), F32)] * n_t
                 + [jax.ShapeDtypeStruct((rows, CONV_DIM), F32)])
    t_spec = pl.BlockSpec((1, GROUP_WIDTH, tm), lambda i: (i // tiles_per_seq, 0, i % tiles_per_seq))
    out_specs = ([pl.BlockSpec((n_g, n_hf, tm, LANES), lambda i: (0, 0, i, 0))] * 3
                 + [_rows_spec(tm, D_MODEL)] * 2 + [t_spec] * n_t
                 + [_rows_spec(tm, CONV_DIM)])
    in_specs = [_rows_spec(tm, D_MODEL), _const_spec((1, D_MODEL)), _const_spec((D_MODEL, IN_WIDTH))]
    args = [x, norm_g, w_in]
    scratch = []
    if fuse_conv:
        in_specs.append(_const_spec((CONV_WIDTH, CONV_DIM)))
        args.append(conv_w)
        out_shape.append(jax.ShapeDtypeStruct((n_seq, CONV_HALO, CONV_DIM), F32))
        out_specs.append(pl.BlockSpec((1, CONV_HALO, CONV_DIM), lambda i: (i // tiles_per_seq, 0, 0)))
        scratch.append(pltpu.VMEM((CONV_HALO + tm + SUBLANES, CONV_DIM), F32))
    return pl.pallas_call(
        functools.partial(_win_kernel, t_groups=tuple(t_groups), fuse_conv=fuse_conv, tm=tm,
                          tiles_per_seq=tiles_per_seq),
        grid=(rows // tm,),
        in_specs=in_specs,
        out_specs=out_specs,
        out_shape=out_shape,
        scratch_shapes=scratch,
        compiler_params=pltpu.CompilerParams(dimension_semantics=("arbitrary",),
                                             vmem_limit_bytes=V7X_VMEM_LIMIT_BYTES),
        name="w_in",
    )(*args)


def _rel_bucket(dist):
    n = jnp.maximum(dist, 0)
    max_exact = N_BUCKETS // 2
    nf = jnp.maximum(n, 1).astype(F32)
    large = max_exact + (jnp.log(nf / max_exact) / math.log(MAX_DISTANCE / max_exact)
                         * (N_BUCKETS - max_exact)).astype(jnp.int32)
    return jnp.where(n < max_exact, n, jnp.minimum(large, N_BUCKETS - 1))


def _tap_bias(rel_bias, g, dil):
    bias_g = rel_bias[:, g * HEADS_PER_GROUP:(g + 1) * HEADS_PER_GROUP]
    return bias_g[_rel_bucket(jnp.arange(SPAN, -1, -1) * dil)].astype(F32).T


def _toeplitz(vec, n_rows, n_cols):
    n_heads, period = vec.shape
    flat = jnp.tile(vec, (1, n_rows))[:, :n_rows * (period - 1)]
    return flat.reshape(n_heads, n_rows, period - 1)[:, :, :n_cols]


def _band_bias(tap_bias):
    vec = jnp.concatenate([tap_bias, jnp.full((HEADS_PER_GROUP, 2 * SPAN - 1), NEG_INF, F32)], axis=1)
    return _toeplitz(vec, SPAN, 2 * SPAN).reshape(HEADS_PER_GROUP * SPAN, 2 * SPAN)


def _cache_bias(tap_bias, dil, cache_len, n_new):
    rev = lax.pad(tap_bias, jnp.float32(NEG_INF), [(0, 0, 0), (cache_len + n_new - 1 - SPAN * dil, 0, dil - 1)])
    bias_c = _toeplitz(jnp.roll(rev, -(n_new - 1), axis=1), n_new, cache_len)
    near = jnp.concatenate([rev[:, cache_len:], jnp.full((HEADS_PER_GROUP, LANES), NEG_INF, F32)], axis=1)
    bias_n = _toeplitz(jnp.roll(near, -(n_new - 1), axis=1), n_new, LANES)
    return (bias_c.reshape(HEADS_PER_GROUP * n_new, cache_len), bias_n.reshape(HEADS_PER_GROUP * n_new, LANES))


MAX_ROW_STRIDE = 4


def _gather_residues(src, store, tmp_ref, dil, seq):
    if dil == 1:
        store(0, src[...])
    elif dil <= MAX_ROW_STRIDE:
        for r in range(dil):
            store(r, src[pl.ds(r, seq, stride=dil), :])
    else:
        inner = dil // MAX_ROW_STRIDE
        assert inner <= MAX_ROW_STRIDE and inner * MAX_ROW_STRIDE == dil
        n_part = seq * inner
        for c in range(MAX_ROW_STRIDE):
            tmp_ref[c * n_part:(c + 1) * n_part, :] = src[pl.ds(c, n_part, stride=MAX_ROW_STRIDE), :]
        for c in range(MAX_ROW_STRIDE):
            for c2 in range(inner):
                store(c + MAX_ROW_STRIDE * c2, tmp_ref[pl.ds(c * n_part + c2, seq, stride=inner), :])


def _scatter_residues(load, dst, tmp_ref, dil, seq):
    if dil == 1:
        dst[...] = load(0)
    elif dil <= MAX_ROW_STRIDE:
        for r in range(dil):
            dst[pl.ds(r, seq, stride=dil), :] = load(r)
    else:
        inner = dil // MAX_ROW_STRIDE
        assert inner <= MAX_ROW_STRIDE and inner * MAX_ROW_STRIDE == dil
        n_part = seq * inner
        for c in range(MAX_ROW_STRIDE):
            for c2 in range(inner):
                tmp_ref[pl.ds(c * n_part + c2, seq, stride=inner), :] = load(c + MAX_ROW_STRIDE * c2)
        for c in range(MAX_ROW_STRIDE):
            dst[pl.ds(c, n_part, stride=MAX_ROW_STRIDE), :] = tmp_ref[c * n_part:(c + 1) * n_part, :]


def _merge_groups(outs, lses):
    m = functools.reduce(jnp.maximum, lses)
    es = [jnp.exp(l - m) for l in lses]
    num = sum(e * o for e, o in zip(es, outs))
    return num * (1.0 / sum(es))


def _band_group(q_lo, q_hi, k_lo, k_hi, v_lo, v_hi, b_ref, o_dst, l_dst, qb_ref, kb_ref, vb_ref, ob_ref, lb_ref,
                tmp_ref, *, dil, seq):
    n_blk = seq // SPAN
    halves = (slice(0, LANES), slice(LANES, 2 * LANES))
    for dst, srcs in ((qb_ref, (q_lo, q_hi)), (kb_ref, (k_lo, k_hi)), (vb_ref, (v_lo, v_hi))):
        for lanes, src in zip(halves, srcs):
            def store(r, rows, dst=dst, lanes=lanes):
                dst[r * seq:(r + 1) * seq, lanes] = rows.astype(BF16)
            _gather_residues(src, store, tmp_ref, dil, seq)
    masks = _head_masks()

    def block(q, kb, vb, key_lo):
        qm = jnp.concatenate([jnp.where(masks[h], q, jnp.zeros_like(q)) for h in range(HEADS_PER_GROUP)], axis=0)
        s = _dot_nt(qm, kb) + b_ref[:, key_lo:]
        m = jnp.max(s, axis=-1, keepdims=True)
        p = jnp.exp(s - m)
        den = jnp.sum(p, axis=-1, keepdims=True)
        acc = _dot(p.astype(BF16), vb) * (1.0 / den)
        lse_rows = m + jnp.log(den)
        o = acc[0:SPAN]
        lse = jnp.broadcast_to(lse_rows[0:SPAN], (SPAN, GROUP_WIDTH))
        for h in range(1, HEADS_PER_GROUP):
            o = jnp.where(masks[h], acc[h * SPAN:(h + 1) * SPAN], o)
            lse = jnp.where(masks[h], lse_rows[h * SPAN:(h + 1) * SPAN], lse)
        return o, lse

    for r in range(dil):
        for j in range(n_blk):
            q_rows = slice(r * seq + j * SPAN, r * seq + (j + 1) * SPAN)
            k_rows = slice(r * seq + max(j - 1, 0) * SPAN, r * seq + (j + 1) * SPAN)
            o, lse = block(qb_ref[q_rows, :], kb_ref[k_rows, :], vb_ref[k_rows, :], SPAN if j == 0 else 0)
            ob_ref[q_rows, :] = o
            lb_ref[q_rows, :] = lse

    for src, dsts in ((ob_ref, o_dst), (lb_ref, l_dst)):
        for lanes, dst in zip(halves, dsts):
            _scatter_residues(lambda r, src=src, lanes=lanes: src[r * seq:(r + 1) * seq, lanes], dst, tmp_ref, dil, seq)


MERGE_CHUNK_ROWS = 256


def _band_attn_kernel(q_lo, q_hi, k_lo, k_hi, v_lo, v_hi, b_ref, out_ref,
                      qb_ref, kb_ref, vb_ref, ob_ref, lb_ref, tmp_ref, *nat_refs, seq_len):
    n_half = 2 * len(GROUPS)
    on_refs, ln_refs = nat_refs[:n_half], nat_refs[n_half:]
    g = pl.program_id(1)
    for gi, (_, dil) in enumerate(GROUPS):
        @pl.when(g == gi)
        def _(gi=gi, dil=dil):
            _band_group(q_lo, q_hi, k_lo, k_hi, v_lo, v_hi, b_ref,
                        on_refs[2 * gi:2 * gi + 2], ln_refs[2 * gi:2 * gi + 2],
                        qb_ref, kb_ref, vb_ref, ob_ref, lb_ref, tmp_ref, dil=dil, seq=seq_len // dil)

    @pl.when(g == len(GROUPS) - 1)
    def _():
        for r0 in range(0, seq_len, MERGE_CHUNK_ROWS):
            rows = slice(r0, r0 + MERGE_CHUNK_ROWS)
            for hf in (0, 1):
                merged = _merge_groups([on_refs[2 * gi + hf][rows, :] for gi in range(len(GROUPS))],
                                       [ln_refs[2 * gi + hf][rows, :] for gi in range(len(GROUPS))])
                out_ref[rows, hf * LANES:(hf + 1) * LANES] = merged.astype(BF16)


def _band_attn(q, k, v, biases, *, batch, seq_len):
    n_g = len(GROUPS)
    view = lambda a: a.reshape(n_g, 2, batch, seq_len, LANES)
    in_halves = [pl.BlockSpec((None, None, None, seq_len, LANES), lambda b, g, hf=hf: (g, hf, b, 0, 0))
                 for hf in (0, 1)]
    out = pl.pallas_call(
        functools.partial(_band_attn_kernel, seq_len=seq_len),
        grid=(batch, n_g),
        in_specs=in_halves * 3 + [pl.BlockSpec((None, HEADS_PER_GROUP * SPAN, 2 * SPAN), lambda b, g: (g, 0, 0))],
        out_specs=pl.BlockSpec((None, seq_len, GROUP_WIDTH), lambda b, g: (b, 0, 0)),
        out_shape=jax.ShapeDtypeStruct((batch, seq_len, GROUP_WIDTH), BF16),
        scratch_shapes=([pltpu.VMEM((seq_len, GROUP_WIDTH), BF16)] * 3 + [pltpu.VMEM((seq_len, GROUP_WIDTH), F32)] * 2
                        + [pltpu.VMEM((seq_len, LANES), F32)]
                        + [pltpu.VMEM((seq_len, LANES), F32)] * (4 * n_g)),
        compiler_params=pltpu.CompilerParams(dimension_semantics=("parallel", "arbitrary"),
                                             vmem_limit_bytes=V7X_VMEM_LIMIT_BYTES),
        name="band_attn",
    )(view(q), view(q), view(k), view(k), view(v), view(v), biases)
    return out.reshape(batch * seq_len, GROUP_WIDTH)


def _cache_attn_kernel(*refs, n_new):
    n_g = len(GROUPS)
    q_refs, kn_refs, vn_refs = refs[0:n_g], refs[n_g:2 * n_g], refs[2 * n_g:3 * n_g]
    knt_refs, vnt_refs = refs[3 * n_g:4 * n_g], refs[4 * n_g:5 * n_g]
    kc_refs, vc_refs = refs[5 * n_g:6 * n_g], refs[6 * n_g:7 * n_g]
    bc_refs, bn_refs = refs[7 * n_g:8 * n_g], refs[8 * n_g:9 * n_g]
    outs = refs[9 * n_g:]
    ko_refs, vo_refs = outs[0:n_g], outs[n_g:2 * n_g]
    attn_ref = outs[2 * n_g]
    group_out, group_lse = [], []
    masks = _head_masks()
    pad = jnp.zeros((SPAN - n_new, GROUP_WIDTH), F32)
    lane = lax.broadcasted_iota(jnp.int32, (1, LANES), 1)
    j = pl.program_id(0) % (LANES // n_new)
    to_tail = LANES - n_new - n_new * j

    for g in range(n_g):
        kc, vc = kc_refs[g][0], vc_refs[g][0]
        kn, vn = kn_refs[g][0], vn_refs[g][0]
        cache_len = kc.shape[1]
        for src, new_t, dst in ((kc, knt_refs[g][0], ko_refs[g]), (vc, vnt_refs[g][0], vo_refs[g])):
            shifted = pltpu.roll(src, cache_len - n_new, axis=1)
            new_tail = pltpu.roll(new_t, to_tail, axis=1)
            if cache_len > LANES:
                dst[0, :, 0:cache_len - LANES] = shifted[:, 0:cache_len - LANES]
            dst[0, :, cache_len - LANES:cache_len] = jnp.where(lane >= LANES - n_new, new_tail,
                                                              shifted[:, cache_len - LANES:cache_len])
        q = q_refs[g][0]
        qm = jnp.concatenate([jnp.where(masks[h], q, 0.0) for h in range(HEADS_PER_GROUP)], axis=0).astype(BF16)
        kn_p = jnp.concatenate([kn, pad], axis=0).astype(BF16)
        vn_p = jnp.concatenate([vn, pad], axis=0).astype(BF16)
        s_c = _dot(qm, kc.astype(BF16)) + bc_refs[g][...]
        s_n = _dot_nt(qm, kn_p) + bn_refs[g][...]
        m = jnp.maximum(jnp.max(s_c, axis=-1, keepdims=True), jnp.max(s_n, axis=-1, keepdims=True))
        p_c = jnp.exp(s_c - m)
        p_n = jnp.exp(s_n - m)
        den = jnp.sum(p_c, axis=-1, keepdims=True) + jnp.sum(p_n, axis=-1, keepdims=True)
        acc = _dot_nt(p_c.astype(BF16), vc.astype(BF16)) + _dot(p_n.astype(BF16), vn_p)
        acc = acc * (1.0 / den)
        lse_rows = m + jnp.log(den)
        o = jnp.zeros((n_new, GROUP_WIDTH), F32)
        lse = jnp.zeros((n_new, GROUP_WIDTH), F32)
        for h in range(HEADS_PER_GROUP):
            rows = slice(h * n_new, (h + 1) * n_new)
            o = jnp.where(masks[h], acc[rows], o)
            lse = jnp.where(masks[h], lse_rows[rows], lse)
        group_out.append(o)
        group_lse.append(lse)
    attn_ref[0] = _merge_groups(group_out, group_lse).astype(BF16)


def _cache_attn(q, k_new, v_new, k_new_t, v_new_t, k_cache, v_cache, tap_biases, *, batch, n_new):
    n_g = len(GROUPS)
    per_tile = LANES // n_new
    new3 = lambda a: a.reshape(batch, n_new, GROUP_WIDTH)
    new_spec = pl.BlockSpec((1, n_new, GROUP_WIDTH), lambda b: (b, 0, 0))
    new_t_spec = pl.BlockSpec((1, GROUP_WIDTH, LANES), lambda b: (0, 0, b // per_tile))
    cache_specs = [pl.BlockSpec((1, GROUP_WIDTH, c.shape[2]), lambda b: (b, 0, 0)) for c in k_cache]
    biases_c, biases_n = [], []
    for g, (_, dil) in enumerate(GROUPS):
        bc, bn = _cache_bias(tap_biases[g], dil, k_cache[g].shape[2], n_new)
        biases_c.append(bc)
        biases_n.append(bn)
    args = ([new3(a) for a in q] + [new3(a) for a in k_new] + [new3(a) for a in v_new]
            + list(k_new_t) + list(v_new_t) + list(k_cache) + list(v_cache) + biases_c + biases_n)
    in_specs = ([new_spec] * (3 * n_g) + [new_t_spec] * (2 * n_g) + cache_specs * 2
                + [_const_spec(b.shape) for b in biases_c] + [_const_spec(b.shape) for b in biases_n])
    out_shape = ([jax.ShapeDtypeStruct(c.shape, F32) for c in k_cache] * 2
                 + [jax.ShapeDtypeStruct((batch, n_new, GROUP_WIDTH), BF16)])
    out_specs = cache_specs * 2 + [new_spec]
    outs = pl.pallas_call(
        functools.partial(_cache_attn_kernel, n_new=n_new),
        grid=(batch,),
        in_specs=in_specs,
        out_specs=out_specs,
        out_shape=out_shape,
        compiler_params=_params(1),
        name="cache_attn",
    )(*args)
    return outs[0:n_g], outs[n_g:2 * n_g], outs[2 * n_g].reshape(batch * n_new, GROUP_WIDTH)


def _conv_taps(ext_ref, w_ref, first_row, n_rows):
    acc = jnp.zeros((n_rows, CONV_DIM), F32)
    for j in range(CONV_WIDTH):
        acc = acc + ext_ref[pl.ds(first_row + j, n_rows), :] * w_ref[j:j + 1, :]
    return acc


CONV_CHUNK_ROWS = 64


def _conv_taps_aligned(ext_ref, w_ref, y_ref, first_row, pieces):
    for r0, l0 in pieces:
        lanes = slice(l0, l0 + LANES)
        acc = None
        for b in range(SUBLANES):
            part = None
            for a in range((first_row + CONV_WIDTH - 1 - b) // SUBLANES + 1):
                j = SUBLANES * a + b - first_row
                if j < 0:
                    continue
                term = ext_ref[pl.ds(r0 + SUBLANES * a, CONV_CHUNK_ROWS + SUBLANES), lanes] * w_ref[j:j + 1, lanes]
                part = term if part is None else part + term
            part = part[b:b + CONV_CHUNK_ROWS]
            acc = part if acc is None else acc + part
        y_ref[r0:r0 + CONV_CHUNK_ROWS, lanes] = acc


def _conv_sample_kernel(u_ref, st_ref, w_ref, y_ref, so_ref, ext_ref, *, n_new, n_seq):
    hist = CONV_WIDTH - 1
    for b in range(n_seq):
        ext_ref[b, 0:hist, :] = st_ref[b]
        ext_ref[b, hist:hist + n_new, :] = u_ref[b]
        y_ref[b] = _conv_taps(ext_ref.at[b], w_ref, 0, n_new)
        so_ref[b] = ext_ref[b, n_new:n_new + hist, :]


def _conv_sample(u, state, w, *, batch, n_new, seqs_per_step=8):
    hist = CONV_WIDTH - 1
    u3 = u.reshape(batch, n_new, CONV_DIM)
    n_seq = math.gcd(batch, seqs_per_step)
    y, st = pl.pallas_call(
        functools.partial(_conv_sample_kernel, n_new=n_new, n_seq=n_seq),
        grid=(batch // n_seq,),
        in_specs=[pl.BlockSpec((n_seq, n_new, CONV_DIM), lambda b: (b, 0, 0)),
                  pl.BlockSpec((n_seq, hist, CONV_DIM), lambda b: (b, 0, 0)),
                  pl.BlockSpec((CONV_WIDTH, CONV_DIM), lambda b: (0, 0))],
        out_specs=[pl.BlockSpec((n_seq, n_new, CONV_DIM), lambda b: (b, 0, 0)),
                   pl.BlockSpec((n_seq, hist, CONV_DIM), lambda b: (b, 0, 0))],
        out_shape=[jax.ShapeDtypeStruct((batch, n_new, CONV_DIM), F32),
                   jax.ShapeDtypeStruct((batch, hist, CONV_DIM), F32)],
        scratch_shapes=[pltpu.VMEM((n_seq, hist + n_new + 2, CONV_DIM), F32)],
        compiler_params=_params(1),
        name="conv_sample",
    )(u3, state, w)
    return y.reshape(batch * n_new, CONV_DIM), st


def _mix_branches(attn, yconv, gate_a, gate_b, x, wap_ref, cb_ref, lg_ref, lb_ref, wcp_ref, wo_ref):
    a_branch = _dot(attn, wap_ref[...])
    y = yconv + cb_ref[...]
    mu = jnp.mean(y, axis=-1, keepdims=True)
    yc = y - mu
    var = jnp.mean(yc * yc, axis=-1, keepdims=True)
    y = yc * lax.rsqrt(var + EPS) * lg_ref[...] + lb_ref[...]
    y = y * jax.nn.sigmoid(y)
    c_branch = _dot(y.astype(BF16), wcp_ref[...])
    merged = gate_a * a_branch + gate_b * c_branch
    return x + _dot(merged.astype(BF16), wo_ref[...])


def _xattn_query(x, xg_ref, wxq_ref):
    hq = _rms(x, xg_ref[...]).astype(BF16)
    return (_dot(hq, wxq_ref[...]) * (X_HEAD_DIM ** -0.5)).astype(BF16)


def _xattn_heads(q, k_ref, v_ref):
    outs = []
    for h in range(X_HEADS):
        lanes = slice(h * X_HEAD_DIM, (h + 1) * X_HEAD_DIM)
        s = _dot_nt(q[:, lanes], k_ref[:, lanes])
        m = jnp.max(s, axis=-1, keepdims=True)
        p = jnp.exp(s - m)
        den = jnp.sum(p, axis=-1, keepdims=True)
        outs.append((_dot(p.astype(BF16), v_ref[:, lanes]) * (1.0 / den)).astype(BF16))
    return jnp.concatenate(outs, axis=-1)


def _mix_sample_kernel(at_ref, yc_ref, ga_ref, gb_ref, x_ref, wap_ref, cb_ref, lg_ref, lb_ref, wcp_ref, wo_ref,
                       xg_ref, wxq_ref, x_out, q_out):
    x = _mix_branches(at_ref[...], yc_ref[...], ga_ref[...], gb_ref[...], x_ref[...],
                      wap_ref, cb_ref, lg_ref, lb_ref, wcp_ref, wo_ref)
    x_out[...] = x
    q_out[...] = _xattn_query(x, xg_ref, wxq_ref)


def _mix_prompt_kernel(at_ref, yc_ref, ga_ref, gb_ref, x_ref, wap_ref, cb_ref, lg_ref, lb_ref, wcp_ref, wo_ref,
                       xg_ref, wxq_ref, mk_ref, mv_ref, wxo_ref, x_out):
    x = _mix_branches(at_ref[...], yc_ref[...], ga_ref[...], gb_ref[...], x_ref[...],
                      wap_ref, cb_ref, lg_ref, lb_ref, wcp_ref, wo_ref)
    attn = _xattn_heads(_xattn_query(x, xg_ref, wxq_ref), mk_ref, mv_ref)
    x_out[...] = x + _dot(attn, wxo_ref[...])


def _mix_weight_specs():
    return [_const_spec((GROUP_WIDTH, D_MODEL)), _const_spec((1, CONV_DIM)), _const_spec((1, CONV_DIM)),
            _const_spec((1, CONV_DIM)), _const_spec((CONV_DIM, D_MODEL)), _const_spec((D_MODEL, D_MODEL)),
            _const_spec((1, D_MODEL)), _const_spec((D_MODEL, D_MODEL))]


def _mix_weights(w):
    return (w['w_attn_proj'], w['conv_dw_b'], w['conv_ln_g'], w['conv_ln_b'], w['w_conv_proj'], w['w_o'],
            w['xattn_norm'], w['w_xq'])


def _mix_sample(attn, yconv, ga, gb, x, w, *, tm):
    rows = x.shape[0]
    wide = _rows_spec(tm, D_MODEL)
    return pl.pallas_call(
        _mix_sample_kernel,
        grid=(rows // tm,),
        in_specs=[_rows_spec(tm, GROUP_WIDTH), _rows_spec(tm, CONV_DIM), wide, wide, wide] + _mix_weight_specs(),
        out_specs=[wide, wide],
        out_shape=[jax.ShapeDtypeStruct((rows, D_MODEL), F32), jax.ShapeDtypeStruct((rows, D_MODEL), BF16)],
        compiler_params=_params(1),
        name="mix_sample",
    )(attn, yconv, ga, gb, x, *_mix_weights(w))


def _mix_prompt(attn, yconv, ga, gb, x, mk, mv, w, *, tm, seq_len):
    rows = x.shape[0]
    tiles_per_seq = seq_len // tm
    wide = _rows_spec(tm, D_MODEL)
    mem = pl.BlockSpec((N_MEM, D_MODEL), lambda i: (i // tiles_per_seq, 0))
    return pl.pallas_call(
        _mix_prompt_kernel,
        grid=(rows // tm,),
        in_specs=([_rows_spec(tm, GROUP_WIDTH), _rows_spec(tm, CONV_DIM), wide, wide, wide]
                  + _mix_weight_specs() + [mem, mem, _const_spec((D_MODEL, D_MODEL))]),
        out_specs=wide,
        out_shape=jax.ShapeDtypeStruct((rows, D_MODEL), F32),
        compiler_params=_params(1),
        name="mix_prompt",
    )(attn, yconv, ga, gb, x, *_mix_weights(w), mk, mv, w['w_xo'])


def _memkv_kernel(m_ref, g_ref, w_ref, k5_ref, v5_ref, k_ref, v_ref):
    h = _rms(m_ref[...], g_ref[...]).astype(BF16)
    for hd in range(X_HEADS):
        lo = hd * X_HEAD_DIM
        k = _dot(h, w_ref[:, lo:lo + X_HEAD_DIM])
        v = _dot(h, w_ref[:, D_MODEL + lo:D_MODEL + lo + X_HEAD_DIM])
        k5_ref[0, 0, :, hd, :] = k
        v5_ref[0, 0, :, hd, :] = v
        k_ref[:, lo:lo + X_HEAD_DIM] = k.astype(BF16)
        v_ref[:, lo:lo + X_HEAD_DIM] = v.astype(BF16)


def _memkv(mem, norm_g, w_xkv, *, batch):
    spec5 = pl.BlockSpec((1, 1, N_MEM, X_HEADS, X_HEAD_DIM), lambda b: (0, b, 0, 0, 0))
    rows = _rows_spec(N_MEM, D_MODEL)
    return pl.pallas_call(
        _memkv_kernel,
        grid=(batch,),
        in_specs=[rows, _const_spec((1, D_MODEL)), _const_spec((D_MODEL, 2 * D_MODEL))],
        out_specs=[spec5, spec5, rows, rows],
        out_shape=([jax.ShapeDtypeStruct((1, batch, N_MEM, X_HEADS, X_HEAD_DIM), F32)] * 2
                   + [jax.ShapeDtypeStruct((batch * N_MEM, D_MODEL), BF16)] * 2),
        compiler_params=_params(1),
        name="memory_kv",
    )(mem, norm_g, w_xkv)


def _xattn_cached_kernel(q_ref, k_lo, k_hi, v_lo, v_hi, mask_ref, o_ref, *, n_q, n_seq):
    for b in range(n_seq):
        flat = lambda ref: ref[0, b].reshape(N_MEM * X_HEADS, LANES)
        k_all = jnp.concatenate([flat(k_lo), flat(k_hi)], axis=1).astype(BF16)
        v_all = jnp.concatenate([flat(v_lo), flat(v_hi)], axis=1).astype(BF16)
        q = q_ref[b].astype(F32)
        qs = jnp.concatenate([q[:, h * X_HEAD_DIM:(h + 1) * X_HEAD_DIM] for h in range(X_HEADS)],
                             axis=0).astype(BF16)
        s = _dot_nt(qs, k_all) + mask_ref[...]
        m = jnp.max(s, axis=-1, keepdims=True)
        p = jnp.exp(s - m)
        den = jnp.sum(p, axis=-1, keepdims=True)
        acc = _dot(p.astype(BF16), v_all) * (1.0 / den)
        o_ref[b] = jnp.concatenate([acc[h * n_q:(h + 1) * n_q] for h in range(X_HEADS)], axis=1).astype(BF16)


def _xattn_cached(q, mk, mv, *, batch, seqs_per_step=2):
    n_q = q.shape[1]
    n_seq = math.gcd(batch, seqs_per_step)
    q_spec = pl.BlockSpec((n_seq, n_q, D_MODEL), lambda b: (b, 0, 0))
    halves = [pl.BlockSpec((1, n_seq, N_MEM, X_HEADS, LANES), lambda b, hf=hf: (0, b, 0, 0, hf))
              for hf in range(X_HEAD_DIM // LANES)]
    own_head = (np.arange(N_MEM * X_HEADS)[None, :] % X_HEADS) == (np.arange(X_HEADS * n_q)[:, None] // n_q)
    mask = jnp.asarray(np.where(own_head, 0.0, NEG_INF), F32)
    return pl.pallas_call(
        functools.partial(_xattn_cached_kernel, n_q=n_q, n_seq=n_seq),
        grid=(batch // n_seq,),
        in_specs=[q_spec] + halves * 2 + [_const_spec(mask.shape)],
        out_specs=q_spec,
        out_shape=jax.ShapeDtypeStruct((batch, n_q, D_MODEL), BF16),
        compiler_params=_params(1),
        name="cross_attn_cached",
    )(q, mk, mk, mv, mv, mask)


def _to_positions_heads(a_t, batch, n_pos):
    return jnp.transpose(a_t.reshape(batch, HEADS_PER_GROUP, HEAD_DIM, n_pos), (0, 3, 1, 2))[None]


def _to_feature_major(a, batch, n_pos):
    return jnp.transpose(a[0], (0, 2, 3, 1)).reshape(batch, GROUP_WIDTH, n_pos)


def kernel(x_prompt, x_sample, mem_prompt, cache_win0_k, cache_win0_v, cache_win1_k, cache_win1_v, cache_win2_k, cache_win2_v, state_conv, cache_mem_k, cache_mem_v, rel_bias, ffn1_norm, ffn1_w_gate, ffn1_w_up, ffn1_w_down, mix_norm, w_in, w_attn_proj, conv_dw_w, conv_dw_b, conv_ln_g, conv_ln_b, w_conv_proj, w_o, xattn_norm, mem_norm, w_xq, w_xkv, w_xo, ffn2_norm, ffn2_w_gate, ffn2_w_up, ffn2_w_down, final_norm):
    batch, seq_len, _ = x_prompt.shape
    dec_batch, dec_seq, _ = x_sample.shape
    n_g = len(GROUPS)
    assert ffn1_norm.shape[0] == 1, "single layer"
    mat = lambda a: a[0].astype(BF16)
    vec = lambda a: a[0].reshape(1, -1)
    w = dict(ffn1_norm=vec(ffn1_norm), ffn1_w_gate=mat(ffn1_w_gate), ffn1_w_up=mat(ffn1_w_up),
             ffn1_w_down=mat(ffn1_w_down), mix_norm=vec(mix_norm), w_in=mat(w_in),
             w_attn_proj=mat(w_attn_proj), conv_dw_w=conv_dw_w[0], conv_dw_b=vec(conv_dw_b),
             conv_ln_g=vec(conv_ln_g), conv_ln_b=vec(conv_ln_b), w_conv_proj=mat(w_conv_proj),
             w_o=mat(w_o), xattn_norm=vec(xattn_norm), mem_norm=vec(mem_norm), w_xq=mat(w_xq),
             w_xkv=mat(w_xkv), w_xo=mat(w_xo), ffn2_norm=vec(ffn2_norm), ffn2_w_gate=mat(ffn2_w_gate),
             ffn2_w_up=mat(ffn2_w_up), ffn2_w_down=mat(ffn2_w_down))
    final_g = final_norm.reshape(1, -1)
    tap_biases = [_tap_bias(rel_bias, g, dil) for g, (_, dil) in enumerate(GROUPS)]

    def head(x, tm, rows_per_seq, t_groups, conv_w=None):
        x1 = _ffn(x, w['ffn1_norm'], w['ffn1_w_gate'], w['ffn1_w_up'], w['ffn1_w_down'], tm=tm)
        outs = _win(x1, w['mix_norm'], w['w_in'], tm=tm, rows_per_seq=rows_per_seq, t_groups=t_groups, conv_w=conv_w)
        n_t = 2 * len(t_groups)
        return x1, outs[0], outs[1], outs[2], outs[3], outs[4], outs[5:5 + n_t], outs[5 + n_t:]

    rows_p = batch * seq_len
    tm_p = 512
    full_groups = [g for g, (window, _) in enumerate(GROUPS) if window >= seq_len]
    x1, q, k, v, ga, gb, kv_t, (yconv, u_tail) = head(x_prompt.reshape(rows_p, D_MODEL), tm_p, seq_len, full_groups,
                                                      conv_w=w['conv_dw_w'])
    attn = _band_attn(q, k, v, jnp.stack([_band_bias(t) for t in tap_biases]), batch=batch, seq_len=seq_len)
    p_mem_k, p_mem_v, mk_p, mv_p = _memkv(mem_prompt.reshape(batch * N_MEM, D_MODEL), w['mem_norm'], w['w_xkv'],
                                          batch=batch)
    x3 = _mix_prompt(attn, yconv, ga, gb, x1, mk_p, mv_p, w, tm=tm_p, seq_len=seq_len)
    y_prompt = _ffn(x3, w['ffn2_norm'], w['ffn2_w_gate'], w['ffn2_w_up'], w['ffn2_w_down'], tm=tm_p,
                    final_g=final_g)

    p_win = []
    for g, (window, _) in enumerate(GROUPS):
        keep = min(window, seq_len)
        if g in full_groups:
            j = full_groups.index(g)
            p_win += [_to_positions_heads(kv_t[2 * j], batch, seq_len), _to_positions_heads(kv_t[2 * j + 1], batch, seq_len)]
        else:
            for a in (k, v):
                tail = lax.slice(a.reshape(n_g, 2, batch, seq_len, LANES), (g, 0, 0, seq_len - keep, 0),
                                 (g + 1, 2, batch, seq_len, LANES))
                tail = jnp.transpose(tail[0], (1, 2, 0, 3))
                p_win.append(tail.reshape(1, batch, keep, HEADS_PER_GROUP, HEAD_DIM))
    p_conv = u_tail[:, CONV_HALO - (CONV_WIDTH - 1):][None]

    rows_s = dec_batch * dec_seq
    x1, q, k, v, ga, gb, kv_t, (u,) = head(x_sample.reshape(rows_s, D_MODEL), rows_s, rows_s, list(range(n_g)))
    caches_k = [_to_feature_major(c, dec_batch, c.shape[2]) for c in (cache_win0_k, cache_win1_k, cache_win2_k)]
    caches_v = [_to_feature_major(c, dec_batch, c.shape[2]) for c in (cache_win0_v, cache_win1_v, cache_win2_v)]
    groups = lambda a: [jnp.concatenate([a[g, 0], a[g, 1]], axis=-1) for g in range(n_g)]
    new_k, new_v, attn = _cache_attn(groups(q), groups(k), groups(v), kv_t[0::2], kv_t[1::2], caches_k, caches_v,
                                     tap_biases, batch=dec_batch, n_new=dec_seq)
    yconv, s_conv = _conv_sample(u, state_conv[0], w['conv_dw_w'], batch=dec_batch, n_new=dec_seq)
    x2, xq = _mix_sample(attn, yconv, ga, gb, x1, w, tm=rows_s)
    xo = _xattn_cached(xq.reshape(dec_batch, dec_seq, D_MODEL), cache_mem_k, cache_mem_v, batch=dec_batch)
    y_sample = _ffn(x2, w['ffn2_norm'], w['ffn2_w_gate'], w['ffn2_w_up'], w['ffn2_w_down'], tm=rows_s,
                    pre=(xo.reshape(rows_s, D_MODEL), w['w_xo']), final_g=final_g)

    s_win = []
    for g in range(n_g):
        for a in (new_k[g], new_v[g]):
            s_win.append(_to_positions_heads(a, dec_batch, a.shape[2]))

    return (y_prompt.reshape(batch, seq_len, D_MODEL), y_sample.reshape(dec_batch, dec_seq, D_MODEL),
            *p_win, p_conv, p_mem_k, p_mem_v, *s_win, s_conv[None])
```

```python
import functools
import math

import jax
import jax.numpy as jnp
import numpy as np
from jax import lax
from jax.experimental import pallas as pl
from jax.experimental.pallas import tpu as pltpu

D_MODEL = 1024
HEAD_DIM = 64
GROUPS = ((128, 1), (512, 4), (2048, 16))
HEADS_PER_GROUP = 4
GROUP_WIDTH = HEADS_PER_GROUP * HEAD_DIM
ATTN_WIDTH = len(GROUPS) * GROUP_WIDTH
CONV_DIM = D_MODEL // 2
CONV_WIDTH = 31
CONV_HALO = 32
N_BUCKETS = 32
MAX_DISTANCE = 2048
D_FF = ((8 * D_MODEL // 3 + 127) // 128) * 128
N_MEM = 256
X_HEADS = 4
X_HEAD_DIM = D_MODEL // X_HEADS
EPS = 1e-6
NEG_INF = -1e30
SPAN = 128
LANES = 128
SUBLANES = 8

V7X_VMEM_LIMIT_BYTES = 56 * 1024 * 1024
PROMPT_ROW_TILE = 512
BF16 = jnp.bfloat16
F32 = jnp.float32


def _params(n_axes):
    return pltpu.CompilerParams(dimension_semantics=("parallel",) * n_axes,
                                vmem_limit_bytes=V7X_VMEM_LIMIT_BYTES)


def _const_spec(shape):
    return pl.BlockSpec(shape, lambda *_: (0,) * len(shape), pipeline_mode=pl.Buffered(1))


def _rows_spec(tm, width):
    return pl.BlockSpec((tm, width), lambda i: (i, 0))


def _rms(x, g):
    return x * lax.rsqrt(jnp.mean(x * x, axis=-1, keepdims=True) + EPS) * g


def _dot(a, b):
    return jnp.dot(a, b, preferred_element_type=F32)


def _dot_nt(a, b):
    return lax.dot_general(a, b, (((1,), (1,)), ((), ())), preferred_element_type=F32)


def _head_masks(width=GROUP_WIDTH):
    lane = lax.broadcasted_iota(jnp.int32, (1, width), 1)
    return [(lane >= h * HEAD_DIM) & (lane < (h + 1) * HEAD_DIM) for h in range(HEADS_PER_GROUP)]


def _ffn_kernel(*refs, pre_proj, final_norm):
    refs = list(refs)
    x_ref = refs.pop(0)
    if pre_proj:
        a_ref, wp_ref = refs.pop(0), refs.pop(0)
    g_ref, wg_ref, wu_ref, wd_ref = refs.pop(0), refs.pop(0), refs.pop(0), refs.pop(0)
    if final_norm:
        fg_ref = refs.pop(0)
    o_ref = refs.pop(0)

    x = x_ref[...]
    if pre_proj:
        x = x + _dot(a_ref[...], wp_ref[...])
    h = _rms(x, g_ref[...]).astype(BF16)
    gate = _dot(h, wg_ref[...])
    up = _dot(h, wu_ref[...])
    act = (gate * jax.nn.sigmoid(gate) * up).astype(BF16)
    x = x + 0.5 * _dot(act, wd_ref[...])
    if final_norm:
        x = _rms(x, fg_ref[...])
    o_ref[...] = x


def _ffn(x, norm_g, wg, wu, wd, *, tm, pre=None, final_g=None):
    rows = x.shape[0]
    args = [x]
    specs = [_rows_spec(tm, D_MODEL)]
    if pre is not None:
        a, wp = pre
        args += [a, wp]
        specs += [_rows_spec(tm, D_MODEL), _const_spec((D_MODEL, D_MODEL))]
    args += [norm_g, wg, wu, wd]
    specs += [_const_spec((1, D_MODEL)), _const_spec((D_MODEL, D_FF)), _const_spec((D_MODEL, D_FF)),
              _const_spec((D_FF, D_MODEL))]
    if final_g is not None:
        args.append(final_g)
        specs.append(_const_spec((1, D_MODEL)))
    return pl.pallas_call(
        functools.partial(_ffn_kernel, pre_proj=pre is not None, final_norm=final_g is not None),
        grid=(rows // tm,),
        in_specs=specs,
        out_specs=_rows_spec(tm, D_MODEL),
        out_shape=jax.ShapeDtypeStruct((rows, D_MODEL), F32),
        compiler_params=_params(1),
        name="ffn",
    )(*args)


_Q_END = ATTN_WIDTH
_K_END = 2 * ATTN_WIDTH
_V_END = 3 * ATTN_WIDTH
_UA_END = _V_END + CONV_DIM
_UB_END = _UA_END + CONV_DIM
_GA_END = _UB_END + D_MODEL
IN_WIDTH = _GA_END + D_MODEL


def _win_kernel(*refs, t_groups, fuse_conv, tm, tiles_per_seq):
    x_ref, g_ref, w_ref = refs[:3]
    cw_ref = refs[3] if fuse_conv else None
    outs = refs[4:] if fuse_conv else refs[3:]
    q_ref, k_ref, v_ref, ga_ref, gb_ref = outs[:5]
    n_t = 2 * len(t_groups)
    t_refs = outs[5:5 + n_t]
    h = _rms(x_ref[...], g_ref[...]).astype(BF16)

    def seg(lo, hi):
        return _dot(h, w_ref[:, lo:hi])

    u = seg(_V_END, _UA_END) * jax.nn.sigmoid(seg(_UA_END, _UB_END))
    if fuse_conv:
        yc_ref, tail_ref, ext_ref = outs[5 + n_t:]
        first_tile = pl.program_id(0) % tiles_per_seq == 0

        @pl.when(first_tile)
        def _():
            ext_ref[0:CONV_HALO, :] = jnp.zeros((CONV_HALO, CONV_DIM), F32)

        @pl.when(jnp.logical_not(first_tile))
        def _():
            ext_ref[0:CONV_HALO, :] = ext_ref[tm:tm + CONV_HALO, :]

        ext_ref[CONV_HALO:CONV_HALO + tm, :] = u
        ext_ref[CONV_HALO + tm:CONV_HALO + tm + SUBLANES, :] = jnp.zeros((SUBLANES, CONV_DIM), F32)
        tail_ref[0] = u[tm - CONV_HALO:tm]
        conv_pieces = [(r0, l0) for r0 in range(0, tm, CONV_CHUNK_ROWS) for l0 in range(0, CONV_DIM, LANES)]
    else:
        outs[5 + n_t][...] = u
        conv_pieces = []
    n_proj_left = [3 * len(GROUPS) + 2 * (D_MODEL // GROUP_WIDTH)]

    def conv_step():
        n = -(-len(conv_pieces) // n_proj_left[0])
        n_proj_left[0] -= 1
        if n:
            _conv_taps_aligned(ext_ref, cw_ref, yc_ref, CONV_HALO - (CONV_WIDTH - 1), conv_pieces[:n])
            del conv_pieces[:n]

    for g in range(len(GROUPS)):
        lo = g * GROUP_WIDTH
        q = seg(lo, lo + GROUP_WIDTH) * (HEAD_DIM ** -0.5)
        conv_step()
        k = seg(_Q_END + lo, _Q_END + lo + GROUP_WIDTH)
        conv_step()
        v = seg(_K_END + lo, _K_END + lo + GROUP_WIDTH)
        conv_step()
        for hf in range(GROUP_WIDTH // LANES):
            lanes = slice(hf * LANES, (hf + 1) * LANES)
            q_ref[g, hf] = q[:, lanes]
            k_ref[g, hf] = k[:, lanes]
            v_ref[g, hf] = v[:, lanes]
        if g in t_groups:
            j = t_groups.index(g)
            t_refs[2 * j][0] = k.T
            t_refs[2 * j + 1][0] = v.T
    for gate_ref, first in ((ga_ref, _UB_END), (gb_ref, _GA_END)):
        for c in range(0, D_MODEL, GROUP_WIDTH):
            gate_ref[:, c:c + GROUP_WIDTH] = jax.nn.sigmoid(seg(first + c, first + c + GROUP_WIDTH)).astype(BF16)
            conv_step()


def _win(x, norm_g, w_in, *, tm, rows_per_seq, t_groups, conv_w=None):
    rows = x.shape[0]
    n_seq = rows // rows_per_seq
    n_t = 2 * len(t_groups)
    tiles_per_seq = rows_per_seq // tm
    fuse_conv = conv_w is not None
    n_g = len(GROUPS)
    n_hf = GROUP_WIDTH // LANES
    out_shape = ([jax.ShapeDtypeStruct((n_g, n_hf, rows, LANES), F32)] * 3
                 + [jax.ShapeDtypeStruct((rows, D_MODEL), BF16)] * 2
                 + [jax.ShapeDtypeStruct((n_seq, GROUP_WIDTH, rows_per_seq), F32)] * n_t
                 + [jax.ShapeDtypeStruct((rows, CONV_DIM), F32)])
    t_spec = pl.BlockSpec((1, GROUP_WIDTH, tm), lambda i: (i // tiles_per_seq, 0, i % tiles_per_seq))
    out_specs = ([pl.BlockSpec((n_g, n_hf, tm, LANES), lambda i: (0, 0, i, 0))] * 3
                 + [_rows_spec(tm, D_MODEL)] * 2 + [t_spec] * n_t
                 + [_rows_spec(tm, CONV_DIM)])
    in_specs = [_rows_spec(tm, D_MODEL), _const_spec((1, D_MODEL)), _const_spec((D_MODEL, IN_WIDTH))]
    args = [x, norm_g, w_in]
    scratch = []
    if fuse_conv:
        in_specs.append(_const_spec((CONV_WIDTH, CONV_DIM)))
        args.append(conv_w)
        out_shape.append(jax.ShapeDtypeStruct((n_seq, CONV_HALO, CONV_DIM), F32))
        out_specs.append(pl.BlockSpec((1, CONV_HALO, CONV_DIM), lambda i: (i // tiles_per_seq, 0, 0)))
        scratch.append(pltpu.VMEM((CONV_HALO + tm + SUBLANES, CONV_DIM), F32))
    return pl.pallas_call(
        functools.partial(_win_kernel, t_groups=tuple(t_groups), fuse_conv=fuse_conv, tm=tm,
                          tiles_per_seq=tiles_per_seq),
        grid=(rows // tm,),
        in_specs=in_specs,
        out_specs=out_specs,
        out_shape=out_shape,
        scratch_shapes=scratch,
        compiler_params=pltpu.CompilerParams(dimension_semantics=("arbitrary",),
                                             vmem_limit_bytes=V7X_VMEM_LIMIT_BYTES),
        name="w_in",
    )(*args)


def _rel_bucket(dist):
    n = jnp.maximum(dist, 0)
    max_exact = N_BUCKETS // 2
    nf = jnp.maximum(n, 1).astype(F32)
    large = max_exact + (jnp.log(nf / max_exact) / math.log(MAX_DISTANCE / max_exact)
                         * (N_BUCKETS - max_exact)).astype(jnp.int32)
    return jnp.where(n < max_exact, n, jnp.minimum(large, N_BUCKETS - 1))


def _tap_bias(rel_bias, g, dil):
    bias_g = rel_bias[:, g * HEADS_PER_GROUP:(g + 1) * HEADS_PER_GROUP]
    return bias_g[_rel_bucket(jnp.arange(SPAN, -1, -1) * dil)].astype(F32).T


def _toeplitz(vec, n_rows, n_cols):
    n_heads, period = vec.shape
    flat = jnp.tile(vec, (1, n_rows))[:, :n_rows * (period - 1)]
    return flat.reshape(n_heads, n_rows, period - 1)[:, :, :n_cols]


def _band_bias(tap_bias):
    vec = jnp.concatenate([tap_bias, jnp.full((HEADS_PER_GROUP, 2 * SPAN - 1), NEG_INF, F32)], axis=1)
    return _toeplitz(vec, SPAN, 2 * SPAN).reshape(HEADS_PER_GROUP * SPAN, 2 * SPAN)


def _cache_bias(tap_bias, dil, cache_len, n_new):
    rev = lax.pad(tap_bias, jnp.float32(NEG_INF), [(0, 0, 0), (cache_len + n_new - 1 - SPAN * dil, 0, dil - 1)])
    bias_c = _toeplitz(jnp.roll(rev, -(n_new - 1), axis=1), n_new, cache_len)
    near = jnp.concatenate([rev[:, cache_len:], jnp.full((HEADS_PER_GROUP, LANES), NEG_INF, F32)], axis=1)
    bias_n = _toeplitz(jnp.roll(near, -(n_new - 1), axis=1), n_new, LANES)
    return (bias_c.reshape(HEADS_PER_GROUP * n_new, cache_len), bias_n.reshape(HEADS_PER_GROUP * n_new, LANES))


MAX_ROW_STRIDE = 4


def _gather_residues(src, store, tmp_ref, dil, seq):
    if dil == 1:
        store(0, src[...])
    elif dil <= MAX_ROW_STRIDE:
        for r in range(dil):
            store(r, src[pl.ds(r, seq, stride=dil), :])
    else:
        inner = dil // MAX_ROW_STRIDE
        assert inner <= MAX_ROW_STRIDE and inner * MAX_ROW_STRIDE == dil
        n_part = seq * inner
        for c in range(MAX_ROW_STRIDE):
            tmp_ref[c * n_part:(c + 1) * n_part, :] = src[pl.ds(c, n_part, stride=MAX_ROW_STRIDE), :]
        for c in range(MAX_ROW_STRIDE):
            for c2 in range(inner):
                store(c + MAX_ROW_STRIDE * c2, tmp_ref[pl.ds(c * n_part + c2, seq, stride=inner), :])


def _scatter_residues(load, dst, tmp_ref, dil, seq):
    if dil == 1:
        dst[...] = load(0)
    elif dil <= MAX_ROW_STRIDE:
        for r in range(dil):
            dst[pl.ds(r, seq, stride=dil), :] = load(r)
    else:
        inner = dil // MAX_ROW_STRIDE
        assert inner <= MAX_ROW_STRIDE and inner * MAX_ROW_STRIDE == dil
        n_part = seq * inner
        for c in range(MAX_ROW_STRIDE):
            for c2 in range(inner):
                tmp_ref[pl.ds(c * n_part + c2, seq, stride=inner), :] = load(c + MAX_ROW_STRIDE * c2)
        for c in range(MAX_ROW_STRIDE):
            dst[pl.ds(c, n_part, stride=MAX_ROW_STRIDE), :] = tmp_ref[c * n_part:(c + 1) * n_part, :]


def _merge_groups(outs, lses):
    m = functools.reduce(jnp.maximum, lses)
    es = [jnp.exp(l - m) for l in lses]
    num = sum(e * o for e, o in zip(es, outs))
    return num * (1.0 / sum(es))


def _band_group(q_lo, q_hi, k_lo, k_hi, v_lo, v_hi, b_ref, o_dst, l_dst, qb_ref, kb_ref, vb_ref, ob_ref, lb_ref,
                tmp_ref, *, dil, seq):
    n_blk = seq // SPAN
    halves = (slice(0, LANES), slice(LANES, 2 * LANES))
    for dst, srcs in ((qb_ref, (q_lo, q_hi)), (kb_ref, (k_lo, k_hi)), (vb_ref, (v_lo, v_hi))):
        for lanes, src in zip(halves, srcs):
            def store(r, rows, dst=dst, lanes=lanes):
                dst[r * seq:(r + 1) * seq, lanes] = rows.astype(BF16)
            _gather_residues(src, store, tmp_ref, dil, seq)
    masks = _head_masks()

    def block(q, kb, vb, key_lo):
        qm = jnp.concatenate([jnp.where(masks[h], q, jnp.zeros_like(q)) for h in range(HEADS_PER_GROUP)], axis=0)
        s = _dot_nt(qm, kb) + b_ref[:, key_lo:]
        m = jnp.max(s, axis=-1, keepdims=True)
        p = jnp.exp(s - m)
        den = jnp.sum(p, axis=-1, keepdims=True)
        acc = _dot(p.astype(BF16), vb) * (1.0 / den)
        lse_rows = m + jnp.log(den)
        o = acc[0:SPAN]
        lse = jnp.broadcast_to(lse_rows[0:SPAN], (SPAN, GROUP_WIDTH))
        for h in range(1, HEADS_PER_GROUP):
            o = jnp.where(masks[h], acc[h * SPAN:(h + 1) * SPAN], o)
            lse = jnp.where(masks[h], lse_rows[h * SPAN:(h + 1) * SPAN], lse)
        return o, lse

    for r in range(dil):
        for j in range(n_blk):
            q_rows = slice(r * seq + j * SPAN, r * seq + (j + 1) * SPAN)
            k_rows = slice(r * seq + max(j - 1, 0) * SPAN, r * seq + (j + 1) * SPAN)
            o, lse = block(qb_ref[q_rows, :], kb_ref[k_rows, :], vb_ref[k_rows, :], SPAN if j == 0 else 0)
            ob_ref[q_rows, :] = o
            lb_ref[q_rows, :] = lse

    for src, dsts in ((ob_ref, o_dst), (lb_ref, l_dst)):
        for lanes, dst in zip(halves, dsts):
            _scatter_residues(lambda r, src=src, lanes=lanes: src[r * seq:(r + 1) * seq, lanes], dst, tmp_ref, dil, seq)


MERGE_CHUNK_ROWS = 256


def _band_attn_kernel(q_lo, q_hi, k_lo, k_hi, v_lo, v_hi, b_ref, out_ref,
                      qb_ref, kb_ref, vb_ref, ob_ref, lb_ref, tmp_ref, *nat_refs, seq_len):
    n_half = 2 * len(GROUPS)
    on_refs, ln_refs = nat_refs[:n_half], nat_refs[n_half:]
    g = pl.program_id(1)
    for gi, (_, dil) in enumerate(GROUPS):
        @pl.when(g == gi)
        def _(gi=gi, dil=dil):
            _band_group(q_lo, q_hi, k_lo, k_hi, v_lo, v_hi, b_ref,
                        on_refs[2 * gi:2 * gi + 2], ln_refs[2 * gi:2 * gi + 2],
                        qb_ref, kb_ref, vb_ref, ob_ref, lb_ref, tmp_ref, dil=dil, seq=seq_len // dil)

    @pl.when(g == len(GROUPS) - 1)
    def _():
        for r0 in range(0, seq_len, MERGE_CHUNK_ROWS):
            rows = slice(r0, r0 + MERGE_CHUNK_ROWS)
            for hf in (0, 1):
                merged = _merge_groups([on_refs[2 * gi + hf][rows, :] for gi in range(len(GROUPS))],
                                       [ln_refs[2 * gi + hf][rows, :] for gi in range(len(GROUPS))])
                out_ref[rows, hf * LANES:(hf + 1) * LANES] = merged.astype(BF16)


def _band_attn(q, k, v, biases, *, batch, seq_len):
    n_g = len(GROUPS)
    view = lambda a: a.reshape(n_g, 2, batch, seq_len, LANES)
    in_halves = [pl.BlockSpec((None, None, None, seq_len, LANES), lambda b, g, hf=hf: (g, hf, b, 0, 0))
                 for hf in (0, 1)]
    out = pl.pallas_call(
        functools.partial(_band_attn_kernel, seq_len=seq_len),
        grid=(batch, n_g),
        in_specs=in_halves * 3 + [pl.BlockSpec((None, HEADS_PER_GROUP * SPAN, 2 * SPAN), lambda b, g: (g, 0, 0))],
        out_specs=pl.BlockSpec((None, seq_len, GROUP_WIDTH), lambda b, g: (b, 0, 0)),
        out_shape=jax.ShapeDtypeStruct((batch, seq_len, GROUP_WIDTH), BF16),
        scratch_shapes=([pltpu.VMEM((seq_len, GROUP_WIDTH), BF16)] * 3 + [pltpu.VMEM((seq_len, GROUP_WIDTH), F32)] * 2
                        + [pltpu.VMEM((seq_len, LANES), F32)]
                        + [pltpu.VMEM((seq_len, LANES), F32)] * (4 * n_g)),
        compiler_params=pltpu.CompilerParams(dimension_semantics=("parallel", "arbitrary"),
                                             vmem_limit_bytes=V7X_VMEM_LIMIT_BYTES),
        name="band_attn",
    )(view(q), view(q), view(k), view(k), view(v), view(v), biases)
    return out.reshape(batch * seq_len, GROUP_WIDTH)


def _cache_attn_kernel(*refs, n_new):
    n_g = len(GROUPS)
    q_refs, kn_refs, vn_refs = refs[0:n_g], refs[n_g:2 * n_g], refs[2 * n_g:3 * n_g]
    knt_refs, vnt_refs = refs[3 * n_g:4 * n_g], refs[4 * n_g:5 * n_g]
    kc_refs, vc_refs = refs[5 * n_g:6 * n_g], refs[6 * n_g:7 * n_g]
    bc_refs, bn_refs = refs[7 * n_g:8 * n_g], refs[8 * n_g:9 * n_g]
    outs = refs[9 * n_g:]
    ko_refs, vo_refs = outs[0:n_g], outs[n_g:2 * n_g]
    attn_ref = outs[2 * n_g]
    group_out, group_lse = [], []
    masks = _head_masks()
    pad = jnp.zeros((SPAN - n_new, GROUP_WIDTH), F32)
    lane = lax.broadcasted_iota(jnp.int32, (1, LANES), 1)
    j = pl.program_id(0) % (LANES // n_new)
    to_tail = LANES - n_new - n_new * j

    for g in range(n_g):
        kc, vc = kc_refs[g][0], vc_refs[g][0]
        kn, vn = kn_refs[g][0], vn_refs[g][0]
        cache_len = kc.shape[1]
        for src, new_t, dst in ((kc, knt_refs[g][0], ko_refs[g]), (vc, vnt_refs[g][0], vo_refs[g])):
            shifted = pltpu.roll(src, cache_len - n_new, axis=1)
            new_tail = pltpu.roll(new_t, to_tail, axis=1)
            if cache_len > LANES:
                dst[0, :, 0:cache_len - LANES] = shifted[:, 0:cache_len - LANES]
            dst[0, :, cache_len - LANES:cache_len] = jnp.where(lane >= LANES - n_new, new_tail,
                                                              shifted[:, cache_len - LANES:cache_len])
        q = q_refs[g][0]
        qm = jnp.concatenate([jnp.where(masks[h], q, 0.0) for h in range(HEADS_PER_GROUP)], axis=0).astype(BF16)
        kn_p = jnp.concatenate([kn, pad], axis=0).astype(BF16)
        vn_p = jnp.concatenate([vn, pad], axis=0).astype(BF16)
        s_c = _dot(qm, kc.astype(BF16)) + bc_refs[g][...]
        s_n = _dot_nt(qm, kn_p) + bn_refs[g][...]
        m = jnp.maximum(jnp.max(s_c, axis=-1, keepdims=True), jnp.max(s_n, axis=-1, keepdims=True))
        p_c = jnp.exp(s_c - m)
        p_n = jnp.exp(s_n - m)
        den = jnp.sum(p_c, axis=-1, keepdims=True) + jnp.sum(p_n, axis=-1, keepdims=True)
        acc = _dot_nt(p_c.astype(BF16), vc.astype(BF16)) + _dot(p_n.astype(BF16), vn_p)
        acc = acc * (1.0 / den)
        lse_rows = m + jnp.log(den)
        o = jnp.zeros((n_new, GROUP_WIDTH), F32)
        lse = jnp.zeros((n_new, GROUP_WIDTH), F32)
        for h in range(HEADS_PER_GROUP):
            rows = slice(h * n_new, (h + 1) * n_new)
            o = jnp.where(masks[h], acc[rows], o)
            lse = jnp.where(masks[h], lse_rows[rows], lse)
        group_out.append(o)
        group_lse.append(lse)
    attn_ref[0] = _merge_groups(group_out, group_lse).astype(BF16)


def _cache_attn(q, k_new, v_new, k_new_t, v_new_t, k_cache, v_cache, tap_biases, *, batch, n_new):
    n_g = len(GROUPS)
    per_tile = LANES // n_new
    new3 = lambda a: a.reshape(batch, n_new, GROUP_WIDTH)
    new_spec = pl.BlockSpec((1, n_new, GROUP_WIDTH), lambda b: (b, 0, 0))
    new_t_spec = pl.BlockSpec((1, GROUP_WIDTH, LANES), lambda b: (0, 0, b // per_tile))
    cache_specs = [pl.BlockSpec((1, GROUP_WIDTH, c.shape[2]), lambda b: (b, 0, 0)) for c in k_cache]
    biases_c, biases_n = [], []
    for g, (_, dil) in enumerate(GROUPS):
        bc, bn = _cache_bias(tap_biases[g], dil, k_cache[g].shape[2], n_new)
        biases_c.append(bc)
        biases_n.append(bn)
    args = ([new3(a) for a in q] + [new3(a) for a in k_new] + [new3(a) for a in v_new]
            + list(k_new_t) + list(v_new_t) + list(k_cache) + list(v_cache) + biases_c + biases_n)
    in_specs = ([new_spec] * (3 * n_g) + [new_t_spec] * (2 * n_g) + cache_specs * 2
                + [_const_spec(b.shape) for b in biases_c] + [_const_spec(b.shape) for b in biases_n])
    out_shape = ([jax.ShapeDtypeStruct(c.shape, F32) for c in k_cache] * 2
                 + [jax.ShapeDtypeStruct((batch, n_new, GROUP_WIDTH), BF16)])
    out_specs = cache_specs * 2 + [new_spec]
    outs = pl.pallas_call(
        functools.partial(_cache_attn_kernel, n_new=n_new),
        grid=(batch,),
        in_specs=in_specs,
        out_specs=out_specs,
        out_shape=out_shape,
        compiler_params=_params(1),
        name="cache_attn",
    )(*args)
    return outs[0:n_g], outs[n_g:2 * n_g], outs[2 * n_g].reshape(batch * n_new, GROUP_WIDTH)


def _conv_taps(ext_ref, w_ref, first_row, n_rows):
    acc = jnp.zeros((n_rows, CONV_DIM), F32)
    for j in range(CONV_WIDTH):
        acc = acc + ext_ref[pl.ds(first_row + j, n_rows), :] * w_ref[j:j + 1, :]
    return acc


CONV_CHUNK_ROWS = 64


def _conv_taps_aligned(ext_ref, w_ref, y_ref, first_row, pieces):
    for r0, l0 in pieces:
        lanes = slice(l0, l0 + LANES)
        acc = None
        for b in range(SUBLANES):
            part = None
            for a in range((first_row + CONV_WIDTH - 1 - b) // SUBLANES + 1):
                j = SUBLANES * a + b - first_row
                if j < 0:
                    continue
                term = ext_ref[pl.ds(r0 + SUBLANES * a, CONV_CHUNK_ROWS + SUBLANES), lanes] * w_ref[j:j + 1, lanes]
                part = term if part is None else part + term
            part = part[b:b + CONV_CHUNK_ROWS]
            acc = part if acc is None else acc + part
        y_ref[r0:r0 + CONV_CHUNK_ROWS, lanes] = acc


def _conv_sample_kernel(u_ref, st_ref, w_ref, y_ref, so_ref, ext_ref, *, n_new, n_seq):
    hist = CONV_WIDTH - 1
    for b in range(n_seq):
        ext_ref[b, 0:hist, :] = st_ref[b]
        ext_ref[b, hist:hist + n_new, :] = u_ref[b]
        y_ref[b] = _conv_taps(ext_ref.at[b], w_ref, 0, n_new)
        so_ref[b] = ext_ref[b, n_new:n_new + hist, :]


def _conv_sample(u, state, w, *, batch, n_new, seqs_per_step=8):
    hist = CONV_WIDTH - 1
    u3 = u.reshape(batch, n_new, CONV_DIM)
    n_seq = math.gcd(batch, seqs_per_step)
    y, st = pl.pallas_call(
        functools.partial(_conv_sample_kernel, n_new=n_new, n_seq=n_seq),
        grid=(batch // n_seq,),
        in_specs=[pl.BlockSpec((n_seq, n_new, CONV_DIM), lambda b: (b, 0, 0)),
                  pl.BlockSpec((n_seq, hist, CONV_DIM), lambda b: (b, 0, 0)),
                  pl.BlockSpec((CONV_WIDTH, CONV_DIM), lambda b: (0, 0))],
        out_specs=[pl.BlockSpec((n_seq, n_new, CONV_DIM), lambda b: (b, 0, 0)),
                   pl.BlockSpec((n_seq, hist, CONV_DIM), lambda b: (b, 0, 0))],
        out_shape=[jax.ShapeDtypeStruct((batch, n_new, CONV_DIM), F32),
                   jax.ShapeDtypeStruct((batch, hist, CONV_DIM), F32)],
        scratch_shapes=[pltpu.VMEM((n_seq, hist + n_new + 2, CONV_DIM), F32)],
        compiler_params=_params(1),
        name="conv_sample",
    )(u3, state, w)
    return y.reshape(batch * n_new, CONV_DIM), st


def _mix_branches(attn, yconv, gate_a, gate_b, x, wap_ref, cb_ref, lg_ref, lb_ref, wcp_ref, wo_ref):
    a_branch = _dot(attn, wap_ref[...])
    y = yconv + cb_ref[...]
    mu = jnp.mean(y, axis=-1, keepdims=True)
    yc = y - mu
    var = jnp.mean(yc * yc, axis=-1, keepdims=True)
    y = yc * lax.rsqrt(var + EPS) * lg_ref[...] + lb_ref[...]
    y = y * jax.nn.sigmoid(y)
    c_branch = _dot(y.astype(BF16), wcp_ref[...])
    merged = gate_a * a_branch + gate_b * c_branch
    return x + _dot(merged.astype(BF16), wo_ref[...])


def _xattn_query(x, xg_ref, wxq_ref):
    hq = _rms(x, xg_ref[...]).astype(BF16)
    return (_dot(hq, wxq_ref[...]) * (X_HEAD_DIM ** -0.5)).astype(BF16)


def _xattn_heads(q, k_ref, v_ref):
    outs = []
    for h in range(X_HEADS):
        lanes = slice(h * X_HEAD_DIM, (h + 1) * X_HEAD_DIM)
        s = _dot_nt(q[:, lanes], k_ref[:, lanes])
        m = jnp.max(s, axis=-1, keepdims=True)
        p = jnp.exp(s - m)
        den = jnp.sum(p, axis=-1, keepdims=True)
        outs.append((_dot(p.astype(BF16), v_ref[:, lanes]) * (1.0 / den)).astype(BF16))
    return jnp.concatenate(outs, axis=-1)


def _mix_sample_kernel(at_ref, yc_ref, ga_ref, gb_ref, x_ref, wap_ref, cb_ref, lg_ref, lb_ref, wcp_ref, wo_ref,
                       xg_ref, wxq_ref, x_out, q_out):
    x = _mix_branches(at_ref[...], yc_ref[...], ga_ref[...], gb_ref[...], x_ref[...],
                      wap_ref, cb_ref, lg_ref, lb_ref, wcp_ref, wo_ref)
    x_out[...] = x
    q_out[...] = _xattn_query(x, xg_ref, wxq_ref)


def _mix_prompt_kernel(at_ref, yc_ref, ga_ref, gb_ref, x_ref, wap_ref, cb_ref, lg_ref, lb_ref, wcp_ref, wo_ref,
                       xg_ref, wxq_ref, mk_ref, mv_ref, wxo_ref, x_out):
    x = _mix_branches(at_ref[...], yc_ref[...], ga_ref[...], gb_ref[...], x_ref[...],
                      wap_ref, cb_ref, lg_ref, lb_ref, wcp_ref, wo_ref)
    attn = _xattn_heads(_xattn_query(x, xg_ref, wxq_ref), mk_ref, mv_ref)
    x_out[...] = x + _dot(attn, wxo_ref[...])


def _mix_weight_specs():
    return [_const_spec((GROUP_WIDTH, D_MODEL)), _const_spec((1, CONV_DIM)), _const_spec((1, CONV_DIM)),
            _const_spec((1, CONV_DIM)), _const_spec((CONV_DIM, D_MODEL)), _const_spec((D_MODEL, D_MODEL)),
            _const_spec((1, D_MODEL)), _const_spec((D_MODEL, D_MODEL))]


def _mix_weights(w):
    return (w['w_attn_proj'], w['conv_dw_b'], w['conv_ln_g'], w['conv_ln_b'], w['w_conv_proj'], w['w_o'],
            w['xattn_norm'], w['w_xq'])


def _mix_sample(attn, yconv, ga, gb, x, w, *, tm):
    rows = x.shape[0]
    wide = _rows_spec(tm, D_MODEL)
    return pl.pallas_call(
        _mix_sample_kernel,
        grid=(rows // tm,),
        in_specs=[_rows_spec(tm, GROUP_WIDTH), _rows_spec(tm, CONV_DIM), wide, wide, wide] + _mix_weight_specs(),
        out_specs=[wide, wide],
        out_shape=[jax.ShapeDtypeStruct((rows, D_MODEL), F32), jax.ShapeDtypeStruct((rows, D_MODEL), BF16)],
        compiler_params=_params(1),
        name="mix_sample",
    )(attn, yconv, ga, gb, x, *_mix_weights(w))


def _mix_prompt(attn, yconv, ga, gb, x, mk, mv, w, *, tm, seq_len):
    rows = x.shape[0]
    tiles_per_seq = seq_len // tm
    wide = _rows_spec(tm, D_MODEL)
    mem = pl.BlockSpec((N_MEM, D_MODEL), lambda i: (i // tiles_per_seq, 0))
    return pl.pallas_call(
        _mix_prompt_kernel,
        grid=(rows // tm,),
        in_specs=([_rows_spec(tm, GROUP_WIDTH), _rows_spec(tm, CONV_DIM), wide, wide, wide]
                  + _mix_weight_specs() + [mem, mem, _const_spec((D_MODEL, D_MODEL))]),
        out_specs=wide,
        out_shape=jax.ShapeDtypeStruct((rows, D_MODEL), F32),
        compiler_params=_params(1),
        name="mix_prompt",
    )(attn, yconv, ga, gb, x, *_mix_weights(w), mk, mv, w['w_xo'])


def _memkv_kernel(m_ref, g_ref, w_ref, k5_ref, v5_ref, k_ref, v_ref, *, n_seq):
    h = _rms(m_ref[...], g_ref[...]).astype(BF16)
    for hd in range(X_HEADS):
        lo = hd * X_HEAD_DIM
        k = _dot(h, w_ref[:, lo:lo + X_HEAD_DIM])
        v = _dot(h, w_ref[:, D_MODEL + lo:D_MODEL + lo + X_HEAD_DIM])
        for b in range(n_seq):
            k5_ref[0, b, :, hd, :] = k[b * N_MEM:(b + 1) * N_MEM]
            v5_ref[0, b, :, hd, :] = v[b * N_MEM:(b + 1) * N_MEM]
        k_ref[:, lo:lo + X_HEAD_DIM] = k.astype(BF16)
        v_ref[:, lo:lo + X_HEAD_DIM] = v.astype(BF16)


def _memkv(mem, norm_g, w_xkv, *, batch, seqs_per_step=4):
    n_seq = math.gcd(batch, seqs_per_step)
    spec5 = pl.BlockSpec((1, n_seq, N_MEM, X_HEADS, X_HEAD_DIM), lambda b: (0, b, 0, 0, 0))
    rows = _rows_spec(n_seq * N_MEM, D_MODEL)
    return pl.pallas_call(
        functools.partial(_memkv_kernel, n_seq=n_seq),
        grid=(batch // n_seq,),
        in_specs=[rows, _const_spec((1, D_MODEL)), _const_spec((D_MODEL, 2 * D_MODEL))],
        out_specs=[spec5, spec5, rows, rows],
        out_shape=([jax.ShapeDtypeStruct((1, batch, N_MEM, X_HEADS, X_HEAD_DIM), F32)] * 2
                   + [jax.ShapeDtypeStruct((batch * N_MEM, D_MODEL), BF16)] * 2),
        compiler_params=_params(1),
        name="memory_kv",
    )(mem, norm_g, w_xkv)


def _xattn_cached_kernel(q_ref, k_lo, k_hi, v_lo, v_hi, mask_ref, o_ref, *, n_q, n_seq):
    for b in range(n_seq):
        flat = lambda ref: ref[0, b].reshape(N_MEM * X_HEADS, LANES)
        k_all = jnp.concatenate([flat(k_lo), flat(k_hi)], axis=1).astype(BF16)
        v_all = jnp.concatenate([flat(v_lo), flat(v_hi)], axis=1).astype(BF16)
        q = q_ref[b].astype(F32)
        qs = jnp.concatenate([q[:, h * X_HEAD_DIM:(h + 1) * X_HEAD_DIM] for h in range(X_HEADS)],
                             axis=0).astype(BF16)
        s = _dot_nt(qs, k_all) + mask_ref[...]
        m = jnp.max(s, axis=-1, keepdims=True)
        p = jnp.exp(s - m)
        den = jnp.sum(p, axis=-1, keepdims=True)
        acc = _dot(p.astype(BF16), v_all) * (1.0 / den)
        o_ref[b] = jnp.concatenate([acc[h * n_q:(h + 1) * n_q] for h in range(X_HEADS)], axis=1).astype(BF16)


def _xattn_cached(q, mk, mv, *, batch, seqs_per_step=4):
    n_q = q.shape[1]
    n_seq = math.gcd(batch, seqs_per_step)
    q_spec = pl.BlockSpec((n_seq, n_q, D_MODEL), lambda b: (b, 0, 0))
    halves = [pl.BlockSpec((1, n_seq, N_MEM, X_HEADS, LANES), lambda b, hf=hf: (0, b, 0, 0, hf))
              for hf in range(X_HEAD_DIM // LANES)]
    own_head = (np.arange(N_MEM * X_HEADS)[None, :] % X_HEADS) == (np.arange(X_HEADS * n_q)[:, None] // n_q)
    mask = jnp.asarray(np.where(own_head, 0.0, NEG_INF), F32)
    return pl.pallas_call(
        functools.partial(_xattn_cached_kernel, n_q=n_q, n_seq=n_seq),
        grid=(batch // n_seq,),
        in_specs=[q_spec] + halves * 2 + [_const_spec(mask.shape)],
        out_specs=q_spec,
        out_shape=jax.ShapeDtypeStruct((batch, n_q, D_MODEL), BF16),
        compiler_params=_params(1),
        name="cross_attn_cached",
    )(q, mk, mk, mv, mv, mask)


def _to_positions_heads(a_t, batch, n_pos):
    return jnp.transpose(a_t.reshape(batch, HEADS_PER_GROUP, HEAD_DIM, n_pos), (0, 3, 1, 2))[None]


def _to_feature_major(a, batch, n_pos):
    return jnp.transpose(a[0], (0, 2, 3, 1)).reshape(batch, GROUP_WIDTH, n_pos)


def kernel(x_prompt, x_sample, mem_prompt, cache_win0_k, cache_win0_v, cache_win1_k, cache_win1_v, cache_win2_k, cache_win2_v, state_conv, cache_mem_k, cache_mem_v, rel_bias, ffn1_norm, ffn1_w_gate, ffn1_w_up, ffn1_w_down, mix_norm, w_in, w_attn_proj, conv_dw_w, conv_dw_b, conv_ln_g, conv_ln_b, w_conv_proj, w_o, xattn_norm, mem_norm, w_xq, w_xkv, w_xo, ffn2_norm, ffn2_w_gate, ffn2_w_up, ffn2_w_down, final_norm):
    batch, seq_len, _ = x_prompt.shape
    dec_batch, dec_seq, _ = x_sample.shape
    n_g = len(GROUPS)
    assert ffn1_norm.shape[0] == 1, "single layer"
    mat = lambda a: a[0].astype(BF16)
    vec = lambda a: a[0].reshape(1, -1)
    w = dict(ffn1_norm=vec(ffn1_norm), ffn1_w_gate=mat(ffn1_w_gate), ffn1_w_up=mat(ffn1_w_up),
             ffn1_w_down=mat(ffn1_w_down), mix_norm=vec(mix_norm), w_in=mat(w_in),
             w_attn_proj=mat(w_attn_proj), conv_dw_w=conv_dw_w[0], conv_dw_b=vec(conv_dw_b),
             conv_ln_g=vec(conv_ln_g), conv_ln_b=vec(conv_ln_b), w_conv_proj=mat(w_conv_proj),
             w_o=mat(w_o), xattn_norm=vec(xattn_norm), mem_norm=vec(mem_norm), w_xq=mat(w_xq),
             w_xkv=mat(w_xkv), w_xo=mat(w_xo), ffn2_norm=vec(ffn2_norm), ffn2_w_gate=mat(ffn2_w_gate),
             ffn2_w_up=mat(ffn2_w_up), ffn2_w_down=mat(ffn2_w_down))
    final_g = final_norm.reshape(1, -1)
    tap_biases = [_tap_bias(rel_bias, g, dil) for g, (_, dil) in enumerate(GROUPS)]

    def head(x, tm, rows_per_seq, t_groups, conv_w=None):
        x1 = _ffn(x, w['ffn1_norm'], w['ffn1_w_gate'], w['ffn1_w_up'], w['ffn1_w_down'], tm=tm)
        outs = _win(x1, w['mix_norm'], w['w_in'], tm=tm, rows_per_seq=rows_per_seq, t_groups=t_groups, conv_w=conv_w)
        n_t = 2 * len(t_groups)
        return x1, outs[0], outs[1], outs[2], outs[3], outs[4], outs[5:5 + n_t], outs[5 + n_t:]

    rows_p = batch * seq_len
    tm_p = PROMPT_ROW_TILE
    full_groups = [g for g, (window, _) in enumerate(GROUPS) if window >= seq_len]
    x1, q, k, v, ga, gb, kv_t, (yconv, u_tail) = head(x_prompt.reshape(rows_p, D_MODEL), tm_p, seq_len, full_groups,
                                                      conv_w=w['conv_dw_w'])
    attn = _band_attn(q, k, v, jnp.stack([_band_bias(t) for t in tap_biases]), batch=batch, seq_len=seq_len)
    p_mem_k, p_mem_v, mk_p, mv_p = _memkv(mem_prompt.reshape(batch * N_MEM, D_MODEL), w['mem_norm'], w['w_xkv'],
                                          batch=batch)
    x3 = _mix_prompt(attn, yconv, ga, gb, x1, mk_p, mv_p, w, tm=tm_p, seq_len=seq_len)
    y_prompt = _ffn(x3, w['ffn2_norm'], w['ffn2_w_gate'], w['ffn2_w_up'], w['ffn2_w_down'], tm=tm_p,
                    final_g=final_g)

    p_win = []
    for g, (window, _) in enumerate(GROUPS):
        keep = min(window, seq_len)
        if g in full_groups:
            j = full_groups.index(g)
            p_win += [_to_positions_heads(kv_t[2 * j], batch, seq_len), _to_positions_heads(kv_t[2 * j + 1], batch, seq_len)]
        else:
            for a in (k, v):
                tail = lax.slice(a.reshape(n_g, 2, batch, seq_len, LANES), (g, 0, 0, seq_len - keep, 0),
                                 (g + 1, 2, batch, seq_len, LANES))
                tail = jnp.transpose(tail[0], (1, 2, 0, 3))
                p_win.append(tail.reshape(1, batch, keep, HEADS_PER_GROUP, HEAD_DIM))
    p_conv = u_tail[:, CONV_HALO - (CONV_WIDTH - 1):][None]

    rows_s = dec_batch * dec_seq
    x1, q, k, v, ga, gb, kv_t, (u,) = head(x_sample.reshape(rows_s, D_MODEL), rows_s, rows_s, list(range(n_g)))
    caches_k = [_to_feature_major(c, dec_batch, c.shape[2]) for c in (cache_win0_k, cache_win1_k, cache_win2_k)]
    caches_v = [_to_feature_major(c, dec_batch, c.shape[2]) for c in (cache_win0_v, cache_win1_v, cache_win2_v)]
    groups = lambda a: [jnp.concatenate([a[g, 0], a[g, 1]], axis=-1) for g in range(n_g)]
    new_k, new_v, attn = _cache_attn(groups(q), groups(k), groups(v), kv_t[0::2], kv_t[1::2], caches_k, caches_v,
                                     tap_biases, batch=dec_batch, n_new=dec_seq)
    yconv, s_conv = _conv_sample(u, state_conv[0], w['conv_dw_w'], batch=dec_batch, n_new=dec_seq)
    x2, xq = _mix_sample(attn, yconv, ga, gb, x1, w, tm=rows_s)
    xo = _xattn_cached(xq.reshape(dec_batch, dec_seq, D_MODEL), cache_mem_k, cache_mem_v, batch=dec_batch)
    y_sample = _ffn(x2, w['ffn2_norm'], w['ffn2_w_gate'], w['ffn2_w_up'], w['ffn2_w_down'], tm=rows_s,
                    pre=(xo.reshape(rows_s, D_MODEL), w['w_xo']), final_g=final_g)

    s_win = []
    for g in range(n_g):
        for a in (new_k[g], new_v[g]):
            s_win.append(_to_positions_heads(a, dec_batch, a.shape[2]))

    return (y_prompt.reshape(batch, seq_len, D_MODEL), y_sample.reshape(dec_batch, dec_seq, D_MODEL),
            *p_win, p_conv, p_mem_k, p_mem_v, *s_win, s_conv[None])
```

```python
import functools
import math

import jax
import jax.numpy as jnp
import numpy as np
from jax import lax
from jax.experimental import pallas as pl
from jax.experimental.pallas import tpu as pltpu

D_MODEL = 1024
HEAD_DIM = 64
GROUPS = ((128, 1), (512, 4), (2048, 16))
HEADS_PER_GROUP = 4
GROUP_WIDTH = HEADS_PER_GROUP * HEAD_DIM
ATTN_WIDTH = len(GROUPS) * GROUP_WIDTH
CONV_DIM = D_MODEL // 2
CONV_WIDTH = 31
CONV_HALO = 32
N_BUCKETS = 32
MAX_DISTANCE = 2048
D_FF = ((8 * D_MODEL // 3 + 127) // 128) * 128
N_MEM = 256
X_HEADS = 4
X_HEAD_DIM = D_MODEL // X_HEADS
EPS = 1e-6
NEG_INF = -1e30
SPAN = 128
LANES = 128
SUBLANES = 8

V7X_VMEM_LIMIT_BYTES = 56 * 1024 * 1024
PROMPT_ROW_TILE = 512
BF16 = jnp.bfloat16
F32 = jnp.float32


def _params(n_axes):
    return pltpu.CompilerParams(dimension_semantics=("parallel",) * n_axes,
                                vmem_limit_bytes=V7X_VMEM_LIMIT_BYTES)


def _const_spec(shape):
    return pl.BlockSpec(shape, lambda *_: (0,) * len(shape), pipeline_mode=pl.Buffered(1))


def _rows_spec(tm, width):
    return pl.BlockSpec((tm, width), lambda i: (i, 0))


def _rms(x, g):
    return x * lax.rsqrt(jnp.mean(x * x, axis=-1, keepdims=True) + EPS) * g


def _dot(a, b):
    return jnp.dot(a, b, preferred_element_type=F32)


def _dot_nt(a, b):
    return lax.dot_general(a, b, (((1,), (1,)), ((), ())), preferred_element_type=F32)


def _head_masks(width=GROUP_WIDTH):
    lane = lax.broadcasted_iota(jnp.int32, (1, width), 1)
    return [(lane >= h * HEAD_DIM) & (lane < (h + 1) * HEAD_DIM) for h in range(HEADS_PER_GROUP)]


MXU_WIDTH = 256
FFN_SIDE_JOB_PIECES = 4
CACHE_SHIFT_ROWS = 64


def _cache_shift_jobs(src_ref, new_t_ref, dst_ref, to_tail, n_new):
    features, cache_len = src_ref.shape[1:]
    lane = lax.broadcasted_iota(jnp.int32, (1, LANES), 1)

    def job(r0):
        rows = slice(r0, r0 + CACHE_SHIFT_ROWS)
        shifted = pltpu.roll(src_ref[0, rows, :], cache_len - n_new, axis=1)
        new_tail = pltpu.roll(new_t_ref[0, rows, :], to_tail, axis=1)
        if cache_len > LANES:
            dst_ref[0, rows, 0:cache_len - LANES] = shifted[:, 0:cache_len - LANES]
        dst_ref[0, rows, cache_len - LANES:cache_len] = jnp.where(lane >= LANES - n_new, new_tail,
                                                                  shifted[:, cache_len - LANES:cache_len])

    return [functools.partial(job, r0) for r0 in range(0, features, CACHE_SHIFT_ROWS)]


def _ffn_kernel(*refs, pre_proj, final_norm, n_caches, n_new):
    refs = list(refs)
    x_ref = refs.pop(0)
    if pre_proj:
        a_ref, wp_ref = refs.pop(0), refs.pop(0)
    g_ref, wg_ref, wu_ref, wd_ref = refs.pop(0), refs.pop(0), refs.pop(0), refs.pop(0)
    if final_norm:
        fg_ref = refs.pop(0)
    cache_refs = [refs.pop(0) for _ in range(n_caches)]
    new_t_refs = [refs.pop(0) for _ in range(n_caches)]
    o_ref = refs.pop(0)
    cache_out_refs = refs

    side_jobs = []
    if n_caches:
        j = pl.program_id(0) % (LANES // n_new)
        to_tail = LANES - n_new - n_new * j
        per_cache = [_cache_shift_jobs(c, t, o, to_tail, n_new)
                     for c, t, o in zip(cache_refs, new_t_refs, cache_out_refs)]
        side_jobs = [job for jobs in zip(*per_cache) for job in jobs]
    n_pieces = FFN_SIDE_JOB_PIECES if side_jobs else 1

    x = x_ref[...]
    if pre_proj:
        x = x + _dot(a_ref[...], wp_ref[...])
    h = _rms(x, g_ref[...]).astype(BF16)
    bounds = [(D_FF // MXU_WIDTH * p // n_pieces) * MXU_WIDTH for p in range(n_pieces + 1)]
    acts = []
    for p in range(n_pieces):
        cols = slice(bounds[p], bounds[p + 1])
        gate = _dot(h, wg_ref[:, cols])
        up = _dot(h, wu_ref[:, cols])
        acts.append((gate * jax.nn.sigmoid(gate) * up).astype(BF16))
        for job in side_jobs[p::n_pieces]:
            job()
    act = acts[0] if n_pieces == 1 else jnp.concatenate(acts, axis=1)
    x = x + 0.5 * _dot(act, wd_ref[...])
    if final_norm:
        x = _rms(x, fg_ref[...])
    o_ref[...] = x


def _ffn(x, norm_g, wg, wu, wd, *, tm, pre=None, final_g=None, cache_update=None):
    rows = x.shape[0]
    args = [x]
    specs = [_rows_spec(tm, D_MODEL)]
    if pre is not None:
        a, wp = pre
        args += [a, wp]
        specs += [_rows_spec(tm, D_MODEL), _const_spec((D_MODEL, D_MODEL))]
    args += [norm_g, wg, wu, wd]
    specs += [_const_spec((1, D_MODEL)), _const_spec((D_MODEL, D_FF)), _const_spec((D_MODEL, D_FF)),
              _const_spec((D_FF, D_MODEL))]
    if final_g is not None:
        args.append(final_g)
        specs.append(_const_spec((1, D_MODEL)))
    out_specs = [_rows_spec(tm, D_MODEL)]
    out_shape = [jax.ShapeDtypeStruct((rows, D_MODEL), F32)]
    n_caches, n_new = 0, 0
    if cache_update is not None:
        caches, new_t, n_new = cache_update
        n_caches = len(caches)
        assert all(c.shape[0] == rows // tm for c in caches), "one grid step per cached sequence"
        per_tile = LANES // n_new
        cache_specs = [pl.BlockSpec((1,) + c.shape[1:], lambda i: (i, 0, 0)) for c in caches]
        args += list(caches) + list(new_t)
        specs += cache_specs + [pl.BlockSpec((1, t.shape[1], LANES), lambda i: (0, 0, i // per_tile)) for t in new_t]
        out_specs += cache_specs
        out_shape += [jax.ShapeDtypeStruct(c.shape, c.dtype) for c in caches]
    outs = pl.pallas_call(
        functools.partial(_ffn_kernel, pre_proj=pre is not None, final_norm=final_g is not None,
                          n_caches=n_caches, n_new=n_new),
        grid=(rows // tm,),
        in_specs=specs,
        out_specs=out_specs,
        out_shape=out_shape,
        compiler_params=_params(1),
        name="ffn",
    )(*args)
    return outs[0] if cache_update is None else outs


_Q_END = ATTN_WIDTH
_K_END = 2 * ATTN_WIDTH
_V_END = 3 * ATTN_WIDTH
_UA_END = _V_END + CONV_DIM
_UB_END = _UA_END + CONV_DIM
_GA_END = _UB_END + D_MODEL
IN_WIDTH = _GA_END + D_MODEL


def _win_kernel(*refs, t_groups, fuse_conv, tm, tiles_per_seq):
    x_ref, g_ref, w_ref = refs[:3]
    cw_ref = refs[3] if fuse_conv else None
    outs = refs[4:] if fuse_conv else refs[3:]
    q_ref, k_ref, v_ref, ga_ref, gb_ref = outs[:5]
    n_t = 2 * len(t_groups)
    t_refs = outs[5:5 + n_t]
    h = _rms(x_ref[...], g_ref[...]).astype(BF16)

    def seg(lo, hi):
        return _dot(h, w_ref[:, lo:hi])

    u = seg(_V_END, _UA_END) * jax.nn.sigmoid(seg(_UA_END, _UB_END))
    if fuse_conv:
        yc_ref, tail_ref, ext_ref = outs[5 + n_t:]
        first_tile = pl.program_id(0) % tiles_per_seq == 0

        @pl.when(first_tile)
        def _():
            ext_ref[0:CONV_HALO, :] = jnp.zeros((CONV_HALO, CONV_DIM), F32)

        @pl.when(jnp.logical_not(first_tile))
        def _():
            ext_ref[0:CONV_HALO, :] = ext_ref[tm:tm + CONV_HALO, :]

        ext_ref[CONV_HALO:CONV_HALO + tm, :] = u
        ext_ref[CONV_HALO + tm:CONV_HALO + tm + SUBLANES, :] = jnp.zeros((SUBLANES, CONV_DIM), F32)
        tail_ref[0] = u[tm - CONV_HALO:tm]
        conv_pieces = [(r0, l0) for r0 in range(0, tm, CONV_CHUNK_ROWS) for l0 in range(0, CONV_DIM, LANES)]
    else:
        outs[5 + n_t][...] = u
        conv_pieces = []
    n_proj_left = [3 * len(GROUPS) + 2 * (D_MODEL // GROUP_WIDTH)]

    def conv_step():
        n = -(-len(conv_pieces) // n_proj_left[0])
        n_proj_left[0] -= 1
        if n:
            _conv_taps_aligned(ext_ref, cw_ref, yc_ref, CONV_HALO - (CONV_WIDTH - 1), conv_pieces[:n])
            del conv_pieces[:n]

    for g in range(len(GROUPS)):
        lo = g * GROUP_WIDTH
        q = seg(lo, lo + GROUP_WIDTH) * (HEAD_DIM ** -0.5)
        conv_step()
        k = seg(_Q_END + lo, _Q_END + lo + GROUP_WIDTH)
        conv_step()
        v = seg(_K_END + lo, _K_END + lo + GROUP_WIDTH)
        conv_step()
        for hf in range(GROUP_WIDTH // LANES):
            lanes = slice(hf * LANES, (hf + 1) * LANES)
            q_ref[g, hf] = q[:, lanes]
            k_ref[g, hf] = k[:, lanes]
            v_ref[g, hf] = v[:, lanes]
        if g in t_groups:
            j = t_groups.index(g)
            t_refs[2 * j][0] = k.T
            t_refs[2 * j + 1][0] = v.T
    for gate_ref, first in ((ga_ref, _UB_END), (gb_ref, _GA_END)):
        for c in range(0, D_MODEL, GROUP_WIDTH):
            gate_ref[:, c:c + GROUP_WIDTH] = jax.nn.sigmoid(seg(first + c, first + c + GROUP_WIDTH)).astype(BF16)
            conv_step()


def _win(x, norm_g, w_in, *, tm, rows_per_seq, t_groups, conv_w=None):
    rows = x.shape[0]
    n_seq = rows // rows_per_seq
    n_t = 2 * len(t_groups)
    tiles_per_seq = rows_per_seq // tm
    fuse_conv = conv_w is not None
    n_g = len(GROUPS)
    n_hf = GROUP_WIDTH // LANES
    out_shape = ([jax.ShapeDtypeStruct((n_g, n_hf, rows, LANES), F32)] * 3
                 + [jax.ShapeDtypeStruct((rows, D_MODEL), BF16)] * 2
                 + [jax.ShapeDtypeStruct((n_seq, GROUP_WIDTH, rows_per_seq), F32)] * n_t
                 + [jax.ShapeDtypeStruct((rows, CONV_DIM), F32)])
    t_spec = pl.BlockSpec((1, GROUP_WIDTH, tm), lambda i: (i // tiles_per_seq, 0, i % tiles_per_seq))
    out_specs = ([pl.BlockSpec((n_g, n_hf, tm, LANES), lambda i: (0, 0, i, 0))] * 3
                 + [_rows_spec(tm, D_MODEL)] * 2 + [t_spec] * n_t
                 + [_rows_spec(tm, CONV_DIM)])
    in_specs = [_rows_spec(tm, D_MODEL), _const_spec((1, D_MODEL)), _const_spec((D_MODEL, IN_WIDTH))]
    args = [x, norm_g, w_in]
    scratch = []
    if fuse_conv:
        in_specs.append(_const_spec((CONV_WIDTH, CONV_DIM)))
        args.append(conv_w)
        out_shape.append(jax.ShapeDtypeStruct((n_seq, CONV_HALO, CONV_DIM), F32))
        out_specs.append(pl.BlockSpec((1, CONV_HALO, CONV_DIM), lambda i: (i // tiles_per_seq, 0, 0)))
        scratch.append(pltpu.VMEM((CONV_HALO + tm + SUBLANES, CONV_DIM), F32))
    return pl.pallas_call(
        functools.partial(_win_kernel, t_groups=tuple(t_groups), fuse_conv=fuse_conv, tm=tm,
                          tiles_per_seq=tiles_per_seq),
        grid=(rows // tm,),
        in_specs=in_specs,
        out_specs=out_specs,
        out_shape=out_shape,
        scratch_shapes=scratch,
        compiler_params=pltpu.CompilerParams(dimension_semantics=("arbitrary",),
                                             vmem_limit_bytes=V7X_VMEM_LIMIT_BYTES),
        name="w_in",
    )(*args)


def _rel_bucket(dist):
    n = jnp.maximum(dist, 0)
    max_exact = N_BUCKETS // 2
    nf = jnp.maximum(n, 1).astype(F32)
    large = max_exact + (jnp.log(nf / max_exact) / math.log(MAX_DISTANCE / max_exact)
                         * (N_BUCKETS - max_exact)).astype(jnp.int32)
    return jnp.where(n < max_exact, n, jnp.minimum(large, N_BUCKETS - 1))


def _tap_bias(rel_bias, g, dil):
    bias_g = rel_bias[:, g * HEADS_PER_GROUP:(g + 1) * HEADS_PER_GROUP]
    return bias_g[_rel_bucket(jnp.arange(SPAN, -1, -1) * dil)].astype(F32).T


def _toeplitz(vec, n_rows, n_cols):
    n_heads, period = vec.shape
    flat = jnp.tile(vec, (1, n_rows))[:, :n_rows * (period - 1)]
    return flat.reshape(n_heads, n_rows, period - 1)[:, :, :n_cols]


def _band_bias(tap_bias):
    vec = jnp.concatenate([tap_bias, jnp.full((HEADS_PER_GROUP, 2 * SPAN - 1), NEG_INF, F32)], axis=1)
    return _toeplitz(vec, SPAN, 2 * SPAN).reshape(HEADS_PER_GROUP * SPAN, 2 * SPAN)


def _cache_bias(tap_bias, dil, cache_len, n_new):
    rev = lax.pad(tap_bias, jnp.float32(NEG_INF), [(0, 0, 0), (cache_len + n_new - 1 - SPAN * dil, 0, dil - 1)])
    bias_c = _toeplitz(jnp.roll(rev, -(n_new - 1), axis=1), n_new, cache_len)
    near = jnp.concatenate([rev[:, cache_len:], jnp.full((HEADS_PER_GROUP, LANES), NEG_INF, F32)], axis=1)
    bias_n = _toeplitz(jnp.roll(near, -(n_new - 1), axis=1), n_new, LANES)
    return (bias_c.reshape(HEADS_PER_GROUP * n_new, cache_len), bias_n.reshape(HEADS_PER_GROUP * n_new, LANES))


MAX_ROW_STRIDE = 4


def _gather_residues(src, store, tmp_ref, dil, seq):
    if dil == 1:
        store(0, src[...])
    elif dil <= MAX_ROW_STRIDE:
        for r in range(dil):
            store(r, src[pl.ds(r, seq, stride=dil), :])
    else:
        inner = dil // MAX_ROW_STRIDE
        assert inner <= MAX_ROW_STRIDE and inner * MAX_ROW_STRIDE == dil
        n_part = seq * inner
        for c in range(MAX_ROW_STRIDE):
            tmp_ref[c * n_part:(c + 1) * n_part, :] = src[pl.ds(c, n_part, stride=MAX_ROW_STRIDE), :]
        for c in range(MAX_ROW_STRIDE):
            for c2 in range(inner):
                store(c + MAX_ROW_STRIDE * c2, tmp_ref[pl.ds(c * n_part + c2, seq, stride=inner), :])


def _scatter_residues(load, dst, tmp_ref, dil, seq):
    if dil == 1:
        dst[...] = load(0)
    elif dil <= MAX_ROW_STRIDE:
        for r in range(dil):
            dst[pl.ds(r, seq, stride=dil), :] = load(r)
    else:
        inner = dil // MAX_ROW_STRIDE
        assert inner <= MAX_ROW_STRIDE and inner * MAX_ROW_STRIDE == dil
        n_part = seq * inner
        for c in range(MAX_ROW_STRIDE):
            for c2 in range(inner):
                tmp_ref[pl.ds(c * n_part + c2, seq, stride=inner), :] = load(c + MAX_ROW_STRIDE * c2)
        for c in range(MAX_ROW_STRIDE):
            dst[pl.ds(c, n_part, stride=MAX_ROW_STRIDE), :] = tmp_ref[c * n_part:(c + 1) * n_part, :]


def _merge_groups(outs, lses):
    m = functools.reduce(jnp.maximum, lses)
    es = [jnp.exp(l - m) for l in lses]
    num = sum(e * o for e, o in zip(es, outs))
    return num * (1.0 / sum(es))


def _band_group(q_lo, q_hi, k_lo, k_hi, v_lo, v_hi, b_ref, o_dst, l_dst, qb_ref, kb_ref, vb_ref, ob_ref, lb_ref,
                tmp_ref, *, dil, seq):
    n_blk = seq // SPAN
    halves = (slice(0, LANES), slice(LANES, 2 * LANES))
    for dst, srcs in ((qb_ref, (q_lo, q_hi)), (kb_ref, (k_lo, k_hi)), (vb_ref, (v_lo, v_hi))):
        for lanes, src in zip(halves, srcs):
            def store(r, rows, dst=dst, lanes=lanes):
                dst[r * seq:(r + 1) * seq, lanes] = rows.astype(BF16)
            _gather_residues(src, store, tmp_ref, dil, seq)
    masks = _head_masks()

    def block(q, kb, vb, key_lo):
        qm = jnp.concatenate([jnp.where(masks[h], q, jnp.zeros_like(q)) for h in range(HEADS_PER_GROUP)], axis=0)
        s = _dot_nt(qm, kb) + b_ref[:, key_lo:]
        m = jnp.max(s, axis=-1, keepdims=True)
        p = jnp.exp(s - m)
        den = jnp.sum(p, axis=-1, keepdims=True)
        acc = _dot(p.astype(BF16), vb) * (1.0 / den)
        lse_rows = m + jnp.log(den)
        o = acc[0:SPAN]
        lse = jnp.broadcast_to(lse_rows[0:SPAN], (SPAN, GROUP_WIDTH))
        for h in range(1, HEADS_PER_GROUP):
            o = jnp.where(masks[h], acc[h * SPAN:(h + 1) * SPAN], o)
            lse = jnp.where(masks[h], lse_rows[h * SPAN:(h + 1) * SPAN], lse)
        return o, lse

    for r in range(dil):
        for j in range(n_blk):
            q_rows = slice(r * seq + j * SPAN, r * seq + (j + 1) * SPAN)
            k_rows = slice(r * seq + max(j - 1, 0) * SPAN, r * seq + (j + 1) * SPAN)
            o, lse = block(qb_ref[q_rows, :], kb_ref[k_rows, :], vb_ref[k_rows, :], SPAN if j == 0 else 0)
            ob_ref[q_rows, :] = o
            lb_ref[q_rows, :] = lse

    for src, dsts in ((ob_ref, o_dst), (lb_ref, l_dst)):
        for lanes, dst in zip(halves, dsts):
            _scatter_residues(lambda r, src=src, lanes=lanes: src[r * seq:(r + 1) * seq, lanes], dst, tmp_ref, dil, seq)


MERGE_CHUNK_ROWS = 256


def _band_attn_kernel(q_lo, q_hi, k_lo, k_hi, v_lo, v_hi, b_ref, out_ref,
                      qb_ref, kb_ref, vb_ref, ob_ref, lb_ref, tmp_ref, *nat_refs, seq_len):
    n_half = 2 * len(GROUPS)
    on_refs, ln_refs = nat_refs[:n_half], nat_refs[n_half:]
    g = pl.program_id(1)
    for gi, (_, dil) in enumerate(GROUPS):
        @pl.when(g == gi)
        def _(gi=gi, dil=dil):
            _band_group(q_lo, q_hi, k_lo, k_hi, v_lo, v_hi, b_ref,
                        on_refs[2 * gi:2 * gi + 2], ln_refs[2 * gi:2 * gi + 2],
                        qb_ref, kb_ref, vb_ref, ob_ref, lb_ref, tmp_ref, dil=dil, seq=seq_len // dil)

    @pl.when(g == len(GROUPS) - 1)
    def _():
        for r0 in range(0, seq_len, MERGE_CHUNK_ROWS):
            rows = slice(r0, r0 + MERGE_CHUNK_ROWS)
            for hf in (0, 1):
                merged = _merge_groups([on_refs[2 * gi + hf][rows, :] for gi in range(len(GROUPS))],
                                       [ln_refs[2 * gi + hf][rows, :] for gi in range(len(GROUPS))])
                out_ref[rows, hf * LANES:(hf + 1) * LANES] = merged.astype(BF16)


def _band_attn(q, k, v, biases, *, batch, seq_len):
    n_g = len(GROUPS)
    view = lambda a: a.reshape(n_g, 2, batch, seq_len, LANES)
    in_halves = [pl.BlockSpec((None, None, None, seq_len, LANES), lambda b, g, hf=hf: (g, hf, b, 0, 0))
                 for hf in (0, 1)]
    out = pl.pallas_call(
        functools.partial(_band_attn_kernel, seq_len=seq_len),
        grid=(batch, n_g),
        in_specs=in_halves * 3 + [pl.BlockSpec((None, HEADS_PER_GROUP * SPAN, 2 * SPAN), lambda b, g: (g, 0, 0))],
        out_specs=pl.BlockSpec((None, seq_len, GROUP_WIDTH), lambda b, g: (b, 0, 0)),
        out_shape=jax.ShapeDtypeStruct((batch, seq_len, GROUP_WIDTH), BF16),
        scratch_shapes=([pltpu.VMEM((seq_len, GROUP_WIDTH), BF16)] * 3 + [pltpu.VMEM((seq_len, GROUP_WIDTH), F32)] * 2
                        + [pltpu.VMEM((seq_len, LANES), F32)]
                        + [pltpu.VMEM((seq_len, LANES), F32)] * (4 * n_g)),
        compiler_params=pltpu.CompilerParams(dimension_semantics=("parallel", "arbitrary"),
                                             vmem_limit_bytes=V7X_VMEM_LIMIT_BYTES),
        name="band_attn",
    )(view(q), view(q), view(k), view(k), view(v), view(v), biases)
    return out.reshape(batch * seq_len, GROUP_WIDTH)


def _cache_attn_kernel(*refs, n_new):
    n_g = len(GROUPS)
    q_refs, kn_refs, vn_refs = refs[0:n_g], refs[n_g:2 * n_g], refs[2 * n_g:3 * n_g]
    kc_refs, vc_refs = refs[3 * n_g:4 * n_g], refs[4 * n_g:5 * n_g]
    bc_refs, bn_refs = refs[5 * n_g:6 * n_g], refs[6 * n_g:7 * n_g]
    attn_ref = refs[7 * n_g]
    group_out, group_lse = [], []
    masks = _head_masks()
    pad = jnp.zeros((SPAN - n_new, GROUP_WIDTH), F32)

    for g in range(n_g):
        kc, vc = kc_refs[g][0], vc_refs[g][0]
        kn, vn = kn_refs[g][0], vn_refs[g][0]
        q = q_refs[g][0]
        qm = jnp.concatenate([jnp.where(masks[h], q, 0.0) for h in range(HEADS_PER_GROUP)], axis=0).astype(BF16)
        kn_p = jnp.concatenate([kn, pad], axis=0).astype(BF16)
        vn_p = jnp.concatenate([vn, pad], axis=0).astype(BF16)
        s_c = _dot(qm, kc.astype(BF16)) + bc_refs[g][...]
        s_n = _dot_nt(qm, kn_p) + bn_refs[g][...]
        m = jnp.maximum(jnp.max(s_c, axis=-1, keepdims=True), jnp.max(s_n, axis=-1, keepdims=True))
        p_c = jnp.exp(s_c - m)
        p_n = jnp.exp(s_n - m)
        den = jnp.sum(p_c, axis=-1, keepdims=True) + jnp.sum(p_n, axis=-1, keepdims=True)
        acc = _dot_nt(p_c.astype(BF16), vc.astype(BF16)) + _dot(p_n.astype(BF16), vn_p)
        acc = acc * (1.0 / den)
        lse_rows = m + jnp.log(den)
        o = jnp.zeros((n_new, GROUP_WIDTH), F32)
        lse = jnp.zeros((n_new, GROUP_WIDTH), F32)
        for h in range(HEADS_PER_GROUP):
            rows = slice(h * n_new, (h + 1) * n_new)
            o = jnp.where(masks[h], acc[rows], o)
            lse = jnp.where(masks[h], lse_rows[rows], lse)
        group_out.append(o)
        group_lse.append(lse)
    attn_ref[0] = _merge_groups(group_out, group_lse).astype(BF16)


def _cache_attn(q, k_new, v_new, k_cache, v_cache, tap_biases, *, batch, n_new):
    n_g = len(GROUPS)
    new3 = lambda a: a.reshape(batch, n_new, GROUP_WIDTH)
    new_spec = pl.BlockSpec((1, n_new, GROUP_WIDTH), lambda b: (b, 0, 0))
    cache_specs = [pl.BlockSpec((1, GROUP_WIDTH, c.shape[2]), lambda b: (b, 0, 0)) for c in k_cache]
    biases_c, biases_n = [], []
    for g, (_, dil) in enumerate(GROUPS):
        bc, bn = _cache_bias(tap_biases[g], dil, k_cache[g].shape[2], n_new)
        biases_c.append(bc)
        biases_n.append(bn)
    args = ([new3(a) for a in q] + [new3(a) for a in k_new] + [new3(a) for a in v_new]
            + list(k_cache) + list(v_cache) + biases_c + biases_n)
    in_specs = ([new_spec] * (3 * n_g) + cache_specs * 2
                + [_const_spec(b.shape) for b in biases_c] + [_const_spec(b.shape) for b in biases_n])
    attn = pl.pallas_call(
        functools.partial(_cache_attn_kernel, n_new=n_new),
        grid=(batch,),
        in_specs=in_specs,
        out_specs=new_spec,
        out_shape=jax.ShapeDtypeStruct((batch, n_new, GROUP_WIDTH), BF16),
        compiler_params=_params(1),
        name="cache_attn",
    )(*args)
    return attn.reshape(batch * n_new, GROUP_WIDTH)


def _conv_taps(ext_ref, w_ref, first_row, n_rows):
    acc = jnp.zeros((n_rows, CONV_DIM), F32)
    for j in range(CONV_WIDTH):
        acc = acc + ext_ref[pl.ds(first_row + j, n_rows), :] * w_ref[j:j + 1, :]
    return acc


CONV_CHUNK_ROWS = 64


def _conv_taps_aligned(ext_ref, w_ref, y_ref, first_row, pieces):
    for r0, l0 in pieces:
        lanes = slice(l0, l0 + LANES)
        acc = None
        for b in range(SUBLANES):
            part = None
            for a in range((first_row + CONV_WIDTH - 1 - b) // SUBLANES + 1):
                j = SUBLANES * a + b - first_row
                if j < 0:
                    continue
                term = ext_ref[pl.ds(r0 + SUBLANES * a, CONV_CHUNK_ROWS + SUBLANES), lanes] * w_ref[j:j + 1, lanes]
                part = term if part is None else part + term
            part = part[b:b + CONV_CHUNK_ROWS]
            acc = part if acc is None else acc + part
        y_ref[r0:r0 + CONV_CHUNK_ROWS, lanes] = acc


def _conv_sample_kernel(u_ref, st_ref, w_ref, y_ref, so_ref, ext_ref, *, n_new, n_seq):
    hist = CONV_WIDTH - 1
    for b in range(n_seq):
        ext_ref[b, 0:hist, :] = st_ref[b]
        ext_ref[b, hist:hist + n_new, :] = u_ref[b]
        y_ref[b] = _conv_taps(ext_ref.at[b], w_ref, 0, n_new)
        so_ref[b] = ext_ref[b, n_new:n_new + hist, :]


def _conv_sample(u, state, w, *, batch, n_new, seqs_per_step=8):
    hist = CONV_WIDTH - 1
    u3 = u.reshape(batch, n_new, CONV_DIM)
    n_seq = math.gcd(batch, seqs_per_step)
    y, st = pl.pallas_call(
        functools.partial(_conv_sample_kernel, n_new=n_new, n_seq=n_seq),
        grid=(batch // n_seq,),
        in_specs=[pl.BlockSpec((n_seq, n_new, CONV_DIM), lambda b: (b, 0, 0)),
                  pl.BlockSpec((n_seq, hist, CONV_DIM), lambda b: (b, 0, 0)),
                  pl.BlockSpec((CONV_WIDTH, CONV_DIM), lambda b: (0, 0))],
        out_specs=[pl.BlockSpec((n_seq, n_new, CONV_DIM), lambda b: (b, 0, 0)),
                   pl.BlockSpec((n_seq, hist, CONV_DIM), lambda b: (b, 0, 0))],
        out_shape=[jax.ShapeDtypeStruct((batch, n_new, CONV_DIM), F32),
                   jax.ShapeDtypeStruct((batch, hist, CONV_DIM), F32)],
        scratch_shapes=[pltpu.VMEM((n_seq, hist + n_new + 2, CONV_DIM), F32)],
        compiler_params=_params(1),
        name="conv_sample",
    )(u3, state, w)
    return y.reshape(batch * n_new, CONV_DIM), st


def _mix_branches(attn, yconv, gate_a, gate_b, x, wap_ref, cb_ref, lg_ref, lb_ref, wcp_ref, wo_ref):
    a_branch = _dot(attn, wap_ref[...])
    y = yconv + cb_ref[...]
    mu = jnp.mean(y, axis=-1, keepdims=True)
    yc = y - mu
    var = jnp.mean(yc * yc, axis=-1, keepdims=True)
    y = yc * lax.rsqrt(var + EPS) * lg_ref[...] + lb_ref[...]
    y = y * jax.nn.sigmoid(y)
    c_branch = _dot(y.astype(BF16), wcp_ref[...])
    merged = gate_a * a_branch + gate_b * c_branch
    return x + _dot(merged.astype(BF16), wo_ref[...])


def _xattn_query(x, xg_ref, wxq_ref):
    hq = _rms(x, xg_ref[...]).astype(BF16)
    return (_dot(hq, wxq_ref[...]) * (X_HEAD_DIM ** -0.5)).astype(BF16)


def _xattn_heads(q, k_ref, v_ref):
    outs = []
    for h in range(X_HEADS):
        lanes = slice(h * X_HEAD_DIM, (h + 1) * X_HEAD_DIM)
        s = _dot_nt(q[:, lanes], k_ref[:, lanes])
        m = jnp.max(s, axis=-1, keepdims=True)
        p = jnp.exp(s - m)
        den = jnp.sum(p, axis=-1, keepdims=True)
        outs.append((_dot(p.astype(BF16), v_ref[:, lanes]) * (1.0 / den)).astype(BF16))
    return jnp.concatenate(outs, axis=-1)


def _mix_sample_kernel(at_ref, yc_ref, ga_ref, gb_ref, x_ref, wap_ref, cb_ref, lg_ref, lb_ref, wcp_ref, wo_ref,
                       xg_ref, wxq_ref, x_out, q_out):
    x = _mix_branches(at_ref[...], yc_ref[...], ga_ref[...], gb_ref[...], x_ref[...],
                      wap_ref, cb_ref, lg_ref, lb_ref, wcp_ref, wo_ref)
    x_out[...] = x
    q_out[...] = _xattn_query(x, xg_ref, wxq_ref)


def _mix_prompt_kernel(at_ref, yc_ref, ga_ref, gb_ref, x_ref, wap_ref, cb_ref, lg_ref, lb_ref, wcp_ref, wo_ref,
                       xg_ref, wxq_ref, mk_ref, mv_ref, wxo_ref, x_out):
    x = _mix_branches(at_ref[...], yc_ref[...], ga_ref[...], gb_ref[...], x_ref[...],
                      wap_ref, cb_ref, lg_ref, lb_ref, wcp_ref, wo_ref)
    attn = _xattn_heads(_xattn_query(x, xg_ref, wxq_ref), mk_ref, mv_ref)
    x_out[...] = x + _dot(attn, wxo_ref[...])


def _mix_weight_specs():
    return [_const_spec((GROUP_WIDTH, D_MODEL)), _const_spec((1, CONV_DIM)), _const_spec((1, CONV_DIM)),
            _const_spec((1, CONV_DIM)), _const_spec((CONV_DIM, D_MODEL)), _const_spec((D_MODEL, D_MODEL)),
            _const_spec((1, D_MODEL)), _const_spec((D_MODEL, D_MODEL))]


def _mix_weights(w):
    return (w['w_attn_proj'], w['conv_dw_b'], w['conv_ln_g'], w['conv_ln_b'], w['w_conv_proj'], w['w_o'],
            w['xattn_norm'], w['w_xq'])


def _mix_sample(attn, yconv, ga, gb, x, w, *, tm):
    rows = x.shape[0]
    wide = _rows_spec(tm, D_MODEL)
    return pl.pallas_call(
        _mix_sample_kernel,
        grid=(rows // tm,),
        in_specs=[_rows_spec(tm, GROUP_WIDTH), _rows_spec(tm, CONV_DIM), wide, wide, wide] + _mix_weight_specs(),
        out_specs=[wide, wide],
        out_shape=[jax.ShapeDtypeStruct((rows, D_MODEL), F32), jax.ShapeDtypeStruct((rows, D_MODEL), BF16)],
        compiler_params=_params(1),
        name="mix_sample",
    )(attn, yconv, ga, gb, x, *_mix_weights(w))


def _mix_prompt(attn, yconv, ga, gb, x, mk, mv, w, *, tm, seq_len):
    rows = x.shape[0]
    tiles_per_seq = seq_len // tm
    wide = _rows_spec(tm, D_MODEL)
    mem = pl.BlockSpec((N_MEM, D_MODEL), lambda i: (i // tiles_per_seq, 0))
    return pl.pallas_call(
        _mix_prompt_kernel,
        grid=(rows // tm,),
        in_specs=([_rows_spec(tm, GROUP_WIDTH), _rows_spec(tm, CONV_DIM), wide, wide, wide]
                  + _mix_weight_specs() + [mem, mem, _const_spec((D_MODEL, D_MODEL))]),
        out_specs=wide,
        out_shape=jax.ShapeDtypeStruct((rows, D_MODEL), F32),
        compiler_params=_params(1),
        name="mix_prompt",
    )(attn, yconv, ga, gb, x, *_mix_weights(w), mk, mv, w['w_xo'])


def _memkv_kernel(m_ref, g_ref, w_ref, k5_ref, v5_ref, k_ref, v_ref, *, n_seq):
    h = _rms(m_ref[...], g_ref[...]).astype(BF16)
    for hd in range(X_HEADS):
        lo = hd * X_HEAD_DIM
        k = _dot(h, w_ref[:, lo:lo + X_HEAD_DIM])
        v = _dot(h, w_ref[:, D_MODEL + lo:D_MODEL + lo + X_HEAD_DIM])
        for b in range(n_seq):
            k5_ref[0, b, :, hd, :] = k[b * N_MEM:(b + 1) * N_MEM]
            v5_ref[0, b, :, hd, :] = v[b * N_MEM:(b + 1) * N_MEM]
        k_ref[:, lo:lo + X_HEAD_DIM] = k.astype(BF16)
        v_ref[:, lo:lo + X_HEAD_DIM] = v.astype(BF16)


def _memkv(mem, norm_g, w_xkv, *, batch, seqs_per_step=4):
    n_seq = math.gcd(batch, seqs_per_step)
    spec5 = pl.BlockSpec((1, n_seq, N_MEM, X_HEADS, X_HEAD_DIM), lambda b: (0, b, 0, 0, 0))
    rows = _rows_spec(n_seq * N_MEM, D_MODEL)
    return pl.pallas_call(
        functools.partial(_memkv_kernel, n_seq=n_seq),
        grid=(batch // n_seq,),
        in_specs=[rows, _const_spec((1, D_MODEL)), _const_spec((D_MODEL, 2 * D_MODEL))],
        out_specs=[spec5, spec5, rows, rows],
        out_shape=([jax.ShapeDtypeStruct((1, batch, N_MEM, X_HEADS, X_HEAD_DIM), F32)] * 2
                   + [jax.ShapeDtypeStruct((batch * N_MEM, D_MODEL), BF16)] * 2),
        compiler_params=_params(1),
        name="memory_kv",
    )(mem, norm_g, w_xkv)


def _xattn_cached_kernel(q_ref, k_lo, k_hi, v_lo, v_hi, mask_ref, o_ref, *, n_q, n_seq):
    for b in range(n_seq):
        flat = lambda ref: ref[0, b].reshape(N_MEM * X_HEADS, LANES)
        k_all = jnp.concatenate([flat(k_lo), flat(k_hi)], axis=1).astype(BF16)
        v_all = jnp.concatenate([flat(v_lo), flat(v_hi)], axis=1).astype(BF16)
        q = q_ref[b].astype(F32)
        qs = jnp.concatenate([q[:, h * X_HEAD_DIM:(h + 1) * X_HEAD_DIM] for h in range(X_HEADS)],
                             axis=0).astype(BF16)
        s = _dot_nt(qs, k_all) + mask_ref[...]
        m = jnp.max(s, axis=-1, keepdims=True)
        p = jnp.exp(s - m)
        den = jnp.sum(p, axis=-1, keepdims=True)
        acc = _dot(p.astype(BF16), v_all) * (1.0 / den)
        o_ref[b] = jnp.concatenate([acc[h * n_q:(h + 1) * n_q] for h in range(X_HEADS)], axis=1).astype(BF16)


def _xattn_cached(q, mk, mv, *, batch, seqs_per_step=4):
    n_q = q.shape[1]
    n_seq = math.gcd(batch, seqs_per_step)
    q_spec = pl.BlockSpec((n_seq, n_q, D_MODEL), lambda b: (b, 0, 0))
    halves = [pl.BlockSpec((1, n_seq, N_MEM, X_HEADS, LANES), lambda b, hf=hf: (0, b, 0, 0, hf))
              for hf in range(X_HEAD_DIM // LANES)]
    own_head = (np.arange(N_MEM * X_HEADS)[None, :] % X_HEADS) == (np.arange(X_HEADS * n_q)[:, None] // n_q)
    mask = jnp.asarray(np.where(own_head, 0.0, NEG_INF), F32)
    return pl.pallas_call(
        functools.partial(_xattn_cached_kernel, n_q=n_q, n_seq=n_seq),
        grid=(batch // n_seq,),
        in_specs=[q_spec] + halves * 2 + [_const_spec(mask.shape)],
        out_specs=q_spec,
        out_shape=jax.ShapeDtypeStruct((batch, n_q, D_MODEL), BF16),
        compiler_params=_params(1),
        name="cross_attn_cached",
    )(q, mk, mk, mv, mv, mask)


def _to_positions_heads(a_t, batch, n_pos):
    return jnp.transpose(a_t.reshape(batch, HEADS_PER_GROUP, HEAD_DIM, n_pos), (0, 3, 1, 2))[None]


def _to_feature_major(a, batch, n_pos):
    return jnp.transpose(a[0], (0, 2, 3, 1)).reshape(batch, GROUP_WIDTH, n_pos)


def kernel(x_prompt, x_sample, mem_prompt, cache_win0_k, cache_win0_v, cache_win1_k, cache_win1_v, cache_win2_k, cache_win2_v, state_conv, cache_mem_k, cache_mem_v, rel_bias, ffn1_norm, ffn1_w_gate, ffn1_w_up, ffn1_w_down, mix_norm, w_in, w_attn_proj, conv_dw_w, conv_dw_b, conv_ln_g, conv_ln_b, w_conv_proj, w_o, xattn_norm, mem_norm, w_xq, w_xkv, w_xo, ffn2_norm, ffn2_w_gate, ffn2_w_up, ffn2_w_down, final_norm):
    batch, seq_len, _ = x_prompt.shape
    dec_batch, dec_seq, _ = x_sample.shape
    n_g = len(GROUPS)
    assert ffn1_norm.shape[0] == 1, "single layer"
    mat = lambda a: a[0].astype(BF16)
    vec = lambda a: a[0].reshape(1, -1)
    w = dict(ffn1_norm=vec(ffn1_norm), ffn1_w_gate=mat(ffn1_w_gate), ffn1_w_up=mat(ffn1_w_up),
             ffn1_w_down=mat(ffn1_w_down), mix_norm=vec(mix_norm), w_in=mat(w_in),
             w_attn_proj=mat(w_attn_proj), conv_dw_w=conv_dw_w[0], conv_dw_b=vec(conv_dw_b),
             conv_ln_g=vec(conv_ln_g), conv_ln_b=vec(conv_ln_b), w_conv_proj=mat(w_conv_proj),
             w_o=mat(w_o), xattn_norm=vec(xattn_norm), mem_norm=vec(mem_norm), w_xq=mat(w_xq),
             w_xkv=mat(w_xkv), w_xo=mat(w_xo), ffn2_norm=vec(ffn2_norm), ffn2_w_gate=mat(ffn2_w_gate),
             ffn2_w_up=mat(ffn2_w_up), ffn2_w_down=mat(ffn2_w_down))
    final_g = final_norm.reshape(1, -1)
    tap_biases = [_tap_bias(rel_bias, g, dil) for g, (_, dil) in enumerate(GROUPS)]

    def head(x, tm, rows_per_seq, t_groups, conv_w=None, cache_update=None):
        x1 = _ffn(x, w['ffn1_norm'], w['ffn1_w_gate'], w['ffn1_w_up'], w['ffn1_w_down'], tm=tm,
                  cache_update=cache_update)
        new_caches = []
        if cache_update is not None:
            x1, *new_caches = x1
        outs = _win(x1, w['mix_norm'], w['w_in'], tm=tm, rows_per_seq=rows_per_seq, t_groups=t_groups, conv_w=conv_w)
        n_t = 2 * len(t_groups)
        return x1, outs[0], outs[1], outs[2], outs[3], outs[4], outs[5:5 + n_t], outs[5 + n_t:], new_caches

    rows_s = dec_batch * dec_seq
    x1_s, q_s, k_s, v_s, ga_s, gb_s, kv_t_s, (u_s,), _ = head(x_sample.reshape(rows_s, D_MODEL), rows_s, rows_s,
                                                              list(range(n_g)))
    caches_k = [_to_feature_major(c, dec_batch, c.shape[2]) for c in (cache_win0_k, cache_win1_k, cache_win2_k)]
    caches_v = [_to_feature_major(c, dec_batch, c.shape[2]) for c in (cache_win0_v, cache_win1_v, cache_win2_v)]

    rows_p = batch * seq_len
    tm_p = PROMPT_ROW_TILE
    full_groups = [g for g, (window, _) in enumerate(GROUPS) if window >= seq_len]
    cache_update = (caches_k + caches_v, list(kv_t_s[0::2]) + list(kv_t_s[1::2]), dec_seq)
    x1, q, k, v, ga, gb, kv_t, (yconv, u_tail), new_caches = head(
        x_prompt.reshape(rows_p, D_MODEL), tm_p, seq_len, full_groups, conv_w=w['conv_dw_w'], cache_update=cache_update)
    attn = _band_attn(q, k, v, jnp.stack([_band_bias(t) for t in tap_biases]), batch=batch, seq_len=seq_len)
    p_mem_k, p_mem_v, mk_p, mv_p = _memkv(mem_prompt.reshape(batch * N_MEM, D_MODEL), w['mem_norm'], w['w_xkv'],
                                          batch=batch)
    x3 = _mix_prompt(attn, yconv, ga, gb, x1, mk_p, mv_p, w, tm=tm_p, seq_len=seq_len)
    y_prompt = _ffn(x3, w['ffn2_norm'], w['ffn2_w_gate'], w['ffn2_w_up'], w['ffn2_w_down'], tm=tm_p,
                    final_g=final_g)

    p_win = []
    for g, (window, _) in enumerate(GROUPS):
        keep = min(window, seq_len)
        if g in full_groups:
            j = full_groups.index(g)
            p_win += [_to_positions_heads(kv_t[2 * j], batch, seq_len), _to_positions_heads(kv_t[2 * j + 1], batch, seq_len)]
        else:
            for a in (k, v):
                tail = lax.slice(a.reshape(n_g, 2, batch, seq_len, LANES), (g, 0, 0, seq_len - keep, 0),
                                 (g + 1, 2, batch, seq_len, LANES))
                tail = jnp.transpose(tail[0], (1, 2, 0, 3))
                p_win.append(tail.reshape(1, batch, keep, HEADS_PER_GROUP, HEAD_DIM))
    p_conv = u_tail[:, CONV_HALO - (CONV_WIDTH - 1):][None]

    groups = lambda a: [jnp.concatenate([a[g, 0], a[g, 1]], axis=-1) for g in range(n_g)]
    attn = _cache_attn(groups(q_s), groups(k_s), groups(v_s), caches_k, caches_v, tap_biases,
                       batch=dec_batch, n_new=dec_seq)
    yconv, s_conv = _conv_sample(u_s, state_conv[0], w['conv_dw_w'], batch=dec_batch, n_new=dec_seq)
    x2, xq = _mix_sample(attn, yconv, ga_s, gb_s, x1_s, w, tm=rows_s)
    xo = _xattn_cached(xq.reshape(dec_batch, dec_seq, D_MODEL), cache_mem_k, cache_mem_v, batch=dec_batch)
    y_sample = _ffn(x2, w['ffn2_norm'], w['ffn2_w_gate'], w['ffn2_w_up'], w['ffn2_w_down'], tm=rows_s,
                    pre=(xo.reshape(rows_s, D_MODEL), w['w_xo']), final_g=final_g)

    s_win = []
    for g in range(n_g):
        for a in (new_caches[g], new_caches[n_g + g]):
            s_win.append(_to_positions_heads(a, dec_batch, a.shape[2]))

    return (y_prompt.reshape(batch, seq_len, D_MODEL), y_sample.reshape(dec_batch, dec_seq, D_MODEL),
            *p_win, p_conv, p_mem_k, p_mem_v, *s_win, s_conv[None])
```

```python
import functools
import math

import jax
import jax.numpy as jnp
import numpy as np
from jax import lax
from jax.experimental import pallas as pl
from jax.experimental.pallas import tpu as pltpu

D_MODEL = 1024
HEAD_DIM = 64
GROUPS = ((128, 1), (512, 4), (2048, 16))
HEADS_PER_GROUP = 4
GROUP_WIDTH = HEADS_PER_GROUP * HEAD_DIM
ATTN_WIDTH = len(GROUPS) * GROUP_WIDTH
CONV_DIM = D_MODEL // 2
CONV_WIDTH = 31
CONV_HALO = 32
N_BUCKETS = 32
MAX_DISTANCE = 2048
D_FF = ((8 * D_MODEL // 3 + 127) // 128) * 128
N_MEM = 256
X_HEADS = 4
X_HEAD_DIM = D_MODEL // X_HEADS
EPS = 1e-6
NEG_INF = -1e30
SPAN = 128
LANES = 128
SUBLANES = 8

V7X_VMEM_LIMIT_BYTES = 56 * 1024 * 1024
PROMPT_ROW_TILE = 512
BF16 = jnp.bfloat16
F32 = jnp.float32


def _params(n_axes):
    return pltpu.CompilerParams(dimension_semantics=("parallel",) * n_axes,
                                vmem_limit_bytes=V7X_VMEM_LIMIT_BYTES)


def _const_spec(shape):
    return pl.BlockSpec(shape, lambda *_: (0,) * len(shape), pipeline_mode=pl.Buffered(1))


def _rows_spec(tm, width):
    return pl.BlockSpec((tm, width), lambda i: (i, 0))


def _rms(x, g):
    return x * lax.rsqrt(jnp.mean(x * x, axis=-1, keepdims=True) + EPS) * g


def _dot(a, b):
    return jnp.dot(a, b, preferred_element_type=F32)


def _dot_nt(a, b):
    return lax.dot_general(a, b, (((1,), (1,)), ((), ())), preferred_element_type=F32)


def _head_masks(width=GROUP_WIDTH):
    lane = lax.broadcasted_iota(jnp.int32, (1, width), 1)
    return [(lane >= h * HEAD_DIM) & (lane < (h + 1) * HEAD_DIM) for h in range(HEADS_PER_GROUP)]


MXU_WIDTH = 256
FFN_SIDE_JOB_PIECES = 4
CACHE_SHIFT_ROWS = 64


def _cache_shift_jobs(src_ref, new_t_ref, dst_ref, to_tail, n_new):
    features, cache_len = src_ref.shape[1:]
    lane = lax.broadcasted_iota(jnp.int32, (1, LANES), 1)

    def job(r0):
        rows = slice(r0, r0 + CACHE_SHIFT_ROWS)
        shifted = pltpu.roll(src_ref[0, rows, :], cache_len - n_new, axis=1)
        new_tail = pltpu.roll(new_t_ref[0, rows, :], to_tail, axis=1)
        if cache_len > LANES:
            dst_ref[0, rows, 0:cache_len - LANES] = shifted[:, 0:cache_len - LANES]
        dst_ref[0, rows, cache_len - LANES:cache_len] = jnp.where(lane >= LANES - n_new, new_tail,
                                                                  shifted[:, cache_len - LANES:cache_len])

    return [functools.partial(job, r0) for r0 in range(0, features, CACHE_SHIFT_ROWS)]


def _ffn_kernel(*refs, pre_proj, final_norm, n_caches, n_new):
    n_g = n_caches // 2
    refs = list(refs)
    x_ref = refs.pop(0)
    if pre_proj:
        a_ref, wp_ref = refs.pop(0), refs.pop(0)
    g_ref, wg_ref, wu_ref, wd_ref = refs.pop(0), refs.pop(0), refs.pop(0), refs.pop(0)
    if final_norm:
        fg_ref = refs.pop(0)
    cache_refs = [refs.pop(0) for _ in range(n_caches)]
    new_t_refs = [refs.pop(0) for _ in range(n_caches)]
    q_refs, kn_refs, vn_refs, bc_refs, bn_refs = ([refs.pop(0) for _ in range(n_g)] for _ in range(5))
    o_ref = refs.pop(0)
    cache_out_refs = [refs.pop(0) for _ in range(n_caches)]
    attn_ref = refs.pop(0) if n_caches else None

    side_jobs = []
    if n_caches:
        j = pl.program_id(0) % (LANES // n_new)
        to_tail = LANES - n_new - n_new * j
        per_cache = [_cache_shift_jobs(c, t, o, to_tail, n_new)
                     for c, t, o in zip(cache_refs, new_t_refs, cache_out_refs)]
        side_jobs = [job for jobs in zip(*per_cache) for job in jobs]
        groups_done = []

        def attend(g):
            groups_done.append(_cache_attn_group(q_refs[g][0], kn_refs[g][0], vn_refs[g][0], cache_refs[g][0],
                                                 cache_refs[n_g + g][0], bc_refs[g][...], bn_refs[g][...], n_new))
            if len(groups_done) == n_g:
                attn_ref[0] = _merge_groups([o for o, _ in groups_done], [l for _, l in groups_done]).astype(BF16)

        for g in range(n_g):
            side_jobs.insert(g * (FFN_SIDE_JOB_PIECES + 1), functools.partial(attend, g))
    n_pieces = FFN_SIDE_JOB_PIECES if side_jobs else 1

    x = x_ref[...]
    if pre_proj:
        x = x + _dot(a_ref[...], wp_ref[...])
    h = _rms(x, g_ref[...]).astype(BF16)
    bounds = [(D_FF // MXU_WIDTH * p // n_pieces) * MXU_WIDTH for p in range(n_pieces + 1)]
    acts = []
    for p in range(n_pieces):
        cols = slice(bounds[p], bounds[p + 1])
        gate = _dot(h, wg_ref[:, cols])
        up = _dot(h, wu_ref[:, cols])
        acts.append((gate * jax.nn.sigmoid(gate) * up).astype(BF16))
        for job in side_jobs[p::n_pieces]:
            job()
    act = acts[0] if n_pieces == 1 else jnp.concatenate(acts, axis=1)
    x = x + 0.5 * _dot(act, wd_ref[...])
    if final_norm:
        x = _rms(x, fg_ref[...])
    o_ref[...] = x


def _ffn(x, norm_g, wg, wu, wd, *, tm, pre=None, final_g=None, cache_update=None):
    rows = x.shape[0]
    args = [x]
    specs = [_rows_spec(tm, D_MODEL)]
    if pre is not None:
        a, wp = pre
        args += [a, wp]
        specs += [_rows_spec(tm, D_MODEL), _const_spec((D_MODEL, D_MODEL))]
    args += [norm_g, wg, wu, wd]
    specs += [_const_spec((1, D_MODEL)), _const_spec((D_MODEL, D_FF)), _const_spec((D_MODEL, D_FF)),
              _const_spec((D_FF, D_MODEL))]
    if final_g is not None:
        args.append(final_g)
        specs.append(_const_spec((1, D_MODEL)))
    out_specs = [_rows_spec(tm, D_MODEL)]
    out_shape = [jax.ShapeDtypeStruct((rows, D_MODEL), F32)]
    n_caches, n_new = 0, 0
    if cache_update is not None:
        caches, new_t, q_new, k_new, v_new, biases_c, biases_n, n_new = cache_update
        n_caches = len(caches)
        assert all(c.shape[0] == rows // tm for c in caches), "one grid step per cached sequence"
        per_tile = LANES // n_new
        cache_specs = [pl.BlockSpec((1,) + c.shape[1:], lambda i: (i, 0, 0)) for c in caches]
        new_spec = pl.BlockSpec((1, n_new, GROUP_WIDTH), lambda i: (i, 0, 0))
        small = list(q_new) + list(k_new) + list(v_new)
        args += list(caches) + list(new_t) + small + list(biases_c) + list(biases_n)
        specs += (cache_specs + [pl.BlockSpec((1, t.shape[1], LANES), lambda i: (0, 0, i // per_tile)) for t in new_t]
                  + [new_spec] * len(small) + [_const_spec(b.shape) for b in list(biases_c) + list(biases_n)])
        out_specs += cache_specs + [new_spec]
        out_shape += ([jax.ShapeDtypeStruct(c.shape, c.dtype) for c in caches]
                      + [jax.ShapeDtypeStruct((rows // tm, n_new, GROUP_WIDTH), BF16)])
    outs = pl.pallas_call(
        functools.partial(_ffn_kernel, pre_proj=pre is not None, final_norm=final_g is not None,
                          n_caches=n_caches, n_new=n_new),
        grid=(rows // tm,),
        in_specs=specs,
        out_specs=out_specs,
        out_shape=out_shape,
        compiler_params=_params(1),
        name="ffn",
    )(*args)
    return outs[0] if cache_update is None else outs


_Q_END = ATTN_WIDTH
_K_END = 2 * ATTN_WIDTH
_V_END = 3 * ATTN_WIDTH
_UA_END = _V_END + CONV_DIM
_UB_END = _UA_END + CONV_DIM
_GA_END = _UB_END + D_MODEL
IN_WIDTH = _GA_END + D_MODEL


def _win_kernel(*refs, t_groups, fuse_conv, tm, tiles_per_seq):
    x_ref, g_ref, w_ref = refs[:3]
    cw_ref = refs[3] if fuse_conv else None
    outs = refs[4:] if fuse_conv else refs[3:]
    q_ref, k_ref, v_ref, ga_ref, gb_ref = outs[:5]
    n_t = 2 * len(t_groups)
    t_refs = outs[5:5 + n_t]
    h = _rms(x_ref[...], g_ref[...]).astype(BF16)

    def seg(lo, hi):
        return _dot(h, w_ref[:, lo:hi])

    u = seg(_V_END, _UA_END) * jax.nn.sigmoid(seg(_UA_END, _UB_END))
    if fuse_conv:
        yc_ref, tail_ref, ext_ref = outs[5 + n_t:]
        first_tile = pl.program_id(0) % tiles_per_seq == 0

        @pl.when(first_tile)
        def _():
            ext_ref[0:CONV_HALO, :] = jnp.zeros((CONV_HALO, CONV_DIM), F32)

        @pl.when(jnp.logical_not(first_tile))
        def _():
            ext_ref[0:CONV_HALO, :] = ext_ref[tm:tm + CONV_HALO, :]

        ext_ref[CONV_HALO:CONV_HALO + tm, :] = u
        ext_ref[CONV_HALO + tm:CONV_HALO + tm + SUBLANES, :] = jnp.zeros((SUBLANES, CONV_DIM), F32)
        tail_ref[0] = u[tm - CONV_HALO:tm]
        conv_pieces = [(r0, l0) for r0 in range(0, tm, CONV_CHUNK_ROWS) for l0 in range(0, CONV_DIM, LANES)]
    else:
        outs[5 + n_t][...] = u
        conv_pieces = []
    n_proj_left = [3 * len(GROUPS) + 2 * (D_MODEL // GROUP_WIDTH)]

    def conv_step():
        n = -(-len(conv_pieces) // n_proj_left[0])
        n_proj_left[0] -= 1
        if n:
            _conv_taps_aligned(ext_ref, cw_ref, yc_ref, CONV_HALO - (CONV_WIDTH - 1), conv_pieces[:n])
            del conv_pieces[:n]

    for g in range(len(GROUPS)):
        lo = g * GROUP_WIDTH
        q = seg(lo, lo + GROUP_WIDTH) * (HEAD_DIM ** -0.5)
        conv_step()
        k = seg(_Q_END + lo, _Q_END + lo + GROUP_WIDTH)
        conv_step()
        v = seg(_K_END + lo, _K_END + lo + GROUP_WIDTH)
        conv_step()
        for hf in range(GROUP_WIDTH // LANES):
            lanes = slice(hf * LANES, (hf + 1) * LANES)
            q_ref[g, hf] = q[:, lanes]
            k_ref[g, hf] = k[:, lanes]
            v_ref[g, hf] = v[:, lanes]
        if g in t_groups:
            j = t_groups.index(g)
            t_refs[2 * j][0] = k.T
            t_refs[2 * j + 1][0] = v.T
    for gate_ref, first in ((ga_ref, _UB_END), (gb_ref, _GA_END)):
        for c in range(0, D_MODEL, GROUP_WIDTH):
            gate_ref[:, c:c + GROUP_WIDTH] = jax.nn.sigmoid(seg(first + c, first + c + GROUP_WIDTH)).astype(BF16)
            conv_step()


def _win(x, norm_g, w_in, *, tm, rows_per_seq, t_groups, conv_w=None):
    rows = x.shape[0]
    n_seq = rows // rows_per_seq
    n_t = 2 * len(t_groups)
    tiles_per_seq = rows_per_seq // tm
    fuse_conv = conv_w is not None
    n_g = len(GROUPS)
    n_hf = GROUP_WIDTH // LANES
    out_shape = ([jax.ShapeDtypeStruct((n_g, n_hf, rows, LANES), F32)] * 3
                 + [jax.ShapeDtypeStruct((rows, D_MODEL), BF16)] * 2
                 + [jax.ShapeDtypeStruct((n_seq, GROUP_WIDTH, rows_per_seq), F32)] * n_t
                 + [jax.ShapeDtypeStruct((rows, CONV_DIM), F32)])
    t_spec = pl.BlockSpec((1, GROUP_WIDTH, tm), lambda i: (i // tiles_per_seq, 0, i % tiles_per_seq))
    out_specs = ([pl.BlockSpec((n_g, n_hf, tm, LANES), lambda i: (0, 0, i, 0))] * 3
                 + [_rows_spec(tm, D_MODEL)] * 2 + [t_spec] * n_t
                 + [_rows_spec(tm, CONV_DIM)])
    in_specs = [_rows_spec(tm, D_MODEL), _const_spec((1, D_MODEL)), _const_spec((D_MODEL, IN_WIDTH))]
    args = [x, norm_g, w_in]
    scratch = []
    if fuse_conv:
        in_specs.append(_const_spec((CONV_WIDTH, CONV_DIM)))
        args.append(conv_w)
        out_shape.append(jax.ShapeDtypeStruct((n_seq, CONV_HALO, CONV_DIM), F32))
        out_specs.append(pl.BlockSpec((1, CONV_HALO, CONV_DIM), lambda i: (i // tiles_per_seq, 0, 0)))
        scratch.append(pltpu.VMEM((CONV_HALO + tm + SUBLANES, CONV_DIM), F32))
    return pl.pallas_call(
        functools.partial(_win_kernel, t_groups=tuple(t_groups), fuse_conv=fuse_conv, tm=tm,
                          tiles_per_seq=tiles_per_seq),
        grid=(rows // tm,),
        in_specs=in_specs,
        out_specs=out_specs,
        out_shape=out_shape,
        scratch_shapes=scratch,
        compiler_params=pltpu.CompilerParams(dimension_semantics=("arbitrary",),
                                             vmem_limit_bytes=V7X_VMEM_LIMIT_BYTES),
        name="w_in",
    )(*args)


def _rel_bucket(dist):
    n = jnp.maximum(dist, 0)
    max_exact = N_BUCKETS // 2
    nf = jnp.maximum(n, 1).astype(F32)
    large = max_exact + (jnp.log(nf / max_exact) / math.log(MAX_DISTANCE / max_exact)
                         * (N_BUCKETS - max_exact)).astype(jnp.int32)
    return jnp.where(n < max_exact, n, jnp.minimum(large, N_BUCKETS - 1))


def _tap_bias(rel_bias, g, dil):
    bias_g = rel_bias[:, g * HEADS_PER_GROUP:(g + 1) * HEADS_PER_GROUP]
    return bias_g[_rel_bucket(jnp.arange(SPAN, -1, -1) * dil)].astype(F32).T


def _toeplitz(vec, n_rows, n_cols):
    n_heads, period = vec.shape
    flat = jnp.tile(vec, (1, n_rows))[:, :n_rows * (period - 1)]
    return flat.reshape(n_heads, n_rows, period - 1)[:, :, :n_cols]


def _band_bias(tap_bias):
    vec = jnp.concatenate([tap_bias, jnp.full((HEADS_PER_GROUP, 2 * SPAN - 1), NEG_INF, F32)], axis=1)
    return _toeplitz(vec, SPAN, 2 * SPAN).reshape(HEADS_PER_GROUP * SPAN, 2 * SPAN)


def _cache_bias(tap_bias, dil, cache_len, n_new):
    rev = lax.pad(tap_bias, jnp.float32(NEG_INF), [(0, 0, 0), (cache_len + n_new - 1 - SPAN * dil, 0, dil - 1)])
    bias_c = _toeplitz(jnp.roll(rev, -(n_new - 1), axis=1), n_new, cache_len)
    near = jnp.concatenate([rev[:, cache_len:], jnp.full((HEADS_PER_GROUP, LANES), NEG_INF, F32)], axis=1)
    bias_n = _toeplitz(jnp.roll(near, -(n_new - 1), axis=1), n_new, LANES)
    return (bias_c.reshape(HEADS_PER_GROUP * n_new, cache_len), bias_n.reshape(HEADS_PER_GROUP * n_new, LANES))


MAX_ROW_STRIDE = 4


def _gather_residues(src, store, tmp_ref, dil, seq):
    if dil == 1:
        store(0, src[...])
    elif dil <= MAX_ROW_STRIDE:
        for r in range(dil):
            store(r, src[pl.ds(r, seq, stride=dil), :])
    else:
        inner = dil // MAX_ROW_STRIDE
        assert inner <= MAX_ROW_STRIDE and inner * MAX_ROW_STRIDE == dil
        n_part = seq * inner
        for c in range(MAX_ROW_STRIDE):
            tmp_ref[c * n_part:(c + 1) * n_part, :] = src[pl.ds(c, n_part, stride=MAX_ROW_STRIDE), :]
        for c in range(MAX_ROW_STRIDE):
            for c2 in range(inner):
                store(c + MAX_ROW_STRIDE * c2, tmp_ref[pl.ds(c * n_part + c2, seq, stride=inner), :])


def _scatter_residues(load, dst, tmp_ref, dil, seq):
    if dil == 1:
        dst[...] = load(0)
    elif dil <= MAX_ROW_STRIDE:
        for r in range(dil):
            dst[pl.ds(r, seq, stride=dil), :] = load(r)
    else:
        inner = dil // MAX_ROW_STRIDE
        assert inner <= MAX_ROW_STRIDE and inner * MAX_ROW_STRIDE == dil
        n_part = seq * inner
        for c in range(MAX_ROW_STRIDE):
            for c2 in range(inner):
                tmp_ref[pl.ds(c * n_part + c2, seq, stride=inner), :] = load(c + MAX_ROW_STRIDE * c2)
        for c in range(MAX_ROW_STRIDE):
            dst[pl.ds(c, n_part, stride=MAX_ROW_STRIDE), :] = tmp_ref[c * n_part:(c + 1) * n_part, :]


def _merge_groups(outs, lses):
    m = functools.reduce(jnp.maximum, lses)
    es = [jnp.exp(l - m) for l in lses]
    num = sum(e * o for e, o in zip(es, outs))
    return num * (1.0 / sum(es))


def _band_group(q_lo, q_hi, k_lo, k_hi, v_lo, v_hi, b_ref, o_dst, l_dst, qb_ref, kb_ref, vb_ref, ob_ref, lb_ref,
                tmp_ref, *, dil, seq):
    n_blk = seq // SPAN
    halves = (slice(0, LANES), slice(LANES, 2 * LANES))
    for dst, srcs in ((qb_ref, (q_lo, q_hi)), (kb_ref, (k_lo, k_hi)), (vb_ref, (v_lo, v_hi))):
        for lanes, src in zip(halves, srcs):
            def store(r, rows, dst=dst, lanes=lanes):
                dst[r * seq:(r + 1) * seq, lanes] = rows.astype(BF16)
            _gather_residues(src, store, tmp_ref, dil, seq)
    masks = _head_masks()

    def block(q, kb, vb, key_lo):
        qm = jnp.concatenate([jnp.where(masks[h], q, jnp.zeros_like(q)) for h in range(HEADS_PER_GROUP)], axis=0)
        s = _dot_nt(qm, kb) + b_ref[:, key_lo:]
        m = jnp.max(s, axis=-1, keepdims=True)
        p = jnp.exp(s - m)
        den = jnp.sum(p, axis=-1, keepdims=True)
        acc = _dot(p.astype(BF16), vb) * (1.0 / den)
        lse_rows = m + jnp.log(den)
        o = acc[0:SPAN]
        lse = jnp.broadcast_to(lse_rows[0:SPAN], (SPAN, GROUP_WIDTH))
        for h in range(1, HEADS_PER_GROUP):
            o = jnp.where(masks[h], acc[h * SPAN:(h + 1) * SPAN], o)
            lse = jnp.where(masks[h], lse_rows[h * SPAN:(h + 1) * SPAN], lse)
        return o, lse

    for r in range(dil):
        for j in range(n_blk):
            q_rows = slice(r * seq + j * SPAN, r * seq + (j + 1) * SPAN)
            k_rows = slice(r * seq + max(j - 1, 0) * SPAN, r * seq + (j + 1) * SPAN)
            o, lse = block(qb_ref[q_rows, :], kb_ref[k_rows, :], vb_ref[k_rows, :], SPAN if j == 0 else 0)
            ob_ref[q_rows, :] = o
            lb_ref[q_rows, :] = lse

    for src, dsts in ((ob_ref, o_dst), (lb_ref, l_dst)):
        for lanes, dst in zip(halves, dsts):
            _scatter_residues(lambda r, src=src, lanes=lanes: src[r * seq:(r + 1) * seq, lanes], dst, tmp_ref, dil, seq)


MERGE_CHUNK_ROWS = 256


def _band_attn_kernel(q_lo, q_hi, k_lo, k_hi, v_lo, v_hi, b_ref, out_ref,
                      qb_ref, kb_ref, vb_ref, ob_ref, lb_ref, tmp_ref, *nat_refs, seq_len):
    n_half = 2 * len(GROUPS)
    on_refs, ln_refs = nat_refs[:n_half], nat_refs[n_half:]
    g = pl.program_id(1)
    for gi, (_, dil) in enumerate(GROUPS):
        @pl.when(g == gi)
        def _(gi=gi, dil=dil):
            _band_group(q_lo, q_hi, k_lo, k_hi, v_lo, v_hi, b_ref,
                        on_refs[2 * gi:2 * gi + 2], ln_refs[2 * gi:2 * gi + 2],
                        qb_ref, kb_ref, vb_ref, ob_ref, lb_ref, tmp_ref, dil=dil, seq=seq_len // dil)

    @pl.when(g == len(GROUPS) - 1)
    def _():
        for r0 in range(0, seq_len, MERGE_CHUNK_ROWS):
            rows = slice(r0, r0 + MERGE_CHUNK_ROWS)
            for hf in (0, 1):
                merged = _merge_groups([on_refs[2 * gi + hf][rows, :] for gi in range(len(GROUPS))],
                                       [ln_refs[2 * gi + hf][rows, :] for gi in range(len(GROUPS))])
                out_ref[rows, hf * LANES:(hf + 1) * LANES] = merged.astype(BF16)


def _band_attn(q, k, v, biases, *, batch, seq_len):
    n_g = len(GROUPS)
    view = lambda a: a.reshape(n_g, 2, batch, seq_len, LANES)
    in_halves = [pl.BlockSpec((None, None, None, seq_len, LANES), lambda b, g, hf=hf: (g, hf, b, 0, 0))
                 for hf in (0, 1)]
    out = pl.pallas_call(
        functools.partial(_band_attn_kernel, seq_len=seq_len),
        grid=(batch, n_g),
        in_specs=in_halves * 3 + [pl.BlockSpec((None, HEADS_PER_GROUP * SPAN, 2 * SPAN), lambda b, g: (g, 0, 0))],
        out_specs=pl.BlockSpec((None, seq_len, GROUP_WIDTH), lambda b, g: (b, 0, 0)),
        out_shape=jax.ShapeDtypeStruct((batch, seq_len, GROUP_WIDTH), BF16),
        scratch_shapes=([pltpu.VMEM((seq_len, GROUP_WIDTH), BF16)] * 3 + [pltpu.VMEM((seq_len, GROUP_WIDTH), F32)] * 2
                        + [pltpu.VMEM((seq_len, LANES), F32)]
                        + [pltpu.VMEM((seq_len, LANES), F32)] * (4 * n_g)),
        compiler_params=pltpu.CompilerParams(dimension_semantics=("parallel", "arbitrary"),
                                             vmem_limit_bytes=V7X_VMEM_LIMIT_BYTES),
        name="band_attn",
    )(view(q), view(q), view(k), view(k), view(v), view(v), biases)
    return out.reshape(batch * seq_len, GROUP_WIDTH)


def _cache_attn_group(q, kn, vn, kc, vc, bias_c, bias_n, n_new):
    masks = _head_masks()
    pad = jnp.zeros((SPAN - n_new, GROUP_WIDTH), F32)
    qm = jnp.concatenate([jnp.where(masks[h], q, 0.0) for h in range(HEADS_PER_GROUP)], axis=0).astype(BF16)
    kn_p = jnp.concatenate([kn, pad], axis=0).astype(BF16)
    vn_p = jnp.concatenate([vn, pad], axis=0).astype(BF16)
    s_c = _dot(qm, kc.astype(BF16)) + bias_c
    s_n = _dot_nt(qm, kn_p) + bias_n
    m = jnp.maximum(jnp.max(s_c, axis=-1, keepdims=True), jnp.max(s_n, axis=-1, keepdims=True))
    p_c = jnp.exp(s_c - m)
    p_n = jnp.exp(s_n - m)
    den = jnp.sum(p_c, axis=-1, keepdims=True) + jnp.sum(p_n, axis=-1, keepdims=True)
    acc = _dot_nt(p_c.astype(BF16), vc.astype(BF16)) + _dot(p_n.astype(BF16), vn_p)
    acc = acc * (1.0 / den)
    lse_rows = m + jnp.log(den)
    o = jnp.zeros((n_new, GROUP_WIDTH), F32)
    lse = jnp.zeros((n_new, GROUP_WIDTH), F32)
    for h in range(HEADS_PER_GROUP):
        rows = slice(h * n_new, (h + 1) * n_new)
        o = jnp.where(masks[h], acc[rows], o)
        lse = jnp.where(masks[h], lse_rows[rows], lse)
    return o, lse


def _conv_taps(ext_ref, w_ref, first_row, n_rows):
    acc = jnp.zeros((n_rows, CONV_DIM), F32)
    for j in range(CONV_WIDTH):
        acc = acc + ext_ref[pl.ds(first_row + j, n_rows), :] * w_ref[j:j + 1, :]
    return acc


CONV_CHUNK_ROWS = 64


def _conv_taps_aligned(ext_ref, w_ref, y_ref, first_row, pieces):
    for r0, l0 in pieces:
        lanes = slice(l0, l0 + LANES)
        acc = None
        for b in range(SUBLANES):
            part = None
            for a in range((first_row + CONV_WIDTH - 1 - b) // SUBLANES + 1):
                j = SUBLANES * a + b - first_row
                if j < 0:
                    continue
                term = ext_ref[pl.ds(r0 + SUBLANES * a, CONV_CHUNK_ROWS + SUBLANES), lanes] * w_ref[j:j + 1, lanes]
                part = term if part is None else part + term
            part = part[b:b + CONV_CHUNK_ROWS]
            acc = part if acc is None else acc + part
        y_ref[r0:r0 + CONV_CHUNK_ROWS, lanes] = acc


def _conv_sample_kernel(u_ref, st_ref, w_ref, y_ref, so_ref, ext_ref, *, n_new, n_seq):
    hist = CONV_WIDTH - 1
    for b in range(n_seq):
        ext_ref[b, 0:hist, :] = st_ref[b]
        ext_ref[b, hist:hist + n_new, :] = u_ref[b]
        y_ref[b] = _conv_taps(ext_ref.at[b], w_ref, 0, n_new)
        so_ref[b] = ext_ref[b, n_new:n_new + hist, :]


def _conv_sample(u, state, w, *, batch, n_new, seqs_per_step=8):
    hist = CONV_WIDTH - 1
    u3 = u.reshape(batch, n_new, CONV_DIM)
    n_seq = math.gcd(batch, seqs_per_step)
    y, st = pl.pallas_call(
        functools.partial(_conv_sample_kernel, n_new=n_new, n_seq=n_seq),
        grid=(batch // n_seq,),
        in_specs=[pl.BlockSpec((n_seq, n_new, CONV_DIM), lambda b: (b, 0, 0)),
                  pl.BlockSpec((n_seq, hist, CONV_DIM), lambda b: (b, 0, 0)),
                  pl.BlockSpec((CONV_WIDTH, CONV_DIM), lambda b: (0, 0))],
        out_specs=[pl.BlockSpec((n_seq, n_new, CONV_DIM), lambda b: (b, 0, 0)),
                   pl.BlockSpec((n_seq, hist, CONV_DIM), lambda b: (b, 0, 0))],
        out_shape=[jax.ShapeDtypeStruct((batch, n_new, CONV_DIM), F32),
                   jax.ShapeDtypeStruct((batch, hist, CONV_DIM), F32)],
        scratch_shapes=[pltpu.VMEM((n_seq, hist + n_new + 2, CONV_DIM), F32)],
        compiler_params=_params(1),
        name="conv_sample",
    )(u3, state, w)
    return y.reshape(batch * n_new, CONV_DIM), st


def _mix_branches(attn, yconv, gate_a, gate_b, x, wap_ref, cb_ref, lg_ref, lb_ref, wcp_ref, wo_ref):
    a_branch = _dot(attn, wap_ref[...])
    y = yconv + cb_ref[...]
    mu = jnp.mean(y, axis=-1, keepdims=True)
    yc = y - mu
    var = jnp.mean(yc * yc, axis=-1, keepdims=True)
    y = yc * lax.rsqrt(var + EPS) * lg_ref[...] + lb_ref[...]
    y = y * jax.nn.sigmoid(y)
    c_branch = _dot(y.astype(BF16), wcp_ref[...])
    merged = gate_a * a_branch + gate_b * c_branch
    return x + _dot(merged.astype(BF16), wo_ref[...])


def _xattn_query(x, xg_ref, wxq_ref):
    hq = _rms(x, xg_ref[...]).astype(BF16)
    return (_dot(hq, wxq_ref[...]) * (X_HEAD_DIM ** -0.5)).astype(BF16)


def _xattn_heads(q, k_ref, v_ref):
    outs = []
    for h in range(X_HEADS):
        lanes = slice(h * X_HEAD_DIM, (h + 1) * X_HEAD_DIM)
        s = _dot_nt(q[:, lanes], k_ref[:, lanes])
        m = jnp.max(s, axis=-1, keepdims=True)
        p = jnp.exp(s - m)
        den = jnp.sum(p, axis=-1, keepdims=True)
        outs.append((_dot(p.astype(BF16), v_ref[:, lanes]) * (1.0 / den)).astype(BF16))
    return jnp.concatenate(outs, axis=-1)


def _mix_sample_kernel(at_ref, yc_ref, ga_ref, gb_ref, x_ref, wap_ref, cb_ref, lg_ref, lb_ref, wcp_ref, wo_ref,
                       xg_ref, wxq_ref, x_out, q_out):
    x = _mix_branches(at_ref[...], yc_ref[...], ga_ref[...], gb_ref[...], x_ref[...],
                      wap_ref, cb_ref, lg_ref, lb_ref, wcp_ref, wo_ref)
    x_out[...] = x
    q_out[...] = _xattn_query(x, xg_ref, wxq_ref)


def _mix_prompt_kernel(at_ref, yc_ref, ga_ref, gb_ref, x_ref, wap_ref, cb_ref, lg_ref, lb_ref, wcp_ref, wo_ref,
                       xg_ref, wxq_ref, mk_ref, mv_ref, wxo_ref, x_out):
    x = _mix_branches(at_ref[...], yc_ref[...], ga_ref[...], gb_ref[...], x_ref[...],
                      wap_ref, cb_ref, lg_ref, lb_ref, wcp_ref, wo_ref)
    attn = _xattn_heads(_xattn_query(x, xg_ref, wxq_ref), mk_ref, mv_ref)
    x_out[...] = x + _dot(attn, wxo_ref[...])


def _mix_weight_specs():
    return [_const_spec((GROUP_WIDTH, D_MODEL)), _const_spec((1, CONV_DIM)), _const_spec((1, CONV_DIM)),
            _const_spec((1, CONV_DIM)), _const_spec((CONV_DIM, D_MODEL)), _const_spec((D_MODEL, D_MODEL)),
            _const_spec((1, D_MODEL)), _const_spec((D_MODEL, D_MODEL))]


def _mix_weights(w):
    return (w['w_attn_proj'], w['conv_dw_b'], w['conv_ln_g'], w['conv_ln_b'], w['w_conv_proj'], w['w_o'],
            w['xattn_norm'], w['w_xq'])


def _mix_sample(attn, yconv, ga, gb, x, w, *, tm):
    rows = x.shape[0]
    wide = _rows_spec(tm, D_MODEL)
    return pl.pallas_call(
        _mix_sample_kernel,
        grid=(rows // tm,),
        in_specs=[_rows_spec(tm, GROUP_WIDTH), _rows_spec(tm, CONV_DIM), wide, wide, wide] + _mix_weight_specs(),
        out_specs=[wide, wide],
        out_shape=[jax.ShapeDtypeStruct((rows, D_MODEL), F32), jax.ShapeDtypeStruct((rows, D_MODEL), BF16)],
        compiler_params=_params(1),
        name="mix_sample",
    )(attn, yconv, ga, gb, x, *_mix_weights(w))


def _mix_prompt(attn, yconv, ga, gb, x, mk, mv, w, *, tm, seq_len):
    rows = x.shape[0]
    tiles_per_seq = seq_len // tm
    wide = _rows_spec(tm, D_MODEL)
    mem = pl.BlockSpec((N_MEM, D_MODEL), lambda i: (i // tiles_per_seq, 0))
    return pl.pallas_call(
        _mix_prompt_kernel,
        grid=(rows // tm,),
        in_specs=([_rows_spec(tm, GROUP_WIDTH), _rows_spec(tm, CONV_DIM), wide, wide, wide]
                  + _mix_weight_specs() + [mem, mem, _const_spec((D_MODEL, D_MODEL))]),
        out_specs=wide,
        out_shape=jax.ShapeDtypeStruct((rows, D_MODEL), F32),
        compiler_params=_params(1),
        name="mix_prompt",
    )(attn, yconv, ga, gb, x, *_mix_weights(w), mk, mv, w['w_xo'])


def _memkv_kernel(m_ref, g_ref, w_ref, k5_ref, v5_ref, k_ref, v_ref, *, n_seq):
    h = _rms(m_ref[...], g_ref[...]).astype(BF16)
    for hd in range(X_HEADS):
        lo = hd * X_HEAD_DIM
        k = _dot(h, w_ref[:, lo:lo + X_HEAD_DIM])
        v = _dot(h, w_ref[:, D_MODEL + lo:D_MODEL + lo + X_HEAD_DIM])
        for b in range(n_seq):
            k5_ref[0, b, :, hd, :] = k[b * N_MEM:(b + 1) * N_MEM]
            v5_ref[0, b, :, hd, :] = v[b * N_MEM:(b + 1) * N_MEM]
        k_ref[:, lo:lo + X_HEAD_DIM] = k.astype(BF16)
        v_ref[:, lo:lo + X_HEAD_DIM] = v.astype(BF16)


def _memkv(mem, norm_g, w_xkv, *, batch, seqs_per_step=4):
    n_seq = math.gcd(batch, seqs_per_step)
    spec5 = pl.BlockSpec((1, n_seq, N_MEM, X_HEADS, X_HEAD_DIM), lambda b: (0, b, 0, 0, 0))
    rows = _rows_spec(n_seq * N_MEM, D_MODEL)
    return pl.pallas_call(
        functools.partial(_memkv_kernel, n_seq=n_seq),
        grid=(batch // n_seq,),
        in_specs=[rows, _const_spec((1, D_MODEL)), _const_spec((D_MODEL, 2 * D_MODEL))],
        out_specs=[spec5, spec5, rows, rows],
        out_shape=([jax.ShapeDtypeStruct((1, batch, N_MEM, X_HEADS, X_HEAD_DIM), F32)] * 2
                   + [jax.ShapeDtypeStruct((batch * N_MEM, D_MODEL), BF16)] * 2),
        compiler_params=_params(1),
        name="memory_kv",
    )(mem, norm_g, w_xkv)


def _xattn_cached_kernel(q_ref, k_lo, k_hi, v_lo, v_hi, mask_ref, o_ref, *, n_q, n_seq):
    for b in range(n_seq):
        flat = lambda ref: ref[0, b].reshape(N_MEM * X_HEADS, LANES)
        k_all = jnp.concatenate([flat(k_lo), flat(k_hi)], axis=1).astype(BF16)
        v_all = jnp.concatenate([flat(v_lo), flat(v_hi)], axis=1).astype(BF16)
        q = q_ref[b].astype(F32)
        qs = jnp.concatenate([q[:, h * X_HEAD_DIM:(h + 1) * X_HEAD_DIM] for h in range(X_HEADS)],
                             axis=0).astype(BF16)
        s = _dot_nt(qs, k_all) + mask_ref[...]
        m = jnp.max(s, axis=-1, keepdims=True)
        p = jnp.exp(s - m)
        den = jnp.sum(p, axis=-1, keepdims=True)
        acc = _dot(p.astype(BF16), v_all) * (1.0 / den)
        o_ref[b] = jnp.concatenate([acc[h * n_q:(h + 1) * n_q] for h in range(X_HEADS)], axis=1).astype(BF16)


def _xattn_cached(q, mk, mv, *, batch, seqs_per_step=4):
    n_q = q.shape[1]
    n_seq = math.gcd(batch, seqs_per_step)
    q_spec = pl.BlockSpec((n_seq, n_q, D_MODEL), lambda b: (b, 0, 0))
    halves = [pl.BlockSpec((1, n_seq, N_MEM, X_HEADS, LANES), lambda b, hf=hf: (0, b, 0, 0, hf))
              for hf in range(X_HEAD_DIM // LANES)]
    own_head = (np.arange(N_MEM * X_HEADS)[None, :] % X_HEADS) == (np.arange(X_HEADS * n_q)[:, None] // n_q)
    mask = jnp.asarray(np.where(own_head, 0.0, NEG_INF), F32)
    return pl.pallas_call(
        functools.partial(_xattn_cached_kernel, n_q=n_q, n_seq=n_seq),
        grid=(batch // n_seq,),
        in_specs=[q_spec] + halves * 2 + [_const_spec(mask.shape)],
        out_specs=q_spec,
        out_shape=jax.ShapeDtypeStruct((batch, n_q, D_MODEL), BF16),
        compiler_params=_params(1),
        name="cross_attn_cached",
    )(q, mk, mk, mv, mv, mask)


def _to_positions_heads(a_t, batch, n_pos):
    return jnp.transpose(a_t.reshape(batch, HEADS_PER_GROUP, HEAD_DIM, n_pos), (0, 3, 1, 2))[None]


def _to_feature_major(a, batch, n_pos):
    return jnp.transpose(a[0], (0, 2, 3, 1)).reshape(batch, GROUP_WIDTH, n_pos)


def kernel(x_prompt, x_sample, mem_prompt, cache_win0_k, cache_win0_v, cache_win1_k, cache_win1_v, cache_win2_k, cache_win2_v, state_conv, cache_mem_k, cache_mem_v, rel_bias, ffn1_norm, ffn1_w_gate, ffn1_w_up, ffn1_w_down, mix_norm, w_in, w_attn_proj, conv_dw_w, conv_dw_b, conv_ln_g, conv_ln_b, w_conv_proj, w_o, xattn_norm, mem_norm, w_xq, w_xkv, w_xo, ffn2_norm, ffn2_w_gate, ffn2_w_up, ffn2_w_down, final_norm):
    batch, seq_len, _ = x_prompt.shape
    dec_batch, dec_seq, _ = x_sample.shape
    n_g = len(GROUPS)
    assert ffn1_norm.shape[0] == 1, "single layer"
    mat = lambda a: a[0].astype(BF16)
    vec = lambda a: a[0].reshape(1, -1)
    w = dict(ffn1_norm=vec(ffn1_norm), ffn1_w_gate=mat(ffn1_w_gate), ffn1_w_up=mat(ffn1_w_up),
             ffn1_w_down=mat(ffn1_w_down), mix_norm=vec(mix_norm), w_in=mat(w_in),
             w_attn_proj=mat(w_attn_proj), conv_dw_w=conv_dw_w[0], conv_dw_b=vec(conv_dw_b),
             conv_ln_g=vec(conv_ln_g), conv_ln_b=vec(conv_ln_b), w_conv_proj=mat(w_conv_proj),
             w_o=mat(w_o), xattn_norm=vec(xattn_norm), mem_norm=vec(mem_norm), w_xq=mat(w_xq),
             w_xkv=mat(w_xkv), w_xo=mat(w_xo), ffn2_norm=vec(ffn2_norm), ffn2_w_gate=mat(ffn2_w_gate),
             ffn2_w_up=mat(ffn2_w_up), ffn2_w_down=mat(ffn2_w_down))
    final_g = final_norm.reshape(1, -1)
    tap_biases = [_tap_bias(rel_bias, g, dil) for g, (_, dil) in enumerate(GROUPS)]

    def head(x, tm, rows_per_seq, t_groups, conv_w=None, cache_update=None):
        x1 = _ffn(x, w['ffn1_norm'], w['ffn1_w_gate'], w['ffn1_w_up'], w['ffn1_w_down'], tm=tm,
                  cache_update=cache_update)
        new_caches = []
        if cache_update is not None:
            x1, *new_caches = x1
        outs = _win(x1, w['mix_norm'], w['w_in'], tm=tm, rows_per_seq=rows_per_seq, t_groups=t_groups, conv_w=conv_w)
        n_t = 2 * len(t_groups)
        return x1, outs[0], outs[1], outs[2], outs[3], outs[4], outs[5:5 + n_t], outs[5 + n_t:], new_caches

    rows_s = dec_batch * dec_seq
    x1_s, q_s, k_s, v_s, ga_s, gb_s, kv_t_s, (u_s,), _ = head(x_sample.reshape(rows_s, D_MODEL), rows_s, rows_s,
                                                              list(range(n_g)))
    caches_k = [_to_feature_major(c, dec_batch, c.shape[2]) for c in (cache_win0_k, cache_win1_k, cache_win2_k)]
    caches_v = [_to_feature_major(c, dec_batch, c.shape[2]) for c in (cache_win0_v, cache_win1_v, cache_win2_v)]

    rows_p = batch * seq_len
    tm_p = PROMPT_ROW_TILE
    full_groups = [g for g, (window, _) in enumerate(GROUPS) if window >= seq_len]
    new3 = lambda a: [jnp.concatenate([a[g, 0], a[g, 1]], axis=-1).reshape(dec_batch, dec_seq, GROUP_WIDTH)
                      for g in range(n_g)]
    cache_biases = [_cache_bias(tap_biases[g], dil, caches_k[g].shape[2], dec_seq) for g, (_, dil) in enumerate(GROUPS)]
    cache_update = (caches_k + caches_v, list(kv_t_s[0::2]) + list(kv_t_s[1::2]), new3(q_s), new3(k_s), new3(v_s),
                    [bc for bc, _ in cache_biases], [bn for _, bn in cache_biases], dec_seq)
    x1, q, k, v, ga, gb, kv_t, (yconv, u_tail), (*new_caches, attn_s) = head(
        x_prompt.reshape(rows_p, D_MODEL), tm_p, seq_len, full_groups, conv_w=w['conv_dw_w'], cache_update=cache_update)
    attn = _band_attn(q, k, v, jnp.stack([_band_bias(t) for t in tap_biases]), batch=batch, seq_len=seq_len)
    p_mem_k, p_mem_v, mk_p, mv_p = _memkv(mem_prompt.reshape(batch * N_MEM, D_MODEL), w['mem_norm'], w['w_xkv'],
                                          batch=batch)
    x3 = _mix_prompt(attn, yconv, ga, gb, x1, mk_p, mv_p, w, tm=tm_p, seq_len=seq_len)
    y_prompt = _ffn(x3, w['ffn2_norm'], w['ffn2_w_gate'], w['ffn2_w_up'], w['ffn2_w_down'], tm=tm_p,
                    final_g=final_g)

    p_win = []
    for g, (window, _) in enumerate(GROUPS):
        keep = min(window, seq_len)
        if g in full_groups:
            j = full_groups.index(g)
            p_win += [_to_positions_heads(kv_t[2 * j], batch, seq_len), _to_positions_heads(kv_t[2 * j + 1], batch, seq_len)]
        else:
            for a in (k, v):
                tail = lax.slice(a.reshape(n_g, 2, batch, seq_len, LANES), (g, 0, 0, seq_len - keep, 0),
                                 (g + 1, 2, batch, seq_len, LANES))
                tail = jnp.transpose(tail[0], (1, 2, 0, 3))
                p_win.append(tail.reshape(1, batch, keep, HEADS_PER_GROUP, HEAD_DIM))
    p_conv = u_tail[:, CONV_HALO - (CONV_WIDTH - 1):][None]

    yconv, s_conv = _conv_sample(u_s, state_conv[0], w['conv_dw_w'], batch=dec_batch, n_new=dec_seq)
    x2, xq = _mix_sample(attn_s.reshape(rows_s, GROUP_WIDTH), yconv, ga_s, gb_s, x1_s, w, tm=rows_s)
    xo = _xattn_cached(xq.reshape(dec_batch, dec_seq, D_MODEL), cache_mem_k, cache_mem_v, batch=dec_batch)
    y_sample = _ffn(x2, w['ffn2_norm'], w['ffn2_w_gate'], w['ffn2_w_up'], w['ffn2_w_down'], tm=rows_s,
                    pre=(xo.reshape(rows_s, D_MODEL), w['w_xo']), final_g=final_g)

    s_win = []
    for g in range(n_g):
        for a in (new_caches[g], new_caches[n_g + g]):
            s_win.append(_to_positions_heads(a, dec_batch, a.shape[2]))

    return (y_prompt.reshape(batch, seq_len, D_MODEL), y_sample.reshape(dec_batch, dec_seq, D_MODEL),
            *p_win, p_conv, p_mem_k, p_mem_v, *s_win, s_conv[None])
```

```python
import functools
import math

import jax
import jax.numpy as jnp
import numpy as np
from jax import lax
from jax.experimental import pallas as pl
from jax.experimental.pallas import tpu as pltpu

D_MODEL = 1024
HEAD_DIM = 64
GROUPS = ((128, 1), (512, 4), (2048, 16))
HEADS_PER_GROUP = 4
GROUP_WIDTH = HEADS_PER_GROUP * HEAD_DIM
ATTN_WIDTH = len(GROUPS) * GROUP_WIDTH
CONV_DIM = D_MODEL // 2
CONV_WIDTH = 31
CONV_HALO = 32
N_BUCKETS = 32
MAX_DISTANCE = 2048
D_FF = ((8 * D_MODEL // 3 + 127) // 128) * 128
N_MEM = 256
X_HEADS = 4
X_HEAD_DIM = D_MODEL // X_HEADS
EPS = 1e-6
NEG_INF = -1e30
SPAN = 128
LANES = 128
SUBLANES = 8

V7X_VMEM_LIMIT_BYTES = 56 * 1024 * 1024
PROMPT_ROW_TILE = 512
BF16 = jnp.bfloat16
F32 = jnp.float32


def _params(n_axes):
    return pltpu.CompilerParams(dimension_semantics=("parallel",) * n_axes,
                                vmem_limit_bytes=V7X_VMEM_LIMIT_BYTES)


def _const_spec(shape):
    return pl.BlockSpec(shape, lambda *_: (0,) * len(shape), pipeline_mode=pl.Buffered(1))


def _rows_spec(tm, width):
    return pl.BlockSpec((tm, width), lambda i: (i, 0))


def _rms(x, g):
    return x * lax.rsqrt(jnp.mean(x * x, axis=-1, keepdims=True) + EPS) * g


def _dot(a, b):
    return jnp.dot(a, b, preferred_element_type=F32)


def _dot_nt(a, b):
    return lax.dot_general(a, b, (((1,), (1,)), ((), ())), preferred_element_type=F32)


def _head_masks(width=GROUP_WIDTH):
    lane = lax.broadcasted_iota(jnp.int32, (1, width), 1)
    return [(lane >= h * HEAD_DIM) & (lane < (h + 1) * HEAD_DIM) for h in range(HEADS_PER_GROUP)]


MXU_WIDTH = 256
FFN_SIDE_JOB_PIECES = 4
CACHE_SHIFT_ROWS = 64


def _cache_shift_jobs(src_ref, new_t_ref, dst_ref, to_tail, n_new):
    features, cache_len = src_ref.shape[1:]
    lane = lax.broadcasted_iota(jnp.int32, (1, LANES), 1)

    def job(r0):
        rows = slice(r0, r0 + CACHE_SHIFT_ROWS)
        shifted = pltpu.roll(src_ref[0, rows, :], cache_len - n_new, axis=1)
        new_tail = pltpu.roll(new_t_ref[0, rows, :], to_tail, axis=1)
        if cache_len > LANES:
            dst_ref[0, rows, 0:cache_len - LANES] = shifted[:, 0:cache_len - LANES]
        dst_ref[0, rows, cache_len - LANES:cache_len] = jnp.where(lane >= LANES - n_new, new_tail,
                                                                  shifted[:, cache_len - LANES:cache_len])

    return [functools.partial(job, r0) for r0 in range(0, features, CACHE_SHIFT_ROWS)]


def _ffn_kernel(*refs, pre_proj, final_norm, n_caches, n_new):
    n_g = n_caches // 2
    refs = list(refs)
    x_ref = refs.pop(0)
    if pre_proj:
        a_ref, wp_ref = refs.pop(0), refs.pop(0)
    g_ref, wg_ref, wu_ref, wd_ref = refs.pop(0), refs.pop(0), refs.pop(0), refs.pop(0)
    if final_norm:
        fg_ref = refs.pop(0)
    cache_refs = [refs.pop(0) for _ in range(n_caches)]
    new_t_refs = [refs.pop(0) for _ in range(n_caches)]
    q_refs, kn_refs, vn_refs, bc_refs, bn_refs = ([refs.pop(0) for _ in range(n_g)] for _ in range(5))
    o_ref = refs.pop(0)
    cache_out_refs = [refs.pop(0) for _ in range(n_caches)]
    attn_ref = refs.pop(0) if n_caches else None

    side_jobs = []
    if n_caches:
        j = pl.program_id(0) % (LANES // n_new)
        to_tail = LANES - n_new - n_new * j
        per_cache = [_cache_shift_jobs(c, t, o, to_tail, n_new)
                     for c, t, o in zip(cache_refs, new_t_refs, cache_out_refs)]
        side_jobs = [job for jobs in zip(*per_cache) for job in jobs]
        groups_done = []

        def attend(g):
            groups_done.append(_cache_attn_group(q_refs[g][0], kn_refs[g][0], vn_refs[g][0], cache_refs[g][0],
                                                 cache_refs[n_g + g][0], bc_refs[g][...], bn_refs[g][...], n_new))
            if len(groups_done) == n_g:
                attn_ref[0] = _merge_groups([o for o, _ in groups_done], [l for _, l in groups_done]).astype(BF16)

        for g in range(n_g):
            side_jobs.insert(g * (FFN_SIDE_JOB_PIECES + 1), functools.partial(attend, g))
    n_pieces = FFN_SIDE_JOB_PIECES if side_jobs else 1

    x = x_ref[...]
    if pre_proj:
        x = x + _dot(a_ref[...], wp_ref[...])
    h = _rms(x, g_ref[...]).astype(BF16)
    bounds = [(D_FF // MXU_WIDTH * p // n_pieces) * MXU_WIDTH for p in range(n_pieces + 1)]
    acts = []
    for p in range(n_pieces):
        cols = slice(bounds[p], bounds[p + 1])
        gate = _dot(h, wg_ref[:, cols])
        up = _dot(h, wu_ref[:, cols])
        acts.append((gate * jax.nn.sigmoid(gate) * up).astype(BF16))
        for job in side_jobs[p::n_pieces]:
            job()
    act = acts[0] if n_pieces == 1 else jnp.concatenate(acts, axis=1)
    x = x + 0.5 * _dot(act, wd_ref[...])
    if final_norm:
        x = _rms(x, fg_ref[...])
    o_ref[...] = x


def _ffn(x, norm_g, wg, wu, wd, *, tm, pre=None, final_g=None, cache_update=None):
    rows = x.shape[0]
    args = [x]
    specs = [_rows_spec(tm, D_MODEL)]
    if pre is not None:
        a, wp = pre
        args += [a, wp]
        specs += [_rows_spec(tm, D_MODEL), _const_spec((D_MODEL, D_MODEL))]
    args += [norm_g, wg, wu, wd]
    specs += [_const_spec((1, D_MODEL)), _const_spec((D_MODEL, D_FF)), _const_spec((D_MODEL, D_FF)),
              _const_spec((D_FF, D_MODEL))]
    if final_g is not None:
        args.append(final_g)
        specs.append(_const_spec((1, D_MODEL)))
    out_specs = [_rows_spec(tm, D_MODEL)]
    out_shape = [jax.ShapeDtypeStruct((rows, D_MODEL), F32)]
    n_caches, n_new = 0, 0
    if cache_update is not None:
        caches, new_t, q_new, k_new, v_new, biases_c, biases_n, n_new = cache_update
        n_caches = len(caches)
        assert all(c.shape[0] == rows // tm for c in caches), "one grid step per cached sequence"
        per_tile = LANES // n_new
        cache_specs = [pl.BlockSpec((1,) + c.shape[1:], lambda i: (i, 0, 0)) for c in caches]
        new_spec = pl.BlockSpec((1, n_new, GROUP_WIDTH), lambda i: (i, 0, 0))
        small = list(q_new) + list(k_new) + list(v_new)
        args += list(caches) + list(new_t) + small + list(biases_c) + list(biases_n)
        specs += (cache_specs + [pl.BlockSpec((1, t.shape[1], LANES), lambda i: (0, 0, i // per_tile)) for t in new_t]
                  + [new_spec] * len(small) + [_const_spec(b.shape) for b in list(biases_c) + list(biases_n)])
        out_specs += cache_specs + [new_spec]
        out_shape += ([jax.ShapeDtypeStruct(c.shape, c.dtype) for c in caches]
                      + [jax.ShapeDtypeStruct((rows // tm, n_new, GROUP_WIDTH), BF16)])
    outs = pl.pallas_call(
        functools.partial(_ffn_kernel, pre_proj=pre is not None, final_norm=final_g is not None,
                          n_caches=n_caches, n_new=n_new),
        grid=(rows // tm,),
        in_specs=specs,
        out_specs=out_specs,
        out_shape=out_shape,
        compiler_params=_params(1),
        name="ffn",
    )(*args)
    return outs[0] if cache_update is None else outs


_Q_END = ATTN_WIDTH
_K_END = 2 * ATTN_WIDTH
_V_END = 3 * ATTN_WIDTH
_UA_END = _V_END + CONV_DIM
_UB_END = _UA_END + CONV_DIM
_GA_END = _UB_END + D_MODEL
IN_WIDTH = _GA_END + D_MODEL


WEIGHT_CAST_CHUNKS = 16


def _win_kernel(*refs, t_groups, fuse_conv, tm, tiles_per_seq, n_cast):
    x_ref, g_ref, w_ref = refs[:3]
    cw_ref = refs[3] if fuse_conv else None
    n_in = 3 + int(fuse_conv) + n_cast
    cast_in, outs = refs[n_in - n_cast:n_in], refs[n_in:]
    q_ref, k_ref, v_ref, ga_ref, gb_ref = outs[:5]
    n_t = 2 * len(t_groups)
    t_refs = outs[5:5 + n_t]
    h = _rms(x_ref[...], g_ref[...]).astype(BF16)

    def seg(lo, hi):
        return _dot(h, w_ref[:, lo:hi])

    u = seg(_V_END, _UA_END) * jax.nn.sigmoid(seg(_UA_END, _UB_END))
    if fuse_conv:
        yc_ref, tail_ref = outs[5 + n_t:7 + n_t]
        cast_out, ext_ref = outs[7 + n_t:7 + n_t + n_cast], outs[7 + n_t + n_cast]
        first_tile = pl.program_id(0) % tiles_per_seq == 0

        @pl.when(pl.program_id(0) < WEIGHT_CAST_CHUNKS)
        def _():
            for src, dst in zip(cast_in, cast_out):
                dst[...] = src[...].astype(BF16)

        @pl.when(first_tile)
        def _():
            ext_ref[0:CONV_HALO, :] = jnp.zeros((CONV_HALO, CONV_DIM), F32)

        @pl.when(jnp.logical_not(first_tile))
        def _():
            ext_ref[0:CONV_HALO, :] = ext_ref[tm:tm + CONV_HALO, :]

        ext_ref[CONV_HALO:CONV_HALO + tm, :] = u
        ext_ref[CONV_HALO + tm:CONV_HALO + tm + SUBLANES, :] = jnp.zeros((SUBLANES, CONV_DIM), F32)
        tail_ref[0] = u[tm - CONV_HALO:tm]
        conv_pieces = [(r0, l0) for r0 in range(0, tm, CONV_CHUNK_ROWS) for l0 in range(0, CONV_DIM, LANES)]
    else:
        outs[5 + n_t][...] = u
        conv_pieces = []
    n_proj_left = [3 * len(GROUPS) + 2 * (D_MODEL // GROUP_WIDTH)]

    def conv_step():
        n = -(-len(conv_pieces) // n_proj_left[0])
        n_proj_left[0] -= 1
        if n:
            _conv_taps_aligned(ext_ref, cw_ref, yc_ref, CONV_HALO - (CONV_WIDTH - 1), conv_pieces[:n])
            del conv_pieces[:n]

    for g in range(len(GROUPS)):
        lo = g * GROUP_WIDTH
        q = seg(lo, lo + GROUP_WIDTH) * (HEAD_DIM ** -0.5)
        conv_step()
        k = seg(_Q_END + lo, _Q_END + lo + GROUP_WIDTH)
        conv_step()
        v = seg(_K_END + lo, _K_END + lo + GROUP_WIDTH)
        conv_step()
        for hf in range(GROUP_WIDTH // LANES):
            lanes = slice(hf * LANES, (hf + 1) * LANES)
            q_ref[g, hf] = q[:, lanes]
            k_ref[g, hf] = k[:, lanes]
            v_ref[g, hf] = v[:, lanes]
        if g in t_groups:
            j = t_groups.index(g)
            t_refs[2 * j][0] = k.T
            t_refs[2 * j + 1][0] = v.T
    for gate_ref, first in ((ga_ref, _UB_END), (gb_ref, _GA_END)):
        for c in range(0, D_MODEL, GROUP_WIDTH):
            gate_ref[:, c:c + GROUP_WIDTH] = jax.nn.sigmoid(seg(first + c, first + c + GROUP_WIDTH)).astype(BF16)
            conv_step()


def _win(x, norm_g, w_in, *, tm, rows_per_seq, t_groups, conv_w=None, cast_weights=()):
    rows = x.shape[0]
    n_seq = rows // rows_per_seq
    n_t = 2 * len(t_groups)
    tiles_per_seq = rows_per_seq // tm
    fuse_conv = conv_w is not None
    n_g = len(GROUPS)
    n_hf = GROUP_WIDTH // LANES
    out_shape = ([jax.ShapeDtypeStruct((n_g, n_hf, rows, LANES), F32)] * 3
                 + [jax.ShapeDtypeStruct((rows, D_MODEL), BF16)] * 2
                 + [jax.ShapeDtypeStruct((n_seq, GROUP_WIDTH, rows_per_seq), F32)] * n_t
                 + [jax.ShapeDtypeStruct((rows, CONV_DIM), F32)])
    t_spec = pl.BlockSpec((1, GROUP_WIDTH, tm), lambda i: (i // tiles_per_seq, 0, i % tiles_per_seq))
    out_specs = ([pl.BlockSpec((n_g, n_hf, tm, LANES), lambda i: (0, 0, i, 0))] * 3
                 + [_rows_spec(tm, D_MODEL)] * 2 + [t_spec] * n_t
                 + [_rows_spec(tm, CONV_DIM)])
    in_specs = [_rows_spec(tm, D_MODEL), _const_spec((1, D_MODEL)), _const_spec((D_MODEL, IN_WIDTH))]
    args = [x, norm_g, w_in]
    scratch = []
    if fuse_conv:
        in_specs.append(_const_spec((CONV_WIDTH, CONV_DIM)))
        args.append(conv_w)
        out_shape.append(jax.ShapeDtypeStruct((n_seq, CONV_HALO, CONV_DIM), F32))
        out_specs.append(pl.BlockSpec((1, CONV_HALO, CONV_DIM), lambda i: (i // tiles_per_seq, 0, 0)))
        scratch.append(pltpu.VMEM((CONV_HALO + tm + SUBLANES, CONV_DIM), F32))
    if cast_weights:
        assert fuse_conv and rows // tm >= WEIGHT_CAST_CHUNKS
        for cw in cast_weights:
            chunk = pl.BlockSpec((cw.shape[0] // WEIGHT_CAST_CHUNKS, cw.shape[1]),
                                 lambda i: (jnp.minimum(i, WEIGHT_CAST_CHUNKS - 1), 0))
            in_specs.append(chunk)
            args.append(cw)
            out_specs.append(chunk)
            out_shape.append(jax.ShapeDtypeStruct(cw.shape, BF16))
    return pl.pallas_call(
        functools.partial(_win_kernel, t_groups=tuple(t_groups), fuse_conv=fuse_conv, tm=tm,
                          tiles_per_seq=tiles_per_seq, n_cast=len(cast_weights)),
        grid=(rows // tm,),
        in_specs=in_specs,
        out_specs=out_specs,
        out_shape=out_shape,
        scratch_shapes=scratch,
        compiler_params=pltpu.CompilerParams(dimension_semantics=("arbitrary",),
                                             vmem_limit_bytes=V7X_VMEM_LIMIT_BYTES),
        name="w_in",
    )(*args)


def _rel_bucket(dist):
    n = jnp.maximum(dist, 0)
    max_exact = N_BUCKETS // 2
    nf = jnp.maximum(n, 1).astype(F32)
    large = max_exact + (jnp.log(nf / max_exact) / math.log(MAX_DISTANCE / max_exact)
                         * (N_BUCKETS - max_exact)).astype(jnp.int32)
    return jnp.where(n < max_exact, n, jnp.minimum(large, N_BUCKETS - 1))


def _tap_bias(rel_bias, g, dil):
    bias_g = rel_bias[:, g * HEADS_PER_GROUP:(g + 1) * HEADS_PER_GROUP]
    return bias_g[_rel_bucket(jnp.arange(SPAN, -1, -1) * dil)].astype(F32).T


def _toeplitz(vec, n_rows, n_cols):
    n_heads, period = vec.shape
    flat = jnp.tile(vec, (1, n_rows))[:, :n_rows * (period - 1)]
    return flat.reshape(n_heads, n_rows, period - 1)[:, :, :n_cols]


def _band_bias(tap_bias):
    vec = jnp.concatenate([tap_bias, jnp.full((HEADS_PER_GROUP, 2 * SPAN - 1), NEG_INF, F32)], axis=1)
    return _toeplitz(vec, SPAN, 2 * SPAN).reshape(HEADS_PER_GROUP * SPAN, 2 * SPAN)


def _cache_bias(tap_bias, dil, cache_len, n_new):
    rev = lax.pad(tap_bias, jnp.float32(NEG_INF), [(0, 0, 0), (cache_len + n_new - 1 - SPAN * dil, 0, dil - 1)])
    bias_c = _toeplitz(jnp.roll(rev, -(n_new - 1), axis=1), n_new, cache_len)
    near = jnp.concatenate([rev[:, cache_len:], jnp.full((HEADS_PER_GROUP, LANES), NEG_INF, F32)], axis=1)
    bias_n = _toeplitz(jnp.roll(near, -(n_new - 1), axis=1), n_new, LANES)
    return (bias_c.reshape(HEADS_PER_GROUP * n_new, cache_len), bias_n.reshape(HEADS_PER_GROUP * n_new, LANES))


MAX_ROW_STRIDE = 4


def _gather_residues(src, store, tmp_ref, dil, seq):
    if dil == 1:
        store(0, src[...])
    elif dil <= MAX_ROW_STRIDE:
        for r in range(dil):
            store(r, src[pl.ds(r, seq, stride=dil), :])
    else:
        inner = dil // MAX_ROW_STRIDE
        assert inner <= MAX_ROW_STRIDE and inner * MAX_ROW_STRIDE == dil
        n_part = seq * inner
        for c in range(MAX_ROW_STRIDE):
            tmp_ref[c * n_part:(c + 1) * n_part, :] = src[pl.ds(c, n_part, stride=MAX_ROW_STRIDE), :]
        for c in range(MAX_ROW_STRIDE):
            for c2 in range(inner):
                store(c + MAX_ROW_STRIDE * c2, tmp_ref[pl.ds(c * n_part + c2, seq, stride=inner), :])


def _scatter_residues(load, dst, tmp_ref, dil, seq):
    if dil == 1:
        dst[...] = load(0)
    elif dil <= MAX_ROW_STRIDE:
        for r in range(dil):
            dst[pl.ds(r, seq, stride=dil), :] = load(r)
    else:
        inner = dil // MAX_ROW_STRIDE
        assert inner <= MAX_ROW_STRIDE and inner * MAX_ROW_STRIDE == dil
        n_part = seq * inner
        for c in range(MAX_ROW_STRIDE):
            for c2 in range(inner):
                tmp_ref[pl.ds(c * n_part + c2, seq, stride=inner), :] = load(c + MAX_ROW_STRIDE * c2)
        for c in range(MAX_ROW_STRIDE):
            dst[pl.ds(c, n_part, stride=MAX_ROW_STRIDE), :] = tmp_ref[c * n_part:(c + 1) * n_part, :]


def _merge_groups(outs, lses):
    m = functools.reduce(jnp.maximum, lses)
    es = [jnp.exp(l - m) for l in lses]
    num = sum(e * o for e, o in zip(es, outs))
    return num * (1.0 / sum(es))


def _band_group(q_lo, q_hi, k_lo, k_hi, v_lo, v_hi, b_ref, o_dst, l_dst, qb_ref, kb_ref, vb_ref, ob_ref, lb_ref,
                tmp_ref, *, dil, seq):
    n_blk = seq // SPAN
    halves = (slice(0, LANES), slice(LANES, 2 * LANES))
    for dst, srcs in ((qb_ref, (q_lo, q_hi)), (kb_ref, (k_lo, k_hi)), (vb_ref, (v_lo, v_hi))):
        for lanes, src in zip(halves, srcs):
            def store(r, rows, dst=dst, lanes=lanes):
                dst[r * seq:(r + 1) * seq, lanes] = rows.astype(BF16)
            _gather_residues(src, store, tmp_ref, dil, seq)
    masks = _head_masks()

    def block(q, kb, vb, key_lo):
        qm = jnp.concatenate([jnp.where(masks[h], q, jnp.zeros_like(q)) for h in range(HEADS_PER_GROUP)], axis=0)
        s = _dot_nt(qm, kb) + b_ref[:, key_lo:]
        m = jnp.max(s, axis=-1, keepdims=True)
        p = jnp.exp(s - m)
        den = jnp.sum(p, axis=-1, keepdims=True)
        acc = _dot(p.astype(BF16), vb) * (1.0 / den)
        lse_rows = m + jnp.log(den)
        o = acc[0:SPAN]
        lse = jnp.broadcast_to(lse_rows[0:SPAN], (SPAN, GROUP_WIDTH))
        for h in range(1, HEADS_PER_GROUP):
            o = jnp.where(masks[h], acc[h * SPAN:(h + 1) * SPAN], o)
            lse = jnp.where(masks[h], lse_rows[h * SPAN:(h + 1) * SPAN], lse)
        return o, lse

    for r in range(dil):
        for j in range(n_blk):
            q_rows = slice(r * seq + j * SPAN, r * seq + (j + 1) * SPAN)
            k_rows = slice(r * seq + max(j - 1, 0) * SPAN, r * seq + (j + 1) * SPAN)
            o, lse = block(qb_ref[q_rows, :], kb_ref[k_rows, :], vb_ref[k_rows, :], SPAN if j == 0 else 0)
            ob_ref[q_rows, :] = o
            lb_ref[q_rows, :] = lse

    for src, dsts in ((ob_ref, o_dst), (lb_ref, l_dst)):
        for lanes, dst in zip(halves, dsts):
            _scatter_residues(lambda r, src=src, lanes=lanes: src[r * seq:(r + 1) * seq, lanes], dst, tmp_ref, dil, seq)


MERGE_CHUNK_ROWS = 256


def _band_attn_kernel(q_lo, q_hi, k_lo, k_hi, v_lo, v_hi, b_ref, out_ref,
                      qb_ref, kb_ref, vb_ref, ob_ref, lb_ref, tmp_ref, *nat_refs, seq_len):
    n_half = 2 * len(GROUPS)
    on_refs, ln_refs = nat_refs[:n_half], nat_refs[n_half:]
    g = pl.program_id(1)
    for gi, (_, dil) in enumerate(GROUPS):
        @pl.when(g == gi)
        def _(gi=gi, dil=dil):
            _band_group(q_lo, q_hi, k_lo, k_hi, v_lo, v_hi, b_ref,
                        on_refs[2 * gi:2 * gi + 2], ln_refs[2 * gi:2 * gi + 2],
                        qb_ref, kb_ref, vb_ref, ob_ref, lb_ref, tmp_ref, dil=dil, seq=seq_len // dil)

    @pl.when(g == len(GROUPS) - 1)
    def _():
        for r0 in range(0, seq_len, MERGE_CHUNK_ROWS):
            rows = slice(r0, r0 + MERGE_CHUNK_ROWS)
            for hf in (0, 1):
                merged = _merge_groups([on_refs[2 * gi + hf][rows, :] for gi in range(len(GROUPS))],
                                       [ln_refs[2 * gi + hf][rows, :] for gi in range(len(GROUPS))])
                out_ref[rows, hf * LANES:(hf + 1) * LANES] = merged.astype(BF16)


def _band_attn(q, k, v, biases, *, batch, seq_len):
    n_g = len(GROUPS)
    view = lambda a: a.reshape(n_g, 2, batch, seq_len, LANES)
    in_halves = [pl.BlockSpec((None, None, None, seq_len, LANES), lambda b, g, hf=hf: (g, hf, b, 0, 0))
                 for hf in (0, 1)]
    out = pl.pallas_call(
        functools.partial(_band_attn_kernel, seq_len=seq_len),
        grid=(batch, n_g),
        in_specs=in_halves * 3 + [pl.BlockSpec((None, HEADS_PER_GROUP * SPAN, 2 * SPAN), lambda b, g: (g, 0, 0))],
        out_specs=pl.BlockSpec((None, seq_len, GROUP_WIDTH), lambda b, g: (b, 0, 0)),
        out_shape=jax.ShapeDtypeStruct((batch, seq_len, GROUP_WIDTH), BF16),
        scratch_shapes=([pltpu.VMEM((seq_len, GROUP_WIDTH), BF16)] * 3 + [pltpu.VMEM((seq_len, GROUP_WIDTH), F32)] * 2
                        + [pltpu.VMEM((seq_len, LANES), F32)]
                        + [pltpu.VMEM((seq_len, LANES), F32)] * (4 * n_g)),
        compiler_params=pltpu.CompilerParams(dimension_semantics=("parallel", "arbitrary"),
                                             vmem_limit_bytes=V7X_VMEM_LIMIT_BYTES),
        name="band_attn",
    )(view(q), view(q), view(k), view(k), view(v), view(v), biases)
    return out.reshape(batch * seq_len, GROUP_WIDTH)


def _cache_attn_group(q, kn, vn, kc, vc, bias_c, bias_n, n_new):
    masks = _head_masks()
    pad = jnp.zeros((SPAN - n_new, GROUP_WIDTH), F32)
    qm = jnp.concatenate([jnp.where(masks[h], q, 0.0) for h in range(HEADS_PER_GROUP)], axis=0).astype(BF16)
    kn_p = jnp.concatenate([kn, pad], axis=0).astype(BF16)
    vn_p = jnp.concatenate([vn, pad], axis=0).astype(BF16)
    s_c = _dot(qm, kc.astype(BF16)) + bias_c
    s_n = _dot_nt(qm, kn_p) + bias_n
    m = jnp.maximum(jnp.max(s_c, axis=-1, keepdims=True), jnp.max(s_n, axis=-1, keepdims=True))
    p_c = jnp.exp(s_c - m)
    p_n = jnp.exp(s_n - m)
    den = jnp.sum(p_c, axis=-1, keepdims=True) + jnp.sum(p_n, axis=-1, keepdims=True)
    acc = _dot_nt(p_c.astype(BF16), vc.astype(BF16)) + _dot(p_n.astype(BF16), vn_p)
    acc = acc * (1.0 / den)
    lse_rows = m + jnp.log(den)
    o = jnp.zeros((n_new, GROUP_WIDTH), F32)
    lse = jnp.zeros((n_new, GROUP_WIDTH), F32)
    for h in range(HEADS_PER_GROUP):
        rows = slice(h * n_new, (h + 1) * n_new)
        o = jnp.where(masks[h], acc[rows], o)
        lse = jnp.where(masks[h], lse_rows[rows], lse)
    return o, lse


def _conv_taps(ext_ref, w_ref, first_row, n_rows):
    acc = jnp.zeros((n_rows, CONV_DIM), F32)
    for j in range(CONV_WIDTH):
        acc = acc + ext_ref[pl.ds(first_row + j, n_rows), :] * w_ref[j:j + 1, :]
    return acc


CONV_CHUNK_ROWS = 64


def _conv_taps_aligned(ext_ref, w_ref, y_ref, first_row, pieces):
    for r0, l0 in pieces:
        lanes = slice(l0, l0 + LANES)
        acc = None
        for b in range(SUBLANES):
            part = None
            for a in range((first_row + CONV_WIDTH - 1 - b) // SUBLANES + 1):
                j = SUBLANES * a + b - first_row
                if j < 0:
                    continue
                term = ext_ref[pl.ds(r0 + SUBLANES * a, CONV_CHUNK_ROWS + SUBLANES), lanes] * w_ref[j:j + 1, lanes]
                part = term if part is None else part + term
            part = part[b:b + CONV_CHUNK_ROWS]
            acc = part if acc is None else acc + part
        y_ref[r0:r0 + CONV_CHUNK_ROWS, lanes] = acc


def _conv_sample_kernel(u_ref, st_ref, w_ref, y_ref, so_ref, ext_ref, *, n_new, n_seq):
    hist = CONV_WIDTH - 1
    for b in range(n_seq):
        ext_ref[b, 0:hist, :] = st_ref[b]
        ext_ref[b, hist:hist + n_new, :] = u_ref[b]
        y_ref[b] = _conv_taps(ext_ref.at[b], w_ref, 0, n_new)
        so_ref[b] = ext_ref[b, n_new:n_new + hist, :]


def _conv_sample(u, state, w, *, batch, n_new, seqs_per_step=8):
    hist = CONV_WIDTH - 1
    u3 = u.reshape(batch, n_new, CONV_DIM)
    n_seq = math.gcd(batch, seqs_per_step)
    y, st = pl.pallas_call(
        functools.partial(_conv_sample_kernel, n_new=n_new, n_seq=n_seq),
        grid=(batch // n_seq,),
        in_specs=[pl.BlockSpec((n_seq, n_new, CONV_DIM), lambda b: (b, 0, 0)),
                  pl.BlockSpec((n_seq, hist, CONV_DIM), lambda b: (b, 0, 0)),
                  pl.BlockSpec((CONV_WIDTH, CONV_DIM), lambda b: (0, 0))],
        out_specs=[pl.BlockSpec((n_seq, n_new, CONV_DIM), lambda b: (b, 0, 0)),
                   pl.BlockSpec((n_seq, hist, CONV_DIM), lambda b: (b, 0, 0))],
        out_shape=[jax.ShapeDtypeStruct((batch, n_new, CONV_DIM), F32),
                   jax.ShapeDtypeStruct((batch, hist, CONV_DIM), F32)],
        scratch_shapes=[pltpu.VMEM((n_seq, hist + n_new + 2, CONV_DIM), F32)],
        compiler_params=_params(1),
        name="conv_sample",
    )(u3, state, w)
    return y.reshape(batch * n_new, CONV_DIM), st


def _mix_branches(attn, yconv, gate_a, gate_b, x, wap_ref, cb_ref, lg_ref, lb_ref, wcp_ref, wo_ref):
    a_branch = _dot(attn, wap_ref[...])
    y = yconv + cb_ref[...]
    mu = jnp.mean(y, axis=-1, keepdims=True)
    yc = y - mu
    var = jnp.mean(yc * yc, axis=-1, keepdims=True)
    y = yc * lax.rsqrt(var + EPS) * lg_ref[...] + lb_ref[...]
    y = y * jax.nn.sigmoid(y)
    c_branch = _dot(y.astype(BF16), wcp_ref[...])
    merged = gate_a * a_branch + gate_b * c_branch
    return x + _dot(merged.astype(BF16), wo_ref[...])


def _xattn_query(x, xg_ref, wxq_ref):
    hq = _rms(x, xg_ref[...]).astype(BF16)
    return (_dot(hq, wxq_ref[...]) * (X_HEAD_DIM ** -0.5)).astype(BF16)


def _xattn_heads(q, k_ref, v_ref):
    outs = []
    for h in range(X_HEADS):
        lanes = slice(h * X_HEAD_DIM, (h + 1) * X_HEAD_DIM)
        s = _dot_nt(q[:, lanes], k_ref[:, lanes])
        m = jnp.max(s, axis=-1, keepdims=True)
        p = jnp.exp(s - m)
        den = jnp.sum(p, axis=-1, keepdims=True)
        outs.append((_dot(p.astype(BF16), v_ref[:, lanes]) * (1.0 / den)).astype(BF16))
    return jnp.concatenate(outs, axis=-1)


def _mix_sample_kernel(at_ref, yc_ref, ga_ref, gb_ref, x_ref, wap_ref, cb_ref, lg_ref, lb_ref, wcp_ref, wo_ref,
                       xg_ref, wxq_ref, x_out, q_out):
    x = _mix_branches(at_ref[...], yc_ref[...], ga_ref[...], gb_ref[...], x_ref[...],
                      wap_ref, cb_ref, lg_ref, lb_ref, wcp_ref, wo_ref)
    x_out[...] = x
    q_out[...] = _xattn_query(x, xg_ref, wxq_ref)


def _mix_prompt_kernel(at_ref, yc_ref, ga_ref, gb_ref, x_ref, wap_ref, cb_ref, lg_ref, lb_ref, wcp_ref, wo_ref,
                       xg_ref, wxq_ref, mk_ref, mv_ref, wxo_ref, x_out):
    x = _mix_branches(at_ref[...], yc_ref[...], ga_ref[...], gb_ref[...], x_ref[...],
                      wap_ref, cb_ref, lg_ref, lb_ref, wcp_ref, wo_ref)
    attn = _xattn_heads(_xattn_query(x, xg_ref, wxq_ref), mk_ref, mv_ref)
    x_out[...] = x + _dot(attn, wxo_ref[...])


def _mix_weight_specs():
    return [_const_spec((GROUP_WIDTH, D_MODEL)), _const_spec((1, CONV_DIM)), _const_spec((1, CONV_DIM)),
            _const_spec((1, CONV_DIM)), _const_spec((CONV_DIM, D_MODEL)), _const_spec((D_MODEL, D_MODEL)),
            _const_spec((1, D_MODEL)), _const_spec((D_MODEL, D_MODEL))]


def _mix_weights(w):
    return (w['w_attn_proj'], w['conv_dw_b'], w['conv_ln_g'], w['conv_ln_b'], w['w_conv_proj'], w['w_o'],
            w['xattn_norm'], w['w_xq'])


def _mix_sample(attn, yconv, ga, gb, x, w, *, tm):
    rows = x.shape[0]
    wide = _rows_spec(tm, D_MODEL)
    return pl.pallas_call(
        _mix_sample_kernel,
        grid=(rows // tm,),
        in_specs=[_rows_spec(tm, GROUP_WIDTH), _rows_spec(tm, CONV_DIM), wide, wide, wide] + _mix_weight_specs(),
        out_specs=[wide, wide],
        out_shape=[jax.ShapeDtypeStruct((rows, D_MODEL), F32), jax.ShapeDtypeStruct((rows, D_MODEL), BF16)],
        compiler_params=_params(1),
        name="mix_sample",
    )(attn, yconv, ga, gb, x, *_mix_weights(w))


def _mix_prompt(attn, yconv, ga, gb, x, mk, mv, w, *, tm, seq_len):
    rows = x.shape[0]
    tiles_per_seq = seq_len // tm
    wide = _rows_spec(tm, D_MODEL)
    mem = pl.BlockSpec((N_MEM, D_MODEL), lambda i: (i // tiles_per_seq, 0))
    return pl.pallas_call(
        _mix_prompt_kernel,
        grid=(rows // tm,),
        in_specs=([_rows_spec(tm, GROUP_WIDTH), _rows_spec(tm, CONV_DIM), wide, wide, wide]
                  + _mix_weight_specs() + [mem, mem, _const_spec((D_MODEL, D_MODEL))]),
        out_specs=wide,
        out_shape=jax.ShapeDtypeStruct((rows, D_MODEL), F32),
        compiler_params=_params(1),
        name="mix_prompt",
    )(attn, yconv, ga, gb, x, *_mix_weights(w), mk, mv, w['w_xo'])


def _memkv_kernel(m_ref, g_ref, w_ref, k5_ref, v5_ref, k_ref, v_ref, *, n_seq):
    h = _rms(m_ref[...], g_ref[...]).astype(BF16)
    for hd in range(X_HEADS):
        lo = hd * X_HEAD_DIM
        k = _dot(h, w_ref[:, lo:lo + X_HEAD_DIM])
        v = _dot(h, w_ref[:, D_MODEL + lo:D_MODEL + lo + X_HEAD_DIM])
        for b in range(n_seq):
            k5_ref[0, b, :, hd, :] = k[b * N_MEM:(b + 1) * N_MEM]
            v5_ref[0, b, :, hd, :] = v[b * N_MEM:(b + 1) * N_MEM]
        k_ref[:, lo:lo + X_HEAD_DIM] = k.astype(BF16)
        v_ref[:, lo:lo + X_HEAD_DIM] = v.astype(BF16)


def _memkv(mem, norm_g, w_xkv, *, batch, seqs_per_step=4):
    n_seq = math.gcd(batch, seqs_per_step)
    spec5 = pl.BlockSpec((1, n_seq, N_MEM, X_HEADS, X_HEAD_DIM), lambda b: (0, b, 0, 0, 0))
    rows = _rows_spec(n_seq * N_MEM, D_MODEL)
    return pl.pallas_call(
        functools.partial(_memkv_kernel, n_seq=n_seq),
        grid=(batch // n_seq,),
        in_specs=[rows, _const_spec((1, D_MODEL)), _const_spec((D_MODEL, 2 * D_MODEL))],
        out_specs=[spec5, spec5, rows, rows],
        out_shape=([jax.ShapeDtypeStruct((1, batch, N_MEM, X_HEADS, X_HEAD_DIM), F32)] * 2
                   + [jax.ShapeDtypeStruct((batch * N_MEM, D_MODEL), BF16)] * 2),
        compiler_params=_params(1),
        name="memory_kv",
    )(mem, norm_g, w_xkv)


def _xattn_cached_kernel(q_ref, k_lo, k_hi, v_lo, v_hi, mask_ref, o_ref, *, n_q, n_seq):
    for b in range(n_seq):
        flat = lambda ref: ref[0, b].reshape(N_MEM * X_HEADS, LANES)
        k_all = jnp.concatenate([flat(k_lo), flat(k_hi)], axis=1).astype(BF16)
        v_all = jnp.concatenate([flat(v_lo), flat(v_hi)], axis=1).astype(BF16)
        q = q_ref[b].astype(F32)
        qs = jnp.concatenate([q[:, h * X_HEAD_DIM:(h + 1) * X_HEAD_DIM] for h in range(X_HEADS)],
                             axis=0).astype(BF16)
        s = _dot_nt(qs, k_all) + mask_ref[...]
        m = jnp.max(s, axis=-1, keepdims=True)
        p = jnp.exp(s - m)
        den = jnp.sum(p, axis=-1, keepdims=True)
        acc = _dot(p.astype(BF16), v_all) * (1.0 / den)
        o_ref[b] = jnp.concatenate([acc[h * n_q:(h + 1) * n_q] for h in range(X_HEADS)], axis=1).astype(BF16)


def _xattn_cached(q, mk, mv, *, batch, seqs_per_step=4):
    n_q = q.shape[1]
    n_seq = math.gcd(batch, seqs_per_step)
    q_spec = pl.BlockSpec((n_seq, n_q, D_MODEL), lambda b: (b, 0, 0))
    halves = [pl.BlockSpec((1, n_seq, N_MEM, X_HEADS, LANES), lambda b, hf=hf: (0, b, 0, 0, hf))
              for hf in range(X_HEAD_DIM // LANES)]
    own_head = (np.arange(N_MEM * X_HEADS)[None, :] % X_HEADS) == (np.arange(X_HEADS * n_q)[:, None] // n_q)
    mask = jnp.asarray(np.where(own_head, 0.0, NEG_INF), F32)
    return pl.pallas_call(
        functools.partial(_xattn_cached_kernel, n_q=n_q, n_seq=n_seq),
        grid=(batch // n_seq,),
        in_specs=[q_spec] + halves * 2 + [_const_spec(mask.shape)],
        out_specs=q_spec,
        out_shape=jax.ShapeDtypeStruct((batch, n_q, D_MODEL), BF16),
        compiler_params=_params(1),
        name="cross_attn_cached",
    )(q, mk, mk, mv, mv, mask)


def _to_positions_heads(a_t, batch, n_pos):
    return jnp.transpose(a_t.reshape(batch, HEADS_PER_GROUP, HEAD_DIM, n_pos), (0, 3, 1, 2))[None]


def _to_feature_major(a, batch, n_pos):
    return jnp.transpose(a[0], (0, 2, 3, 1)).reshape(batch, GROUP_WIDTH, n_pos)


def kernel(x_prompt, x_sample, mem_prompt, cache_win0_k, cache_win0_v, cache_win1_k, cache_win1_v, cache_win2_k, cache_win2_v, state_conv, cache_mem_k, cache_mem_v, rel_bias, ffn1_norm, ffn1_w_gate, ffn1_w_up, ffn1_w_down, mix_norm, w_in, w_attn_proj, conv_dw_w, conv_dw_b, conv_ln_g, conv_ln_b, w_conv_proj, w_o, xattn_norm, mem_norm, w_xq, w_xkv, w_xo, ffn2_norm, ffn2_w_gate, ffn2_w_up, ffn2_w_down, final_norm):
    batch, seq_len, _ = x_prompt.shape
    dec_batch, dec_seq, _ = x_sample.shape
    n_g = len(GROUPS)
    assert ffn1_norm.shape[0] == 1, "single layer"
    mat = lambda a: a[0].astype(BF16)
    vec = lambda a: a[0].reshape(1, -1)
    w = dict(ffn1_norm=vec(ffn1_norm), ffn1_w_gate=mat(ffn1_w_gate), ffn1_w_up=mat(ffn1_w_up),
             ffn1_w_down=mat(ffn1_w_down), mix_norm=vec(mix_norm), w_in=mat(w_in),
             conv_dw_w=conv_dw_w[0], conv_dw_b=vec(conv_dw_b), conv_ln_g=vec(conv_ln_g), conv_ln_b=vec(conv_ln_b),
             xattn_norm=vec(xattn_norm), mem_norm=vec(mem_norm), ffn2_norm=vec(ffn2_norm))
    late = dict(w_attn_proj=w_attn_proj[0], w_conv_proj=w_conv_proj[0], w_o=w_o[0], w_xq=w_xq[0], w_xkv=w_xkv[0],
                w_xo=w_xo[0], ffn2_w_gate=ffn2_w_gate[0], ffn2_w_up=ffn2_w_up[0], ffn2_w_down=ffn2_w_down[0])
    final_g = final_norm.reshape(1, -1)
    tap_biases = [_tap_bias(rel_bias, g, dil) for g, (_, dil) in enumerate(GROUPS)]

    def head(x, tm, rows_per_seq, t_groups, conv_w=None, cache_update=None, cast_weights=()):
        x1 = _ffn(x, w['ffn1_norm'], w['ffn1_w_gate'], w['ffn1_w_up'], w['ffn1_w_down'], tm=tm,
                  cache_update=cache_update)
        new_caches = []
        if cache_update is not None:
            x1, *new_caches = x1
        outs = _win(x1, w['mix_norm'], w['w_in'], tm=tm, rows_per_seq=rows_per_seq, t_groups=t_groups, conv_w=conv_w,
                    cast_weights=cast_weights)
        n_t = 2 * len(t_groups)
        return x1, outs[0], outs[1], outs[2], outs[3], outs[4], outs[5:5 + n_t], outs[5 + n_t:], new_caches

    rows_s = dec_batch * dec_seq
    x1_s, q_s, k_s, v_s, ga_s, gb_s, kv_t_s, (u_s,), _ = head(x_sample.reshape(rows_s, D_MODEL), rows_s, rows_s,
                                                              list(range(n_g)))
    caches_k = [_to_feature_major(c, dec_batch, c.shape[2]) for c in (cache_win0_k, cache_win1_k, cache_win2_k)]
    caches_v = [_to_feature_major(c, dec_batch, c.shape[2]) for c in (cache_win0_v, cache_win1_v, cache_win2_v)]

    rows_p = batch * seq_len
    tm_p = PROMPT_ROW_TILE
    full_groups = [g for g, (window, _) in enumerate(GROUPS) if window >= seq_len]
    new3 = lambda a: [jnp.concatenate([a[g, 0], a[g, 1]], axis=-1).reshape(dec_batch, dec_seq, GROUP_WIDTH)
                      for g in range(n_g)]
    cache_biases = [_cache_bias(tap_biases[g], dil, caches_k[g].shape[2], dec_seq) for g, (_, dil) in enumerate(GROUPS)]
    cache_update = (caches_k + caches_v, list(kv_t_s[0::2]) + list(kv_t_s[1::2]), new3(q_s), new3(k_s), new3(v_s),
                    [bc for bc, _ in cache_biases], [bn for _, bn in cache_biases], dec_seq)
    x1, q, k, v, ga, gb, kv_t, (yconv, u_tail, *late_bf16), (*new_caches, attn_s) = head(
        x_prompt.reshape(rows_p, D_MODEL), tm_p, seq_len, full_groups, conv_w=w['conv_dw_w'], cache_update=cache_update,
        cast_weights=tuple(late.values()))
    w.update(zip(late.keys(), late_bf16))
    attn = _band_attn(q, k, v, jnp.stack([_band_bias(t) for t in tap_biases]), batch=batch, seq_len=seq_len)
    p_mem_k, p_mem_v, mk_p, mv_p = _memkv(mem_prompt.reshape(batch * N_MEM, D_MODEL), w['mem_norm'], w['w_xkv'],
                                          batch=batch)
    x3 = _mix_prompt(attn, yconv, ga, gb, x1, mk_p, mv_p, w, tm=tm_p, seq_len=seq_len)
    y_prompt = _ffn(x3, w['ffn2_norm'], w['ffn2_w_gate'], w['ffn2_w_up'], w['ffn2_w_down'], tm=tm_p,
                    final_g=final_g)

    p_win = []
    for g, (window, _) in enumerate(GROUPS):
        keep = min(window, seq_len)
        if g in full_groups:
            j = full_groups.index(g)
            p_win += [_to_positions_heads(kv_t[2 * j], batch, seq_len), _to_positions_heads(kv_t[2 * j + 1], batch, seq_len)]
        else:
            for a in (k, v):
                tail = lax.slice(a.reshape(n_g, 2, batch, seq_len, LANES), (g, 0, 0, seq_len - keep, 0),
                                 (g + 1, 2, batch, seq_len, LANES))
                tail = jnp.transpose(tail[0], (1, 2, 0, 3))
                p_win.append(tail.reshape(1, batch, keep, HEADS_PER_GROUP, HEAD_DIM))
    p_conv = u_tail[:, CONV_HALO - (CONV_WIDTH - 1):][None]

    yconv, s_conv = _conv_sample(u_s, state_conv[0], w['conv_dw_w'], batch=dec_batch, n_new=dec_seq)
    x2, xq = _mix_sample(attn_s.reshape(rows_s, GROUP_WIDTH), yconv, ga_s, gb_s, x1_s, w, tm=rows_s)
    xo = _xattn_cached(xq.reshape(dec_batch, dec_seq, D_MODEL), cache_mem_k, cache_mem_v, batch=dec_batch)
    y_sample = _ffn(x2, w['ffn2_norm'], w['ffn2_w_gate'], w['ffn2_w_up'], w['ffn2_w_down'], tm=rows_s,
                    pre=(xo.reshape(rows_s, D_MODEL), w['w_xo']), final_g=final_g)

    s_win = []
    for g in range(n_g):
        for a in (new_caches[g], new_caches[n_g + g]):
            s_win.append(_to_positions_heads(a, dec_batch, a.shape[2]))

    return (y_prompt.reshape(batch, seq_len, D_MODEL), y_sample.reshape(dec_batch, dec_seq, D_MODEL),
            *p_win, p_conv, p_mem_k, p_mem_v, *s_win, s_conv[None])
```
